```python
import math
import jax, jax.numpy as jnp
from jax import lax
import numpy as np

D_MODEL = 1024
BATCH = 16
SEQ = 2048
DEPTH = 4

MLA_HEADS = 8
MLA_NOPE = 64
MLA_ROPE = 32
MLA_V = 64
MLA_Q_LORA = 384
MLA_KV_LORA = 256
ROPE_BASE = 10000.0
DIFF_HEADS = 8
DIFF_HD = 64
DIFF_VD = 2 * DIFF_HD
D_FF = 2816
N_EXPERTS = 8
TOP_K = 2
N_DENSE = (DEPTH + 1) // 2
N_MOE = DEPTH // 2
Q_BLOCK = 128
EXPERT_BLOCK = 128
DEEPNORM_ALPHA = (2.0 * DEPTH) ** 0.25
DEEPNORM_BETA = (8.0 * DEPTH) ** -0.25
LN_EPS = 1e-5
RMS_EPS = 1e-6
SPLIT_SIZES = (MLA_Q_LORA, MLA_KV_LORA, MLA_ROPE,
               DIFF_HEADS * 2 * DIFF_HD, DIFF_HEADS * 2 * DIFF_HD, DIFF_HEADS * DIFF_VD,
               D_MODEL, D_MODEL)
N_IN = sum(SPLIT_SIZES)

kernel_name = 'hybrid_mla_diffattn_moe_encoder'


def _layernorm(x, g, b):
    xf = x.astype(jnp.float32)
    mu = jnp.mean(xf, axis=-1, keepdims=True)
    var = jnp.mean(jnp.square(xf - mu), axis=-1, keepdims=True)
    return ((xf - mu) * lax.rsqrt(var + LN_EPS) * g.astype(jnp.float32) + b.astype(jnp.float32)).astype(x.dtype)


def _rmsnorm(x, g):
    xf = x.astype(jnp.float32)
    ms = jnp.mean(jnp.square(xf), axis=-1, keepdims=True)
    return (xf * lax.rsqrt(ms + RMS_EPS) * g.astype(jnp.float32)).astype(x.dtype)


def _split_cols(z):
    out, off = [], 0
    for n in SPLIT_SIZES:
        out.append(z[..., off:off + n])
        off += n
    return out


def _rope_tables(positions):
    inv_freq = ROPE_BASE ** (-jnp.arange(0, MLA_ROPE, 2, dtype=jnp.float32) / MLA_ROPE)
    ang = positions.astype(jnp.float32)[..., None] * inv_freq
    return jnp.cos(ang), jnp.sin(ang)


def _rope(x, cos, sin):
    half = x.shape[-1] // 2
    x1 = x[..., :half].astype(jnp.float32)
    x2 = x[..., half:].astype(jnp.float32)
    return jnp.concatenate([x1 * cos - x2 * sin, x1 * sin + x2 * cos], axis=-1).astype(x.dtype)


def _alibi_slopes():
    return 2.0 ** (-8.0 * jnp.arange(1, DIFF_HEADS + 1, dtype=jnp.float32) / DIFF_HEADS)


def _query_blocks(a):
    b, s = a.shape[0], a.shape[1]
    a = a.reshape(b, s // Q_BLOCK, Q_BLOCK, *a.shape[2:])
    return jnp.moveaxis(a, 1, 0)


def _merge_blocks(a):
    a = jnp.moveaxis(a, 0, 1)
    return a.reshape(a.shape[0], a.shape[1] * a.shape[2], *a.shape[3:])


def _mla(q_lat, kv_lat, k_rope_raw, q_norm_g, w_q_up, kv_norm_g, w_kv_up, cos, sin):
    b, s = q_lat.shape[0], q_lat.shape[1]
    q = (_rmsnorm(q_lat, q_norm_g) @ w_q_up).reshape(b, s, MLA_HEADS, MLA_NOPE + MLA_ROPE)
    q_nope = q[..., :MLA_NOPE]
    q_rope = _rope(q[..., MLA_NOPE:], cos[:, :, None, :], sin[:, :, None, :])
    kv = (_rmsnorm(kv_lat, kv_norm_g) @ w_kv_up).reshape(b, s, MLA_HEADS, MLA_NOPE + MLA_V)
    k_nope = kv[..., :MLA_NOPE]
    v = kv[..., MLA_NOPE:]
    k_rope = _rope(k_rope_raw, cos, sin)
    scale = (MLA_NOPE + MLA_ROPE) ** -0.5

    def block(args):
        qn, qr = args
        sc = jnp.einsum('bqhd,bkhd->bhqk', qn, k_nope) + jnp.einsum('bqhr,bkr->bhqk', qr, k_rope)
        p = jax.nn.softmax(sc.astype(jnp.float32) * scale, axis=-1)
        return jnp.einsum('bhqk,bkhd->bqhd', p.astype(v.dtype), v)

    o = _merge_blocks(lax.map(block, (_query_blocks(q_nope), _query_blocks(q_rope))))
    return o.reshape(b, s, MLA_HEADS * MLA_V)


def _diff_attention(q, k, v, posf, lam, lam_init, norm_g, slopes):
    b, s = q.shape[0], q.shape[1]
    scale = DIFF_HD ** -0.5

    def block(args):
        qb, pq = args
        sc = jnp.einsum('bqhmd,bkhmd->mbhqk', qb, k).astype(jnp.float32) * scale
        dist = jnp.abs(pq[:, :, None] - posf[:, None, :])
        sc = sc - slopes[None, None, :, None, None] * dist[None, :, None, :, :]
        p = jax.nn.softmax(sc, axis=-1)
        a = p[0] - lam * p[1]
        return jnp.einsum('bhqk,bkhd->bqhd', a.astype(v.dtype), v)

    o = _merge_blocks(lax.map(block, (_query_blocks(q), _query_blocks(posf))))
    o = _rmsnorm(o, norm_g.reshape(DIFF_HEADS, DIFF_VD)) * (1.0 - lam_init)
    return o.reshape(b, s, DIFF_HEADS * DIFF_VD)


def _swiglu(h, w1, w3, w2):
    return (jax.nn.silu(h @ w1) * (h @ w3)) @ w2


def _moe(h, router_w, router_b, w1, w3, w2):
    b, s, d = h.shape
    t = b * s
    hf = h.reshape(t, d)
    logits = (hf @ router_w).astype(jnp.float32) + router_b.astype(jnp.float32)
    top_logits, top_idx = lax.top_k(logits, TOP_K)
    top_w = jax.nn.softmax(top_logits, axis=-1)
    expert_ids = top_idx.reshape(-1)
    token_ids = jnp.arange(t * TOP_K, dtype=jnp.int32) // TOP_K
    gates = top_w.reshape(-1)
    order = jnp.argsort(expert_ids)
    sorted_e = expert_ids[order]
    sorted_tok = token_ids[order]
    sorted_g = gates[order]
    counts = jnp.zeros((N_EXPERTS,), jnp.int32).at[expert_ids].add(1)
    starts = jnp.cumsum(counts) - counts
    padded = (counts + EXPERT_BLOCK - 1) // EXPERT_BLOCK * EXPERT_BLOCK
    pstarts = jnp.cumsum(padded) - padded
    pends = pstarts + padded
    dest = pstarts[sorted_e] + (jnp.arange(t * TOP_K, dtype=jnp.int32) - starts[sorted_e])
    n_rows = t * TOP_K + N_EXPERTS * EXPERT_BLOCK
    n_blk = n_rows // EXPERT_BLOCK
    buf_tok = jnp.full((n_rows,), t, jnp.int32).at[dest].set(sorted_tok)
    buf_g = jnp.zeros((n_rows,), jnp.float32).at[dest].set(sorted_g)
    blk_e = jnp.minimum(jnp.searchsorted(pends, jnp.arange(n_blk, dtype=jnp.int32) * EXPERT_BLOCK, side='right'),
                        N_EXPERTS - 1).astype(jnp.int32)
    xpad = jnp.concatenate([hf, jnp.zeros((1, d), hf.dtype)], axis=0)
    xb = xpad[buf_tok].reshape(n_blk, EXPERT_BLOCK, d)

    def expert_block(args):
        xe, e = args
        return (jax.nn.silu(xe @ w1[e]) * (xe @ w3[e])) @ w2[e]

    y = lax.map(expert_block, (xb, blk_e)).reshape(n_rows, d)
    y = y * buf_g[:, None].astype(y.dtype)
    out = jnp.zeros((t + 1, d), y.dtype).at[buf_tok].add(y)[:t]
    return out.reshape(b, s, d)


def setup_inputs(seed: int = 0) -> dict:
    key = jax.random.key(seed)
    keys = list(jax.random.split(key, 40))
    L, D, F, E = DEPTH, D_MODEL, D_FF, N_EXPERTS

    def nrm(shape, scale):
        return jax.random.normal(keys.pop(), shape, jnp.float32) * scale

    def gain(shape):
        return 1.0 + nrm(shape, 0.02)

    x = nrm((BATCH, SEQ, D), 1.0)
    c = nrm((BATCH, D), 1.0)
    offsets = jax.random.randint(keys.pop(), (BATCH, 1), 0, 4096, dtype=jnp.int32)
    positions = offsets + jnp.arange(SEQ, dtype=jnp.int32)[None, :]
    mla_out = MLA_HEADS * MLA_V
    diff_out = DIFF_HEADS * DIFF_VD
    return {
        'x': x,
        'c': c,
        'positions': positions,
        'w_ada': nrm((L, D, 6 * D), D ** -0.5),
        'b_ada': nrm((L, 6 * D), 0.02),
        'w_in': nrm((L, D, N_IN), D ** -0.5),
        'q_norm_g': gain((L, MLA_Q_LORA)),
        'w_q_up': nrm((L, MLA_Q_LORA, MLA_HEADS * (MLA_NOPE + MLA_ROPE)), MLA_Q_LORA ** -0.5),
        'kv_norm_g': gain((L, MLA_KV_LORA)),
        'w_kv_up': nrm((L, MLA_KV_LORA, MLA_HEADS * (MLA_NOPE + MLA_V)), MLA_KV_LORA ** -0.5),
        'lambda_q1': nrm((L, DIFF_HD), 0.1),
        'lambda_k1': nrm((L, DIFF_HD), 0.1),
        'lambda_q2': nrm((L, DIFF_HD), 0.1),
        'lambda_k2': nrm((L, DIFF_HD), 0.1),
        'diff_norm_g': gain((L, diff_out)),
        'w_br_mla': nrm((L, mla_out, D), mla_out ** -0.5 * DEEPNORM_BETA),
        'w_br_diff': nrm((L, diff_out, D), diff_out ** -0.5 * DEEPNORM_BETA),
        'w_out': nrm((L, D, D), D ** -0.5 * DEEPNORM_BETA),
        'ln1_g': gain((L, D)),
        'ln1_b': nrm((L, D), 0.02),
        'ln2_g': gain((L, D)),
        'ln2_b': nrm((L, D), 0.02),
        'ffn_w1': nrm((N_DENSE, D, F), D ** -0.5),
        'ffn_w3': nrm((N_DENSE, D, F), D ** -0.5),
        'ffn_w2': nrm((N_DENSE, F, D), F ** -0.5 * DEEPNORM_BETA),
        'router_w': nrm((N_MOE, D, E), D ** -0.5),
        'router_b': nrm((N_MOE, E), 0.01),
        'moe_w1': nrm((N_MOE, E, D, F), D ** -0.5),
        'moe_w3': nrm((N_MOE, E, D, F), D ** -0.5),
        'moe_w2': nrm((N_MOE, E, F, D), F ** -0.5 * DEEPNORM_BETA),
    }


def reference(x, c, positions, w_ada, b_ada, w_in, q_norm_g, w_q_up, kv_norm_g, w_kv_up,
              lambda_q1, lambda_k1, lambda_q2, lambda_k2, diff_norm_g, w_br_mla, w_br_diff, w_out,
              ln1_g, ln1_b, ln2_g, ln2_b, ffn_w1, ffn_w3, ffn_w2, router_w, router_b,
              moe_w1, moe_w3, moe_w2):
    b, s, _ = x.shape
    cos, sin = _rope_tables(positions)
    posf = positions.astype(jnp.float32)
    slopes = _alibi_slopes()
    cond = jax.nn.silu(c)
    for l in range(DEPTH):
        mod = cond @ w_ada[l] + b_ada[l]
        sh1, sc1, g1, sh2, sc2, g2 = jnp.split(mod[:, None, :], 6, axis=-1)
        h = x * (1.0 + sc1) + sh1
        z = h @ w_in[l]
        q_lat, kv_lat, k_rope, dq, dk, dv, gta, gtb = _split_cols(z)
        ya = _mla(q_lat, kv_lat, k_rope, q_norm_g[l], w_q_up[l], kv_norm_g[l], w_kv_up[l], cos, sin) @ w_br_mla[l]
        lam_init = 0.8 - 0.6 * math.exp(-0.3 * l)
        lam = (jnp.exp(jnp.sum(lambda_q1[l].astype(jnp.float32) * lambda_k1[l].astype(jnp.float32)))
               - jnp.exp(jnp.sum(lambda_q2[l].astype(jnp.float32) * lambda_k2[l].astype(jnp.float32)))
               + lam_init)
        yb = _diff_attention(dq.reshape(b, s, DIFF_HEADS, 2, DIFF_HD), dk.reshape(b, s, DIFF_HEADS, 2, DIFF_HD),
                             dv.reshape(b, s, DIFF_HEADS, DIFF_VD), posf, lam, lam_init, diff_norm_g[l], slopes) @ w_br_diff[l]
        mix = (jax.nn.sigmoid(gta) * ya + jax.nn.sigmoid(gtb) * yb) @ w_out[l]
        x = _layernorm(DEEPNORM_ALPHA * x + g1 * mix, ln1_g[l], ln1_b[l])
        h = x * (1.0 + sc2) + sh2
        if l % 2 == 0:
            f = _swiglu(h, ffn_w1[l // 2], ffn_w3[l // 2], ffn_w2[l // 2])
        else:
            f = _moe(h, router_w[l // 2], router_b[l // 2], moe_w1[l // 2], moe_w3[l // 2], moe_w2[l // 2])
        x = _layernorm(DEEPNORM_ALPHA * x + g2 * f, ln2_g[l], ln2_b[l])
    return x
```

```python
import functools
import math

import jax
import jax.numpy as jnp
from jax import lax
from jax.experimental import pallas as pl
from jax.experimental.pallas import tpu as pltpu

BF16 = jnp.bfloat16
F32 = jnp.float32

D_MODEL = 1024
MLA_HEADS = 8
MLA_NOPE = 64
MLA_ROPE = 32
MLA_V = 64
MLA_Q_LORA = 384
MLA_KV_LORA = 256
ROPE_BASE = 10000.0
DIFF_HEADS = 8
DIFF_HD = 64
DIFF_VD = 2 * DIFF_HD
D_FF = 2816
N_EXPERTS = 8
TOP_K = 2
LN_EPS = 1e-5
RMS_EPS = 1e-6

HEAD_PAD = 128
LOG2E = 1.4426950408889634
MLA_QSCALE = (MLA_NOPE + MLA_ROPE) ** -0.5 * LOG2E
DIFF_QSCALE = DIFF_HD ** -0.5 * LOG2E

VMEM_LIMIT = 56 * 1024 * 1024

TS_PROJ = 256
TQ_ATTN = 256
TS_MIX = 512
TS_FFN = 256
TM_MOE = 256
TS_ROUTE = 512
TS_COMB = 256


def _cparams(n_axes):
    return pltpu.CompilerParams(dimension_semantics=("arbitrary",) * n_axes,
                                vmem_limit_bytes=VMEM_LIMIT)


def _const_spec(shape):
    nd = len(shape)
    return pl.BlockSpec(shape, lambda *_: (0,) * nd, pipeline_mode=pl.Buffered(1))


def _dot(a, b):
    return jnp.dot(a, b, preferred_element_type=F32)


def _dot_nt(a, b):
    return lax.dot_general(a, b, (((1,), (1,)), ((), ())), preferred_element_type=F32)


def _sigmoid(v):
    return 1.0 / (1.0 + jnp.exp(-v))


def _layernorm(r, g, b):
    mu = jnp.mean(r, axis=-1, keepdims=True)
    d = r - mu
    var = jnp.mean(d * d, axis=-1, keepdims=True)
    return d * lax.rsqrt(var + LN_EPS) * g + b


def _rms_rows(v, g):
    ms = jnp.mean(v * v, axis=-1, keepdims=True)
    return v * lax.rsqrt(ms + RMS_EPS) * g


def _ada_kernel(c_ref, w_ref, b_ref, o_ref):
    c = c_ref[...]
    cond = c * _sigmoid(c)
    o_ref[0] = _dot(cond.astype(BF16), w_ref[0].astype(BF16)) + b_ref[0]


def _ada_mod(c, w_ada, b_ada):
    n_l, d, n6 = w_ada.shape
    b = c.shape[0]
    tn = 1536
    return pl.pallas_call(
        _ada_kernel,
        grid=(n_l, n6 // tn),
        in_specs=[
            pl.BlockSpec((b, d), lambda l, j: (0, 0)),
            pl.BlockSpec((1, d, tn), lambda l, j: (l, 0, j)),
            pl.BlockSpec((1, 1, tn), lambda l, j: (l, 0, j)),
        ],
        out_specs=pl.BlockSpec((1, b, tn), lambda l, j: (l, 0, j)),
        out_shape=jax.ShapeDtypeStruct((n_l, b, n6), F32),
        compiler_params=_cparams(2),
        name="ada_mod",
    )(c, w_ada, b_ada.reshape(n_l, 1, n6))


def _inproj_kernel(x_ref, sc_ref, sh_ref, ct_ref, st_ref,
                   wlat_ref, wdq_ref, wdk_ref, wdv_ref, wga_ref, wgb_ref,
                   qg_ref, kvg_ref, wq_ref, wqs_ref, wkn_ref, wv_ref, esel_ref,
                   qm_ref, km_ref, vt_ref, dq1_ref, dq2_ref, dk_ref, dvt_ref, ga_ref, gb_ref):
    x = x_ref[0]
    h = (x * (1.0 + sc_ref[0]) + sh_ref[0]).astype(BF16)
    ct = ct_ref[0]
    st = st_ref[0]

    lat = _dot(h, wlat_ref[...])
    q_lat = lat[:, :MLA_Q_LORA]
    kv_lat = lat[:, MLA_Q_LORA:MLA_Q_LORA + MLA_KV_LORA]
    kr = lat[:, 640:768]
    krs = lat[:, 768:896]

    qn = _rms_rows(q_lat, qg_ref[...]).astype(BF16)
    q = _dot(qn, wq_ref[...])
    qs = _dot(qn, wqs_ref[...])
    c8 = jnp.concatenate([ct] * MLA_HEADS, axis=1)
    s8 = jnp.concatenate([st] * MLA_HEADS, axis=1)
    qr = ((q * c8 + qs * s8) * MLA_QSCALE).astype(BF16)
    for hd in range(MLA_HEADS):
        qm_ref[0, hd] = qr[:, hd * HEAD_PAD:(hd + 1) * HEAD_PAD]

    kvn = _rms_rows(kv_lat, kvg_ref[...]).astype(BF16)
    kn = _dot(kvn, wkn_ref[...])
    ck = pltpu.roll(ct, 64, axis=1)
    sk = pltpu.roll(st, 64, axis=1)
    kro = (kr * ck + krs * sk).astype(BF16)
    kcat = (kn + _dot(kro, esel_ref[...])).astype(BF16)
    for hd in range(MLA_HEADS):
        km_ref[0, hd] = kcat[:, hd * HEAD_PAD:(hd + 1) * HEAD_PAD]
    v = _dot(kvn, wv_ref[...])
    vt = v.T.astype(BF16)
    for hd in range(MLA_HEADS):
        vt_ref[0, hd] = vt[hd * MLA_V:(hd + 1) * MLA_V, :]

    dq = _dot(h, wdq_ref[...]) * DIFF_QSCALE
    lane = lax.broadcasted_iota(jnp.int32, dq.shape, 1)
    first = (lane % HEAD_PAD) < DIFF_HD
    dq1 = jnp.where(first, dq, 0.0).astype(BF16)
    dq2 = jnp.where(first, 0.0, dq).astype(BF16)
    dk = _dot(h, wdk_ref[...]).astype(BF16)
    dvt = _dot(h, wdv_ref[...]).T.astype(BF16)
    for hd in range(DIFF_HEADS):
        sl = slice(hd * HEAD_PAD, (hd + 1) * HEAD_PAD)
        dq1_ref[0, hd] = dq1[:, sl]
        dq2_ref[0, hd] = dq2[:, sl]
        dk_ref[0, hd] = dk[:, sl]
        dvt_ref[0, hd] = dvt[sl, :]

    ga_ref[0] = _sigmoid(_dot(h, wga_ref[...])).astype(BF16)
    gb_ref[0] = _sigmoid(_dot(h, wgb_ref[...])).astype(BF16)


def _inproj(x, sc1, sh1, ct, st, w):
    b, s, d = x.shape
    ts = min(TS_PROJ, s)
    hh = MLA_HEADS
    row = lambda i, j: (i, j, 0)
    bat = lambda i, j: (i, 0, 0)
    head_rows = pl.BlockSpec((1, hh, ts, HEAD_PAD), lambda i, j: (i, 0, j, 0))
    weights = [w["wlat"], w["wdq"], w["wdk"], w["wdv"], w["wga"], w["wgb"],
               w["qg"], w["kvg"], w["wq"], w["wqs"], w["wkn"], w["wv"], w["esel"]]
    head_shape = jax.ShapeDtypeStruct((b, hh, s, HEAD_PAD), BF16)
    return pl.pallas_call(
        _inproj_kernel,
        grid=(b, s // ts),
        in_specs=[
            pl.BlockSpec((1, ts, d), row),
            pl.BlockSpec((1, 1, d), bat),
            pl.BlockSpec((1, 1, d), bat),
            pl.BlockSpec((1, ts, HEAD_PAD), row),
            pl.BlockSpec((1, ts, HEAD_PAD), row),
        ] + [_const_spec(a.shape) for a in weights],
        out_specs=[
            head_rows, head_rows,
            pl.BlockSpec((1, hh, MLA_V, ts), lambda i, j: (i, 0, 0, j)),
            head_rows, head_rows, head_rows,
            pl.BlockSpec((1, hh, DIFF_VD, ts), lambda i, j: (i, 0, 0, j)),
            pl.BlockSpec((1, ts, d), row),
            pl.BlockSpec((1, ts, d), row),
        ],
        out_shape=[
            head_shape, head_shape,
            jax.ShapeDtypeStruct((b, hh, MLA_V, s), BF16),
            head_shape, head_shape, head_shape,
            jax.ShapeDtypeStruct((b, hh, DIFF_VD, s), BF16),
            jax.ShapeDtypeStruct((b, s, d), BF16),
            jax.ShapeDtypeStruct((b, s, d), BF16),
        ],
        compiler_params=_cparams(2),
        name="inproj",
    )(x, sc1, sh1, ct, st, *weights)


def _mla_kernel(q_ref, k_ref, vt_ref, o_ref, acc_ref):
    def head(hd, carry):
        s_t = _dot_nt(k_ref[0, hd], q_ref[0, hd])
        m = jnp.max(s_t, axis=0, keepdims=True)
        p = jnp.exp2(s_t - m)
        l = jnp.sum(p, axis=0, keepdims=True)
        o_t = _dot(vt_ref[0, hd], p.astype(BF16))
        acc_ref[pl.ds(pl.multiple_of(hd * MLA_V, MLA_V), MLA_V), :] = o_t / l
        return carry

    lax.fori_loop(0, MLA_HEADS, head, 0)
    o_ref[0] = acc_ref[...].T.astype(BF16)


def _mla_attention(qm, km, vt):
    b, hh, s, _ = qm.shape
    tq = min(TQ_ATTN, s)
    return pl.pallas_call(
        _mla_kernel,
        grid=(b, s // tq),
        in_specs=[
            pl.BlockSpec((1, hh, tq, HEAD_PAD), lambda i, j: (i, 0, j, 0)),
            pl.BlockSpec((1, hh, s, HEAD_PAD), lambda i, j: (i, 0, 0, 0)),
            pl.BlockSpec((1, hh, MLA_V, s), lambda i, j: (i, 0, 0, 0)),
        ],
        out_specs=pl.BlockSpec((1, tq, hh * MLA_V), lambda i, j: (i, j, 0)),
        out_shape=jax.ShapeDtypeStruct((b, s, hh * MLA_V), BF16),
        scratch_shapes=[pltpu.VMEM((hh * MLA_V, tq), F32)],
        compiler_params=_cparams(2),
        name="mla_attn",
    )(qm, km, vt)


def _diff_kernel(lam_init, q1_ref, q2_ref, k_ref, vt_ref, pk_ref, pq_ref, slope_ref,
                 lq1_ref, lk1_ref, lq2_ref, lk2_ref, g_ref, o_ref, acc_ref, dist_ref):
    dist_ref[...] = jnp.abs(pk_ref[0] - pq_ref[0])
    lam = (jnp.exp(jnp.sum(lq1_ref[...] * lk1_ref[...], axis=1, keepdims=True))
           - jnp.exp(jnp.sum(lq2_ref[...] * lk2_ref[...], axis=1, keepdims=True))
           + lam_init)

    def head(hd, carry):
        k = k_ref[0, hd]
        bias = slope_ref[hd] * dist_ref[...]
        z1 = _dot_nt(k, q1_ref[0, hd]) - bias
        z2 = _dot_nt(k, q2_ref[0, hd]) - bias
        e1 = jnp.exp2(z1 - jnp.max(z1, axis=0, keepdims=True))
        e2 = jnp.exp2(z2 - jnp.max(z2, axis=0, keepdims=True))
        r1 = 1.0 / jnp.sum(e1, axis=0, keepdims=True)
        r2 = lam / jnp.sum(e2, axis=0, keepdims=True)
        a_t = (e1 * r1 - e2 * r2).astype(BF16)
        o_t = _dot(vt_ref[0, hd], a_t)
        ms = jnp.mean(o_t * o_t, axis=0, keepdims=True)
        o_t = o_t * lax.rsqrt(ms + RMS_EPS) * g_ref[hd] * (1.0 - lam_init)
        acc_ref[pl.ds(pl.multiple_of(hd * DIFF_VD, DIFF_VD), DIFF_VD), :] = o_t
        return carry

    lax.fori_loop(0, DIFF_HEADS, head, 0)
    o_ref[0] = acc_ref[...].T.astype(BF16)


def _diff_attention(dq1, dq2, dk, dvt, pos_k, pos_q, slopes, lq1, lk1, lq2, lk2, gcol, lam_init):
    b, hh, s, _ = dk.shape
    tq = min(TQ_ATTN, s)
    head_q = pl.BlockSpec((1, hh, tq, HEAD_PAD), lambda i, j: (i, 0, j, 0))
    vec = pl.BlockSpec((1, DIFF_HD), lambda i, j: (0, 0))
    return pl.pallas_call(
        functools.partial(_diff_kernel, lam_init),
        grid=(b, s // tq),
        in_specs=[
            head_q, head_q,
            pl.BlockSpec((1, hh, s, HEAD_PAD), lambda i, j: (i, 0, 0, 0)),
            pl.BlockSpec((1, hh, DIFF_VD, s), lambda i, j: (i, 0, 0, 0)),
            pl.BlockSpec((1, s, 1), lambda i, j: (i, 0, 0)),
            pl.BlockSpec((1, 1, tq), lambda i, j: (i, 0, j)),
            pl.BlockSpec(memory_space=pltpu.SMEM),
            vec, vec, vec, vec,
            pl.BlockSpec((hh, DIFF_VD, 1), lambda i, j: (0, 0, 0)),
        ],
        out_specs=pl.BlockSpec((1, tq, hh * DIFF_VD), lambda i, j: (i, j, 0)),
        out_shape=jax.ShapeDtypeStruct((b, s, hh * DIFF_VD), BF16),
        scratch_shapes=[pltpu.VMEM((hh * DIFF_VD, tq), F32), pltpu.VMEM((s, tq), F32)],
        compiler_params=_cparams(2),
        name="diff_attn",
    )(dq1, dq2, dk, dvt, pos_k, pos_q, slopes, lq1, lk1, lq2, lk2, gcol)


def _mix_kernel(alpha, mo_ref, do_ref, ga_ref, gb_ref, x_ref, g1_ref, lng_ref, lnb_ref,
                wbm_ref, wbd_ref, wo_ref, o_ref):
    ya = _dot(mo_ref[0], wbm_ref[...])
    yb = _dot(do_ref[0], wbd_ref[...])
    gated = (ga_ref[0].astype(F32) * ya + gb_ref[0].astype(F32) * yb).astype(BF16)
    mix = _dot(gated, wo_ref[...])
    r = alpha * x_ref[0] + g1_ref[0] * mix
    o_ref[0] = _layernorm(r, lng_ref[...], lnb_ref[...])


def _mix(alpha, mla_o, diff_o, ga, gb, x, g1, lng, lnb, wbm, wbd, wo):
    b, s, d = x.shape
    ts = min(TS_MIX, s)
    row = lambda i, j: (i, j, 0)
    return pl.pallas_call(
        functools.partial(_mix_kernel, alpha),
        grid=(b, s // ts),
        in_specs=[
            pl.BlockSpec((1, ts, mla_o.shape[-1]), row),
            pl.BlockSpec((1, ts, diff_o.shape[-1]), row),
            pl.BlockSpec((1, ts, d), row),
            pl.BlockSpec((1, ts, d), row),
            pl.BlockSpec((1, ts, d), row),
            pl.BlockSpec((1, 1, d), lambda i, j: (i, 0, 0)),
            _const_spec(lng.shape), _const_spec(lnb.shape),
            _const_spec(wbm.shape), _const_spec(wbd.shape), _const_spec(wo.shape),
        ],
        out_specs=pl.BlockSpec((1, ts, d), row),
        out_shape=jax.ShapeDtypeStruct((b, s, d), F32),
        compiler_params=_cparams(2),
        name="mix_ln1",
    )(mla_o, diff_o, ga, gb, x, g1, lng, lnb, wbm, wbd, wo)


def _swiglu(h, w1, w3, w2):
    a = _dot(h, w1)
    bgate = _dot(h, w3)
    u = (a * _sigmoid(a) * bgate).astype(BF16)
    return _dot(u, w2)


def _ffn_kernel(alpha, x_ref, sc_ref, sh_ref, g2_ref, lng_ref, lnb_ref, w1_ref, w3_ref, w2_ref, o_ref):
    x = x_ref[0]
    h = (x * (1.0 + sc_ref[0]) + sh_ref[0]).astype(BF16)
    f = _swiglu(h, w1_ref[...], w3_ref[...], w2_ref[...])
    r = alpha * x + g2_ref[0] * f
    o_ref[0] = _layernorm(r, lng_ref[...], lnb_ref[...])


def _ffn(alpha, x, sc2, sh2, g2, lng, lnb, w1, w3, w2):
    b, s, d = x.shape
    ts = min(TS_FFN, s)
    row = lambda i, j: (i, j, 0)
    bat = lambda i, j: (i, 0, 0)
    return pl.pallas_call(
        functools.partial(_ffn_kernel, alpha),
        grid=(b, s // ts),
        in_specs=[
            pl.BlockSpec((1, ts, d), row),
            pl.BlockSpec((1, 1, d), bat), pl.BlockSpec((1, 1, d), bat), pl.BlockSpec((1, 1, d), bat),
            _const_spec(lng.shape), _const_spec(lnb.shape),
            _const_spec(w1.shape), _const_spec(w3.shape), _const_spec(w2.shape),
        ],
        out_specs=pl.BlockSpec((1, ts, d), row),
        out_shape=jax.ShapeDtypeStruct((b, s, d), F32),
        compiler_params=_cparams(2),
        name="ffn_ln2",
    )(x, sc2, sh2, g2, lng, lnb, w1, w3, w2)


def _router_kernel(x_ref, sc_ref, sh_ref, rw_ref, rb_ref, h_ref, ei_ref, gi_ref):
    h = x_ref[0] * (1.0 + sc_ref[0]) + sh_ref[0]
    h_ref[0] = h
    logits = _dot(h.astype(BF16), rw_ref[...]) + rb_ref[...]
    lane = lax.broadcasted_iota(jnp.int32, logits.shape, 1)
    lane_f = lane.astype(F32)
    neg = jnp.float32(-jnp.inf)
    lg = jnp.where(lane < N_EXPERTS, logits, neg)
    m1 = jnp.max(lg, axis=1, keepdims=True)
    i1 = jnp.min(jnp.where(lg == m1, lane_f, 128.0), axis=1, keepdims=True)
    lg2 = jnp.where(lane_f == i1, neg, lg)
    m2 = jnp.max(lg2, axis=1, keepdims=True)
    i2 = jnp.min(jnp.where(lg2 == m2, lane_f, 128.0), axis=1, keepdims=True)
    t = jnp.exp(m2 - m1)
    den = 1.0 + t
    ei_ref[0] = jnp.where(lane == 0, i1, jnp.where(lane == 1, i2, 0.0)).astype(jnp.int32)
    gi_ref[0] = jnp.where(lane == 0, 1.0 / den, jnp.where(lane == 1, t / den, 0.0))


def _router(x, sc2, sh2, rw, rb):
    b, s, d = x.shape
    ts = min(TS_ROUTE, s)
    row = lambda i, j: (i, j, 0)
    bat = lambda i, j: (i, 0, 0)
    return pl.pallas_call(
        _router_kernel,
        grid=(b, s // ts),
        in_specs=[
            pl.BlockSpec((1, ts, d), row),
            pl.BlockSpec((1, 1, d), bat), pl.BlockSpec((1, 1, d), bat),
            _const_spec(rw.shape), _const_spec(rb.shape),
        ],
        out_specs=[pl.BlockSpec((1, ts, d), row),
                   pl.BlockSpec((1, ts, HEAD_PAD), row),
                   pl.BlockSpec((1, ts, HEAD_PAD), row)],
        out_shape=[jax.ShapeDtypeStruct((b, s, d), F32),
                   jax.ShapeDtypeStruct((b, s, HEAD_PAD), jnp.int32),
                   jax.ShapeDtypeStruct((b, s, HEAD_PAD), F32)],
        compiler_params=_cparams(2),
        name="moe_router",
    )(x, sc2, sh2, rw, rb)


def _gather_rows(src_hbm, idx_smem, dst_vmem, sem, n_rows):
    def issue(r, carry):
        pltpu.make_async_copy(src_hbm.at[pl.ds(idx_smem[0, r], 1)],
                              dst_vmem.at[pl.ds(r, 1)], sem).start()
        return carry

    lax.fori_loop(0, n_rows, issue, 0)
    pltpu.make_async_copy(src_hbm.at[pl.ds(0, n_rows)], dst_vmem, sem).wait()


def _load_indices(idx_hbm, tile, idx_smem, sem):
    cp = pltpu.make_async_copy(idx_hbm.at[tile], idx_smem, sem)
    cp.start()
    cp.wait()


def _experts_kernel(te_ref, nu_ref, idx_hbm, h_hbm, w1_ref, w3_ref, w2_ref, y_ref,
                    idx_smem, xbuf, sem_idx, sem_rows):
    i = pl.program_id(0)

    @pl.when(i < nu_ref[0])
    def _():
        _load_indices(idx_hbm, i, idx_smem, sem_idx)
        _gather_rows(h_hbm, idx_smem, xbuf, sem_rows, xbuf.shape[0])
        y_ref[...] = _swiglu(xbuf[...].astype(BF16), w1_ref[0], w3_ref[0], w2_ref[0])

    @pl.when(i >= nu_ref[0])
    def _():
        y_ref[...] = jnp.zeros_like(y_ref)


def _experts(tile_expert, n_used, src_idx, h2, w1, w3, w2):
    n_tiles, _, tm = src_idx.shape
    t, d = h2.shape
    f = w1.shape[-1]
    grid_spec = pltpu.PrefetchScalarGridSpec(
        num_scalar_prefetch=2,
        grid=(n_tiles,),
        in_specs=[
            pl.BlockSpec(memory_space=pl.ANY),
            pl.BlockSpec(memory_space=pl.ANY),
            pl.BlockSpec((1, d, f), lambda i, te, nu: (te[i], 0, 0)),
            pl.BlockSpec((1, d, f), lambda i, te, nu: (te[i], 0, 0)),
            pl.BlockSpec((1, f, d), lambda i, te, nu: (te[i], 0, 0)),
        ],
        out_specs=pl.BlockSpec((tm, d), lambda i, te, nu: (i, 0)),
        scratch_shapes=[
            pltpu.SMEM((1, tm), jnp.int32),
            pltpu.VMEM((tm, d), F32),
            pltpu.SemaphoreType.DMA(()),
            pltpu.SemaphoreType.DMA(()),
        ],
    )
    return pl.pallas_call(
        _experts_kernel,
        grid_spec=grid_spec,
        out_shape=jax.ShapeDtypeStruct((n_tiles * tm, d), F32),
        compiler_params=_cparams(1),
        name="moe_experts",
    )(tile_expert, n_used, src_idx, h2, w1, w3, w2)


def _combine_kernel(alpha, p0_hbm, p1_hbm, y_hbm, gi_ref, x_ref, g2_ref, lng_ref, lnb_ref, o_ref,
                    i0_smem, i1_smem, y0, y1, sem_idx, sem0, sem1):
    tile = pl.program_id(0) * pl.num_programs(1) + pl.program_id(1)
    _load_indices(p0_hbm, tile, i0_smem, sem_idx)
    _load_indices(p1_hbm, tile, i1_smem, sem_idx)
    n = y0.shape[0]

    def issue(r, carry):
        pltpu.make_async_copy(y_hbm.at[pl.ds(i0_smem[0, r], 1)], y0.at[pl.ds(r, 1)], sem0).start()
        pltpu.make_async_copy(y_hbm.at[pl.ds(i1_smem[0, r], 1)], y1.at[pl.ds(r, 1)], sem1).start()
        return carry

    lax.fori_loop(0, n, issue, 0)
    pltpu.make_async_copy(y_hbm.at[pl.ds(0, n)], y0, sem0).wait()
    pltpu.make_async_copy(y_hbm.at[pl.ds(0, n)], y1, sem1).wait()
    gi = gi_ref[0]
    f = gi[:, 0:1] * y0[...] + gi[:, 1:2] * y1[...]
    r = alpha * x_ref[0] + g2_ref[0] * f
    o_ref[0] = _layernorm(r, lng_ref[...], lnb_ref[...])


def _combine(alpha, pos0, pos1, y, gi, x, g2, lng, lnb):
    b, s, d = x.shape
    ts = pos0.shape[-1]
    row = lambda i, j: (i, j, 0)
    return pl.pallas_call(
        functools.partial(_combine_kernel, alpha),
        grid=(b, s // ts),
        in_specs=[
            pl.BlockSpec(memory_space=pl.ANY),
            pl.BlockSpec(memory_space=pl.ANY),
            pl.BlockSpec(memory_space=pl.ANY),
            pl.BlockSpec((1, ts, HEAD_PAD), row),
            pl.BlockSpec((1, ts, d), row),
            pl.BlockSpec((1, 1, d), lambda i, j: (i, 0, 0)),
            _const_spec(lng.shape), _const_spec(lnb.shape),
        ],
        out_specs=pl.BlockSpec((1, ts, d), row),
        out_shape=jax.ShapeDtypeStruct((b, s, d), F32),
        scratch_shapes=[
            pltpu.SMEM((1, ts), jnp.int32), pltpu.SMEM((1, ts), jnp.int32),
            pltpu.VMEM((ts, d), F32), pltpu.VMEM((ts, d), F32),
            pltpu.SemaphoreType.DMA(()), pltpu.SemaphoreType.DMA(()), pltpu.SemaphoreType.DMA(()),
        ],
        compiler_params=_cparams(2),
        name="moe_combine_ln2",
    )(pos0, pos1, y, gi, x, g2, lng, lnb)


def _moe_plan(expert_idx, tm, ts):
    t = expert_idx.shape[0]
    n_slots = t * TOP_K
    flat = expert_idx.reshape(n_slots)
    onehot = (flat[:, None] == jnp.arange(N_EXPERTS, dtype=jnp.int32)[None, :]).astype(jnp.int32)
    csum = jnp.cumsum(onehot, axis=0)
    rank = jnp.sum((csum - 1) * onehot, axis=1)
    counts = csum[-1]
    padded = (counts + tm - 1) // tm * tm
    pends = jnp.cumsum(padded)
    pstarts = pends - padded
    starts = jnp.cumsum(counts) - counts
    dest = (pstarts[flat] + rank).astype(jnp.int32)
    n_tiles = (n_slots + N_EXPERTS * tm) // tm
    tile_expert = jnp.minimum(
        jnp.searchsorted(pends, jnp.arange(n_tiles, dtype=jnp.int32) * tm, side="right"),
        N_EXPERTS - 1).astype(jnp.int32)
    n_used = (pends[-1] // tm).astype(jnp.int32).reshape(1)
    order = jnp.argsort(flat, stable=True).astype(jnp.int32)
    rows = jnp.arange(n_tiles * tm, dtype=jnp.int32)
    row_e = tile_expert[rows // tm]
    r_in = rows - pstarts[row_e].astype(jnp.int32)
    valid = r_in < counts[row_e]
    src_slot = order[jnp.clip(starts[row_e].astype(jnp.int32) + r_in, 0, n_slots - 1)]
    src_tok = jnp.where(valid, src_slot // TOP_K, 0).astype(jnp.int32)
    dest2 = dest.reshape(t, TOP_K)
    pos0 = dest2[:, 0].reshape(t // ts, 1, ts)
    pos1 = dest2[:, 1].reshape(t // ts, 1, ts)
    return tile_expert, n_used, src_tok.reshape(n_tiles, 1, tm), pos0, pos1


def _moe(alpha, x, sc2, sh2, g2, lng, lnb, rw, rb, w1, w3, w2):
    b, s, d = x.shape
    t = b * s
    h2, ei, gi = _router(x, sc2, sh2, rw, rb)
    tm = min(TM_MOE, t)
    ts = min(TS_COMB, s)
    tile_expert, n_used, src_idx, pos0, pos1 = _moe_plan(ei.reshape(t, HEAD_PAD)[:, :TOP_K], tm, ts)
    y = _experts(tile_expert, n_used, src_idx, h2.reshape(t, d), w1, w3, w2)
    return _combine(alpha, pos0, pos1, y, gi, x, g2, lng, lnb)


def _prep_layer_weights(w_in, w_q_up, w_kv_up, q_norm_g, kv_norm_g):
    n_l, d, _ = w_in.shape
    z96 = jnp.zeros((n_l, d, HEAD_PAD - MLA_ROPE), w_in.dtype)
    o_kr = MLA_Q_LORA + MLA_KV_LORA
    half = MLA_ROPE // 2
    kr_a = w_in[:, :, o_kr:o_kr + half]
    kr_b = w_in[:, :, o_kr + half:o_kr + MLA_ROPE]
    wlat = jnp.concatenate([w_in[:, :, :o_kr + MLA_ROPE], z96, kr_b, kr_a, z96], axis=-1)
    o = o_kr + MLA_ROPE
    nd = DIFF_HEADS * 2 * DIFF_HD
    wdq = w_in[:, :, o:o + nd]
    wdk = w_in[:, :, o + nd:o + 2 * nd]
    wdv = w_in[:, :, o + 2 * nd:o + 3 * nd]
    wga = w_in[:, :, o + 3 * nd:o + 3 * nd + d]
    wgb = w_in[:, :, o + 3 * nd + d:o + 3 * nd + 2 * d]

    hq = MLA_NOPE + MLA_ROPE
    wq4 = w_q_up.reshape(n_l, MLA_Q_LORA, MLA_HEADS, hq)
    zq = jnp.zeros((n_l, MLA_Q_LORA, MLA_HEADS, HEAD_PAD - hq), w_q_up.dtype)
    wq = jnp.concatenate([wq4, zq], axis=-1).reshape(n_l, MLA_Q_LORA, MLA_HEADS * HEAD_PAD)
    wqs = jnp.concatenate([jnp.zeros_like(wq4[..., :MLA_NOPE]), wq4[..., MLA_NOPE + half:],
                           wq4[..., MLA_NOPE:MLA_NOPE + half], zq], axis=-1)
    wqs = wqs.reshape(n_l, MLA_Q_LORA, MLA_HEADS * HEAD_PAD)

    wkv4 = w_kv_up.reshape(n_l, MLA_KV_LORA, MLA_HEADS, MLA_NOPE + MLA_V)
    wkn = jnp.concatenate([wkv4[..., :MLA_NOPE], jnp.zeros_like(wkv4[..., :HEAD_PAD - MLA_NOPE])], axis=-1)
    wkn = wkn.reshape(n_l, MLA_KV_LORA, MLA_HEADS * HEAD_PAD)
    wv = wkv4[..., MLA_NOPE:].reshape(n_l, MLA_KV_LORA, MLA_HEADS * MLA_V)

    rr = jnp.arange(HEAD_PAD)[:, None]
    cc = jnp.arange(MLA_HEADS * HEAD_PAD)[None, :]
    esel = ((rr < MLA_ROPE) & (cc % HEAD_PAD == MLA_NOPE + rr)).astype(BF16)

    cast = lambda a: a.astype(BF16)
    return dict(wlat=cast(wlat), wdq=cast(wdq), wdk=cast(wdk), wdv=cast(wdv), wga=cast(wga), wgb=cast(wgb),
                wq=cast(wq), wqs=cast(wqs), wkn=cast(wkn), wv=cast(wv), esel=esel,
                qg=q_norm_g.reshape(n_l, 1, MLA_Q_LORA), kvg=kv_norm_g.reshape(n_l, 1, MLA_KV_LORA))


def _rope_tables(positions):
    inv_freq = ROPE_BASE ** (-jnp.arange(0, MLA_ROPE, 2, dtype=F32) / MLA_ROPE)
    ang = positions.astype(F32)[..., None] * inv_freq
    cos, sin = jnp.cos(ang), jnp.sin(ang)
    ones = jnp.ones(positions.shape + (MLA_NOPE,), F32)
    tail = HEAD_PAD - MLA_NOPE - MLA_ROPE
    ct = jnp.concatenate([ones, cos, cos, ones[..., :tail]], axis=-1)
    st = jnp.concatenate([0.0 * ones, -sin, sin, 0.0 * ones[..., :tail]], axis=-1)
    return ct, st


def kernel(x, c, positions, w_ada, b_ada, w_in, q_norm_g, w_q_up, kv_norm_g, w_kv_up, lambda_q1, lambda_k1, lambda_q2, lambda_k2, diff_norm_g, w_br_mla, w_br_diff, w_out, ln1_g, ln1_b, ln2_g, ln2_b, ffn_w1, ffn_w3, ffn_w2, router_w, router_b, moe_w1, moe_w3, moe_w2):
    b, s, d = x.shape
    depth = w_in.shape[0]
    alpha = (2.0 * depth) ** 0.25

    mod = _ada_mod(c, w_ada, b_ada)
    ct, st = _rope_tables(positions)
    posf = positions.astype(F32)
    pos_k = posf.reshape(b, s, 1)
    pos_q = posf.reshape(b, 1, s)
    slopes = (2.0 ** (-8.0 * jnp.arange(1, DIFF_HEADS + 1, dtype=F32) / DIFF_HEADS)) * LOG2E

    lw = _prep_layer_weights(w_in, w_q_up, w_kv_up, q_norm_g, kv_norm_g)
    wbm, wbd, wo = w_br_mla.astype(BF16), w_br_diff.astype(BF16), w_out.astype(BF16)
    fw1, fw3, fw2 = ffn_w1.astype(BF16), ffn_w3.astype(BF16), ffn_w2.astype(BF16)
    mw1, mw3, mw2 = moe_w1.astype(BF16), moe_w3.astype(BF16), moe_w2.astype(BF16)
    rw = jnp.pad(router_w, ((0, 0), (0, 0), (0, HEAD_PAD - N_EXPERTS))).astype(BF16)
    rb = jnp.pad(router_b, ((0, 0), (0, HEAD_PAD - N_EXPERTS))).reshape(-1, 1, HEAD_PAD)
    gcol = diff_norm_g.reshape(depth, DIFF_HEADS, DIFF_VD, 1)
    vec = lambda a, l: a[l].reshape(1, -1)

    for l in range(depth):
        sh1, sc1, g1, sh2, sc2, g2 = [m.reshape(b, 1, d) for m in jnp.split(mod[l], 6, axis=-1)]
        w_l = {k: v[l] if k != "esel" else v for k, v in lw.items()}
        qm, km, vt, dq1, dq2, dk, dvt, ga, gb = _inproj(x, sc1, sh1, ct, st, w_l)
        mla_o = _mla_attention(qm, km, vt)
        lam_init = 0.8 - 0.6 * math.exp(-0.3 * l)
        diff_o = _diff_attention(dq1, dq2, dk, dvt, pos_k, pos_q, slopes,
                                 vec(lambda_q1, l), vec(lambda_k1, l), vec(lambda_q2, l), vec(lambda_k2, l),
                                 gcol[l], lam_init)
        x = _mix(alpha, mla_o, diff_o, ga, gb, x, g1, vec(ln1_g, l), vec(ln1_b, l), wbm[l], wbd[l], wo[l])
        if l % 2 == 0:
            x = _ffn(alpha, x, sc2, sh2, g2, vec(ln2_g, l), vec(ln2_b, l), fw1[l // 2], fw3[l // 2], fw2[l // 2])
        else:
            x = _moe(alpha, x, sc2, sh2, g2, vec(ln2_g, l), vec(ln2_b, l),
                     rw[l // 2], rb[l // 2], mw1[l // 2], mw3[l // 2], mw2[l // 2])
    return x
```

```python
import functools
import math

import jax
import jax.numpy as jnp
from jax import lax
from jax.experimental import pallas as pl
from jax.experimental.pallas import tpu as pltpu

BF16 = jnp.bfloat16
F32 = jnp.float32

D_MODEL = 1024
MLA_HEADS = 8
MLA_NOPE = 64
MLA_ROPE = 32
MLA_V = 64
MLA_Q_LORA = 384
MLA_KV_LORA = 256
ROPE_BASE = 10000.0
DIFF_HEADS = 8
DIFF_HD = 64
DIFF_VD = 2 * DIFF_HD
D_FF = 2816
N_EXPERTS = 8
TOP_K = 2
LN_EPS = 1e-5
RMS_EPS = 1e-6

HEAD_PAD = 128
ONES_ROWS = 16
LOG2E = 1.4426950408889634
MLA_QSCALE = (MLA_NOPE + MLA_ROPE) ** -0.5 * LOG2E
DIFF_QSCALE = DIFF_HD ** -0.5 * LOG2E

VMEM_LIMIT = 56 * 1024 * 1024

TS_PROJ = 256
TQ_ATTN = 256
TS_MIX = 512
TS_FFN = 256
TM_MOE = 256
TS_ROUTE = 512
TS_COMB = 256


def _cparams(n_axes):
    return pltpu.CompilerParams(dimension_semantics=("arbitrary",) * n_axes,
                                vmem_limit_bytes=VMEM_LIMIT)


def _const_spec(shape):
    nd = len(shape)
    return pl.BlockSpec(shape, lambda *_: (0,) * nd, pipeline_mode=pl.Buffered(1))


def _dot(a, b):
    return jnp.dot(a, b, preferred_element_type=F32)


def _dot_nt(a, b):
    return lax.dot_general(a, b, (((1,), (1,)), ((), ())), preferred_element_type=F32)


def _sigmoid(v):
    return 1.0 / (1.0 + jnp.exp(-v))


def _layernorm(r, g, b):
    mu = jnp.mean(r, axis=-1, keepdims=True)
    d = r - mu
    var = jnp.mean(d * d, axis=-1, keepdims=True)
    return d * lax.rsqrt(var + LN_EPS) * g + b


def _rms_rows(v, g):
    ms = jnp.mean(v * v, axis=-1, keepdims=True)
    return v * lax.rsqrt(ms + RMS_EPS) * g


def _ada_kernel(c_ref, w_ref, b_ref, o_ref):
    c = c_ref[...]
    cond = c * _sigmoid(c)
    o_ref[0] = _dot(cond.astype(BF16), w_ref[0].astype(BF16)) + b_ref[0]


def _ada_mod(c, w_ada, b_ada):
    n_l, d, n6 = w_ada.shape
    b = c.shape[0]
    tn = 1536
    return pl.pallas_call(
        _ada_kernel,
        grid=(n_l, n6 // tn),
        in_specs=[
            pl.BlockSpec((b, d), lambda l, j: (0, 0)),
            pl.BlockSpec((1, d, tn), lambda l, j: (l, 0, j)),
            pl.BlockSpec((1, 1, tn), lambda l, j: (l, 0, j)),
        ],
        out_specs=pl.BlockSpec((1, b, tn), lambda l, j: (l, 0, j)),
        out_shape=jax.ShapeDtypeStruct((n_l, b, n6), F32),
        compiler_params=_cparams(2),
        name="ada_mod",
    )(c, w_ada, b_ada.reshape(n_l, 1, n6))


def _inproj_kernel(x_ref, sc_ref, sh_ref, ct_ref, st_ref,
                   wlat_ref, wdq_ref, wdk_ref, wdv_ref, wga_ref, wgb_ref,
                   qg_ref, kvg_ref, wq_ref, wqs_ref, wkn_ref, wv_ref, esel_ref,
                   qm_ref, km_ref, vt_ref, dq1_ref, dq2_ref, dk_ref, dvt_ref, ga_ref, gb_ref):
    x = x_ref[0]
    h = (x * (1.0 + sc_ref[0]) + sh_ref[0]).astype(BF16)
    ct = ct_ref[0]
    st = st_ref[0]

    lat = _dot(h, wlat_ref[...])
    q_lat = lat[:, :MLA_Q_LORA]
    kv_lat = lat[:, MLA_Q_LORA:MLA_Q_LORA + MLA_KV_LORA]
    kr = lat[:, 640:768]
    krs = lat[:, 768:896]

    qn = _rms_rows(q_lat, qg_ref[...]).astype(BF16)
    q = _dot(qn, wq_ref[...])
    qs = _dot(qn, wqs_ref[...])
    c8 = jnp.concatenate([ct] * MLA_HEADS, axis=1)
    s8 = jnp.concatenate([st] * MLA_HEADS, axis=1)
    qr = ((q * c8 + qs * s8) * MLA_QSCALE).astype(BF16)
    for hd in range(MLA_HEADS):
        qm_ref[0, hd] = qr[:, hd * HEAD_PAD:(hd + 1) * HEAD_PAD]

    kvn = _rms_rows(kv_lat, kvg_ref[...]).astype(BF16)
    kn = _dot(kvn, wkn_ref[...])
    ck = pltpu.roll(ct, 64, axis=1)
    sk = pltpu.roll(st, 64, axis=1)
    kro = (kr * ck + krs * sk).astype(BF16)
    kcat = (kn + _dot(kro, esel_ref[...])).astype(BF16)
    for hd in range(MLA_HEADS):
        km_ref[0, hd] = kcat[:, hd * HEAD_PAD:(hd + 1) * HEAD_PAD]
    v = _dot(kvn, wv_ref[...])
    vt = v.T.astype(BF16)
    ones = jnp.ones((ONES_ROWS, vt.shape[1]), BF16)
    for hd in range(MLA_HEADS):
        vt_ref[0, hd, :MLA_V, :] = vt[hd * MLA_V:(hd + 1) * MLA_V, :]
        vt_ref[0, hd, MLA_V:, :] = ones

    dq = _dot(h, wdq_ref[...]) * DIFF_QSCALE
    lane = lax.broadcasted_iota(jnp.int32, dq.shape, 1)
    first = (lane % HEAD_PAD) < DIFF_HD
    dq1 = jnp.where(first, dq, 0.0).astype(BF16)
    dq2 = jnp.where(first, 0.0, dq).astype(BF16)
    dk = _dot(h, wdk_ref[...]).astype(BF16)
    dvt = _dot(h, wdv_ref[...]).T.astype(BF16)
    for hd in range(DIFF_HEADS):
        sl = slice(hd * HEAD_PAD, (hd + 1) * HEAD_PAD)
        dq1_ref[0, hd] = dq1[:, sl]
        dq2_ref[0, hd] = dq2[:, sl]
        dk_ref[0, hd] = dk[:, sl]
        dvt_ref[0, hd, :DIFF_VD, :] = dvt[sl, :]
        dvt_ref[0, hd, DIFF_VD:, :] = ones

    ga_ref[0] = _sigmoid(_dot(h, wga_ref[...])).astype(BF16)
    gb_ref[0] = _sigmoid(_dot(h, wgb_ref[...])).astype(BF16)


def _inproj(x, sc1, sh1, ct, st, w):
    b, s, d = x.shape
    ts = min(TS_PROJ, s)
    hh = MLA_HEADS
    row = lambda i, j: (i, j, 0)
    bat = lambda i, j: (i, 0, 0)
    head_rows = pl.BlockSpec((1, hh, ts, HEAD_PAD), lambda i, j: (i, 0, j, 0))
    weights = [w["wlat"], w["wdq"], w["wdk"], w["wdv"], w["wga"], w["wgb"],
               w["qg"], w["kvg"], w["wq"], w["wqs"], w["wkn"], w["wv"], w["esel"]]
    head_shape = jax.ShapeDtypeStruct((b, hh, s, HEAD_PAD), BF16)
    return pl.pallas_call(
        _inproj_kernel,
        grid=(b, s // ts),
        in_specs=[
            pl.BlockSpec((1, ts, d), row),
            pl.BlockSpec((1, 1, d), bat),
            pl.BlockSpec((1, 1, d), bat),
            pl.BlockSpec((1, ts, HEAD_PAD), row),
            pl.BlockSpec((1, ts, HEAD_PAD), row),
        ] + [_const_spec(a.shape) for a in weights],
        out_specs=[
            head_rows, head_rows,
            pl.BlockSpec((1, hh, MLA_V + ONES_ROWS, ts), lambda i, j: (i, 0, 0, j)),
            head_rows, head_rows, head_rows,
            pl.BlockSpec((1, hh, DIFF_VD + ONES_ROWS, ts), lambda i, j: (i, 0, 0, j)),
            pl.BlockSpec((1, ts, d), row),
            pl.BlockSpec((1, ts, d), row),
        ],
        out_shape=[
            head_shape, head_shape,
            jax.ShapeDtypeStruct((b, hh, MLA_V + ONES_ROWS, s), BF16),
            head_shape, head_shape, head_shape,
            jax.ShapeDtypeStruct((b, hh, DIFF_VD + ONES_ROWS, s), BF16),
            jax.ShapeDtypeStruct((b, s, d), BF16),
            jax.ShapeDtypeStruct((b, s, d), BF16),
        ],
        compiler_params=_cparams(2),
        name="inproj",
    )(x, sc1, sh1, ct, st, *weights)


def _mla_kernel(q_ref, k_ref, vt_ref, o_ref, acc_ref, za_ref, zb_ref):
    def scores(hd, z_ref):
        s_t = _dot_nt(k_ref[0, hd], q_ref[0, hd])
        z_ref[...] = s_t
        return jnp.max(s_t, axis=0, keepdims=True)

    def values(hd, z_ref, m):
        p = jnp.exp2(z_ref[...] - m).astype(BF16)
        o_t = _dot(vt_ref[0, hd], p)
        acc_ref[pl.ds(pl.multiple_of(hd * MLA_V, MLA_V), MLA_V), :] = o_t[:MLA_V] / o_t[MLA_V:MLA_V + 1]

    def body(i, ma):
        h0 = 2 * i
        mb = scores(h0 + 1, zb_ref)
        values(h0, za_ref, ma)
        ma = scores(h0 + 2, za_ref)
        values(h0 + 1, zb_ref, mb)
        return ma

    last = MLA_HEADS - 1
    ma = lax.fori_loop(0, MLA_HEADS // 2 - 1, body, scores(0, za_ref))
    mb = scores(last, zb_ref)
    values(last - 1, za_ref, ma)
    values(last, zb_ref, mb)
    o_ref[0] = acc_ref[...].T.astype(BF16)


def _mla_attention(qm, km, vt):
    b, hh, s, _ = qm.shape
    tq = min(TQ_ATTN, s)
    return pl.pallas_call(
        _mla_kernel,
        grid=(b, s // tq),
        in_specs=[
            pl.BlockSpec((1, hh, tq, HEAD_PAD), lambda i, j: (i, 0, j, 0)),
            pl.BlockSpec((1, hh, s, HEAD_PAD), lambda i, j: (i, 0, 0, 0)),
            pl.BlockSpec((1, hh, MLA_V + ONES_ROWS, s), lambda i, j: (i, 0, 0, 0)),
        ],
        out_specs=pl.BlockSpec((1, tq, hh * MLA_V), lambda i, j: (i, j, 0)),
        out_shape=jax.ShapeDtypeStruct((b, s, hh * MLA_V), BF16),
        scratch_shapes=[pltpu.VMEM((hh * MLA_V, tq), F32), pltpu.VMEM((s, tq), F32), pltpu.VMEM((s, tq), F32)],
        compiler_params=_cparams(2),
        name="mla_attn",
    )(qm, km, vt)


def _diff_kernel(lam_init, q1_ref, q2_ref, k_ref, vt_ref, pk_ref, pq_ref, slope_ref,
                 lq1_ref, lk1_ref, lq2_ref, lk2_ref, g_ref, o_ref, acc_ref, dist_ref, za_ref, zb_ref):
    dist_ref[...] = jnp.abs(pk_ref[0] - pq_ref[0])
    lam = (jnp.exp(jnp.sum(lq1_ref[...] * lk1_ref[...], axis=1, keepdims=True))
           - jnp.exp(jnp.sum(lq2_ref[...] * lk2_ref[...], axis=1, keepdims=True))
           + lam_init)

    def scores(hd, z_ref):
        k = k_ref[0, hd]
        bias = slope_ref[hd] * dist_ref[...]
        z1 = _dot_nt(k, q1_ref[0, hd]) - bias
        z2 = _dot_nt(k, q2_ref[0, hd]) - bias
        z_ref[0] = z1
        z_ref[1] = z2
        return jnp.max(z1, axis=0, keepdims=True), jnp.max(z2, axis=0, keepdims=True)

    def values(hd, z_ref, m1, m2):
        e1 = jnp.exp2(z_ref[0] - m1).astype(BF16)
        e2 = jnp.exp2(z_ref[1] - m2).astype(BF16)
        o1 = _dot(vt_ref[0, hd], e1)
        o2 = _dot(vt_ref[0, hd], e2)
        r1 = 1.0 / o1[DIFF_VD:DIFF_VD + 1]
        r2 = lam / o2[DIFF_VD:DIFF_VD + 1]
        o_t = o1[:DIFF_VD] * r1 - o2[:DIFF_VD] * r2
        ms = jnp.mean(o_t * o_t, axis=0, keepdims=True)
        o_t = o_t * lax.rsqrt(ms + RMS_EPS) * g_ref[hd] * (1.0 - lam_init)
        acc_ref[pl.ds(pl.multiple_of(hd * DIFF_VD, DIFF_VD), DIFF_VD), :] = o_t

    def body(i, ma):
        h0 = 2 * i
        mb = scores(h0 + 1, zb_ref)
        values(h0, za_ref, *ma)
        ma = scores(h0 + 2, za_ref)
        values(h0 + 1, zb_ref, *mb)
        return ma

    last = DIFF_HEADS - 1
    ma = lax.fori_loop(0, DIFF_HEADS // 2 - 1, body, scores(0, za_ref))
    mb = scores(last, zb_ref)
    values(last - 1, za_ref, *ma)
    values(last, zb_ref, *mb)
    o_ref[0] = acc_ref[...].T.astype(BF16)


def _diff_attention(dq1, dq2, dk, dvt, pos_k, pos_q, slopes, lq1, lk1, lq2, lk2, gcol, lam_init):
    b, hh, s, _ = dk.shape
    tq = min(TQ_ATTN, s)
    head_q = pl.BlockSpec((1, hh, tq, HEAD_PAD), lambda i, j: (i, 0, j, 0))
    vec = pl.BlockSpec((1, DIFF_HD), lambda i, j: (0, 0))
    return pl.pallas_call(
        functools.partial(_diff_kernel, lam_init),
        grid=(b, s // tq),
        in_specs=[
            head_q, head_q,
            pl.BlockSpec((1, hh, s, HEAD_PAD), lambda i, j: (i, 0, 0, 0)),
            pl.BlockSpec((1, hh, DIFF_VD + ONES_ROWS, s), lambda i, j: (i, 0, 0, 0)),
            pl.BlockSpec((1, s, 1), lambda i, j: (i, 0, 0)),
            pl.BlockSpec((1, 1, tq), lambda i, j: (i, 0, j)),
            pl.BlockSpec(memory_space=pltpu.SMEM),
            vec, vec, vec, vec,
            pl.BlockSpec((hh, DIFF_VD, 1), lambda i, j: (0, 0, 0)),
        ],
        out_specs=pl.BlockSpec((1, tq, hh * DIFF_VD), lambda i, j: (i, j, 0)),
        out_shape=jax.ShapeDtypeStruct((b, s, hh * DIFF_VD), BF16),
        scratch_shapes=[pltpu.VMEM((hh * DIFF_VD, tq), F32), pltpu.VMEM((s, tq), F32),
                        pltpu.VMEM((2, s, tq), F32), pltpu.VMEM((2, s, tq), F32)],
        compiler_params=_cparams(2),
        name="diff_attn",
    )(dq1, dq2, dk, dvt, pos_k, pos_q, slopes, lq1, lk1, lq2, lk2, gcol)


def _mix_kernel(alpha, mo_ref, do_ref, ga_ref, gb_ref, x_ref, g1_ref, lng_ref, lnb_ref,
                wbm_ref, wbd_ref, wo_ref, o_ref):
    ya = _dot(mo_ref[0], wbm_ref[...])
    yb = _dot(do_ref[0], wbd_ref[...])
    gated = (ga_ref[0].astype(F32) * ya + gb_ref[0].astype(F32) * yb).astype(BF16)
    mix = _dot(gated, wo_ref[...])
    r = alpha * x_ref[0] + g1_ref[0] * mix
    o_ref[0] = _layernorm(r, lng_ref[...], lnb_ref[...])


def _mix(alpha, mla_o, diff_o, ga, gb, x, g1, lng, lnb, wbm, wbd, wo):
    b, s, d = x.shape
    ts = min(TS_MIX, s)
    row = lambda i, j: (i, j, 0)
    return pl.pallas_call(
        functools.partial(_mix_kernel, alpha),
        grid=(b, s // ts),
        in_specs=[
            pl.BlockSpec((1, ts, mla_o.shape[-1]), row),
            pl.BlockSpec((1, ts, diff_o.shape[-1]), row),
            pl.BlockSpec((1, ts, d), row),
            pl.BlockSpec((1, ts, d), row),
            pl.BlockSpec((1, ts, d), row),
            pl.BlockSpec((1, 1, d), lambda i, j: (i, 0, 0)),
            _const_spec(lng.shape), _const_spec(lnb.shape),
            _const_spec(wbm.shape), _const_spec(wbd.shape), _const_spec(wo.shape),
        ],
        out_specs=pl.BlockSpec((1, ts, d), row),
        out_shape=jax.ShapeDtypeStruct((b, s, d), F32),
        compiler_params=_cparams(2),
        name="mix_ln1",
    )(mla_o, diff_o, ga, gb, x, g1, lng, lnb, wbm, wbd, wo)


def _swiglu(h, w1, w3, w2):
    a = _dot(h, w1)
    bgate = _dot(h, w3)
    u = (a * _sigmoid(a) * bgate).astype(BF16)
    return _dot(u, w2)


def _ffn_kernel(alpha, x_ref, sc_ref, sh_ref, g2_ref, lng_ref, lnb_ref, w1_ref, w3_ref, w2_ref, o_ref):
    x = x_ref[0]
    h = (x * (1.0 + sc_ref[0]) + sh_ref[0]).astype(BF16)
    f = _swiglu(h, w1_ref[...], w3_ref[...], w2_ref[...])
    r = alpha * x + g2_ref[0] * f
    o_ref[0] = _layernorm(r, lng_ref[...], lnb_ref[...])


def _ffn(alpha, x, sc2, sh2, g2, lng, lnb, w1, w3, w2):
    b, s, d = x.shape
    ts = min(TS_FFN, s)
    row = lambda i, j: (i, j, 0)
    bat = lambda i, j: (i, 0, 0)
    return pl.pallas_call(
        functools.partial(_ffn_kernel, alpha),
        grid=(b, s // ts),
        in_specs=[
            pl.BlockSpec((1, ts, d), row),
            pl.BlockSpec((1, 1, d), bat), pl.BlockSpec((1, 1, d), bat), pl.BlockSpec((1, 1, d), bat),
            _const_spec(lng.shape), _const_spec(lnb.shape),
            _const_spec(w1.shape), _const_spec(w3.shape), _const_spec(w2.shape),
        ],
        out_specs=pl.BlockSpec((1, ts, d), row),
        out_shape=jax.ShapeDtypeStruct((b, s, d), F32),
        compiler_params=_cparams(2),
        name="ffn_ln2",
    )(x, sc2, sh2, g2, lng, lnb, w1, w3, w2)


def _router_kernel(x_ref, sc_ref, sh_ref, rw_ref, rb_ref, h_ref, ei_ref, gi_ref):
    h = x_ref[0] * (1.0 + sc_ref[0]) + sh_ref[0]
    h_ref[0] = h
    logits = _dot(h.astype(BF16), rw_ref[...]) + rb_ref[...]
    lane = lax.broadcasted_iota(jnp.int32, logits.shape, 1)
    lane_f = lane.astype(F32)
    neg = jnp.float32(-jnp.inf)
    lg = jnp.where(lane < N_EXPERTS, logits, neg)
    m1 = jnp.max(lg, axis=1, keepdims=True)
    i1 = jnp.min(jnp.where(lg == m1, lane_f, 128.0), axis=1, keepdims=True)
    lg2 = jnp.where(lane_f == i1, neg, lg)
    m2 = jnp.max(lg2, axis=1, keepdims=True)
    i2 = jnp.min(jnp.where(lg2 == m2, lane_f, 128.0), axis=1, keepdims=True)
    t = jnp.exp(m2 - m1)
    den = 1.0 + t
    ei_ref[0] = jnp.where(lane == 0, i1, jnp.where(lane == 1, i2, 0.0)).astype(jnp.int32)
    gi_ref[0] = jnp.where(lane == 0, 1.0 / den, jnp.where(lane == 1, t / den, 0.0))


def _router(x, sc2, sh2, rw, rb):
    b, s, d = x.shape
    ts = min(TS_ROUTE, s)
    row = lambda i, j: (i, j, 0)
    bat = lambda i, j: (i, 0, 0)
    return pl.pallas_call(
        _router_kernel,
        grid=(b, s // ts),
        in_specs=[
            pl.BlockSpec((1, ts, d), row),
            pl.BlockSpec((1, 1, d), bat), pl.BlockSpec((1, 1, d), bat),
            _const_spec(rw.shape), _const_spec(rb.shape),
        ],
        out_specs=[pl.BlockSpec((1, ts, d), row),
                   pl.BlockSpec((1, ts, HEAD_PAD), row),
                   pl.BlockSpec((1, ts, HEAD_PAD), row)],
        out_shape=[jax.ShapeDtypeStruct((b, s, d), F32),
                   jax.ShapeDtypeStruct((b, s, HEAD_PAD), jnp.int32),
                   jax.ShapeDtypeStruct((b, s, HEAD_PAD), F32)],
        compiler_params=_cparams(2),
        name="moe_router",
    )(x, sc2, sh2, rw, rb)


def _gather_rows(src_hbm, idx_smem, dst_vmem, sem, n_rows):
    def issue(r, carry):
        pltpu.make_async_copy(src_hbm.at[pl.ds(idx_smem[0, r], 1)],
                              dst_vmem.at[pl.ds(r, 1)], sem).start()
        return carry

    lax.fori_loop(0, n_rows, issue, 0)
    pltpu.make_async_copy(src_hbm.at[pl.ds(0, n_rows)], dst_vmem, sem).wait()


def _load_indices(idx_hbm, tile, idx_smem, sem):
    cp = pltpu.make_async_copy(idx_hbm.at[tile], idx_smem, sem)
    cp.start()
    cp.wait()


def _experts_kernel(te_ref, nu_ref, idx_hbm, h_hbm, w1_ref, w3_ref, w2_ref, y_ref,
                    idx_smem, xbuf, sem_idx, sem_rows):
    i = pl.program_id(0)

    @pl.when(i < nu_ref[0])
    def _():
        _load_indices(idx_hbm, i, idx_smem, sem_idx)
        _gather_rows(h_hbm, idx_smem, xbuf, sem_rows, xbuf.shape[0])
        y_ref[...] = _swiglu(xbuf[...].astype(BF16), w1_ref[0], w3_ref[0], w2_ref[0])

    @pl.when(i >= nu_ref[0])
    def _():
        y_ref[...] = jnp.zeros_like(y_ref)


def _experts(tile_expert, n_used, src_idx, h2, w1, w3, w2):
    n_tiles, _, tm = src_idx.shape
    t, d = h2.shape
    f = w1.shape[-1]
    grid_spec = pltpu.PrefetchScalarGridSpec(
        num_scalar_prefetch=2,
        grid=(n_tiles,),
        in_specs=[
            pl.BlockSpec(memory_space=pl.ANY),
            pl.BlockSpec(memory_space=pl.ANY),
            pl.BlockSpec((1, d, f), lambda i, te, nu: (te[i], 0, 0)),
            pl.BlockSpec((1, d, f), lambda i, te, nu: (te[i], 0, 0)),
            pl.BlockSpec((1, f, d), lambda i, te, nu: (te[i], 0, 0)),
        ],
        out_specs=pl.BlockSpec((tm, d), lambda i, te, nu: (i, 0)),
        scratch_shapes=[
            pltpu.SMEM((1, tm), jnp.int32),
            pltpu.VMEM((tm, d), F32),
            pltpu.SemaphoreType.DMA(()),
            pltpu.SemaphoreType.DMA(()),
        ],
    )
    return pl.pallas_call(
        _experts_kernel,
        grid_spec=grid_spec,
        out_shape=jax.ShapeDtypeStruct((n_tiles * tm, d), F32),
        compiler_params=_cparams(1),
        name="moe_experts",
    )(tile_expert, n_used, src_idx, h2, w1, w3, w2)


def _combine_kernel(alpha, p0_hbm, p1_hbm, y_hbm, gi_ref, x_ref, g2_ref, lng_ref, lnb_ref, o_ref,
                    i0_smem, i1_smem, y0, y1, sem_idx, sem0, sem1):
    tile = pl.program_id(0) * pl.num_programs(1) + pl.program_id(1)
    _load_indices(p0_hbm, tile, i0_smem, sem_idx)
    _load_indices(p1_hbm, tile, i1_smem, sem_idx)
    n = y0.shape[0]

    def issue(r, carry):
        pltpu.make_async_copy(y_hbm.at[pl.ds(i0_smem[0, r], 1)], y0.at[pl.ds(r, 1)], sem0).start()
        pltpu.make_async_copy(y_hbm.at[pl.ds(i1_smem[0, r], 1)], y1.at[pl.ds(r, 1)], sem1).start()
        return carry

    lax.fori_loop(0, n, issue, 0)
    pltpu.make_async_copy(y_hbm.at[pl.ds(0, n)], y0, sem0).wait()
    pltpu.make_async_copy(y_hbm.at[pl.ds(0, n)], y1, sem1).wait()
    gi = gi_ref[0]
    f = gi[:, 0:1] * y0[...] + gi[:, 1:2] * y1[...]
    r = alpha * x_ref[0] + g2_ref[0] * f
    o_ref[0] = _layernorm(r, lng_ref[...], lnb_ref[...])


def _combine(alpha, pos0, pos1, y, gi, x, g2, lng, lnb):
    b, s, d = x.shape
    ts = pos0.shape[-1]
    row = lambda i, j: (i, j, 0)
    return pl.pallas_call(
        functools.partial(_combine_kernel, alpha),
        grid=(b, s // ts),
        in_specs=[
            pl.BlockSpec(memory_space=pl.ANY),
            pl.BlockSpec(memory_space=pl.ANY),
            pl.BlockSpec(memory_space=pl.ANY),
            pl.BlockSpec((1, ts, HEAD_PAD), row),
            pl.BlockSpec((1, ts, d), row),
            pl.BlockSpec((1, 1, d), lambda i, j: (i, 0, 0)),
            _const_spec(lng.shape), _const_spec(lnb.shape),
        ],
        out_specs=pl.BlockSpec((1, ts, d), row),
        out_shape=jax.ShapeDtypeStruct((b, s, d), F32),
        scratch_shapes=[
            pltpu.SMEM((1, ts), jnp.int32), pltpu.SMEM((1, ts), jnp.int32),
            pltpu.VMEM((ts, d), F32), pltpu.VMEM((ts, d), F32),
            pltpu.SemaphoreType.DMA(()), pltpu.SemaphoreType.DMA(()), pltpu.SemaphoreType.DMA(()),
        ],
        compiler_params=_cparams(2),
        name="moe_combine_ln2",
    )(pos0, pos1, y, gi, x, g2, lng, lnb)


def _moe_plan(expert_idx, tm, ts):
    t = expert_idx.shape[0]
    n_slots = t * TOP_K
    flat = expert_idx.reshape(n_slots)
    onehot = (flat[:, None] == jnp.arange(N_EXPERTS, dtype=jnp.int32)[None, :]).astype(jnp.int32)
    csum = jnp.cumsum(onehot, axis=0)
    rank = jnp.sum((csum - 1) * onehot, axis=1)
    counts = csum[-1]
    padded = (counts + tm - 1) // tm * tm
    pends = jnp.cumsum(padded)
    pstarts = pends - padded
    starts = jnp.cumsum(counts) - counts
    dest = (jnp.sum(onehot * pstarts[None, :], axis=1) + rank).astype(jnp.int32)
    n_tiles = (n_slots + N_EXPERTS * tm) // tm
    tile_expert = jnp.minimum(
        jnp.searchsorted(pends, jnp.arange(n_tiles, dtype=jnp.int32) * tm, side="right"),
        N_EXPERTS - 1).astype(jnp.int32)
    n_used = (pends[-1] // tm).astype(jnp.int32).reshape(1)
    order = jnp.argsort(flat, stable=True).astype(jnp.int32)
    n_rows = n_tiles * tm
    rows = jnp.arange(n_rows, dtype=jnp.int32)
    order_pad = jnp.concatenate([order, jnp.zeros((n_rows - n_slots,), jnp.int32)])
    src_slot = jnp.zeros((n_rows,), jnp.int32)
    for e in range(N_EXPERTS):
        shifted = jnp.roll(order_pad, pstarts[e] - starts[e])
        src_slot = jnp.where((rows >= pstarts[e]) & (rows < pstarts[e] + counts[e]), shifted, src_slot)
    src_tok = src_slot // TOP_K
    dest2 = dest.reshape(t, TOP_K)
    pos0 = dest2[:, 0].reshape(t // ts, 1, ts)
    pos1 = dest2[:, 1].reshape(t // ts, 1, ts)
    return tile_expert, n_used, src_tok.reshape(n_tiles, 1, tm), pos0, pos1


def _moe(alpha, x, sc2, sh2, g2, lng, lnb, rw, rb, w1, w3, w2):
    b, s, d = x.shape
    t = b * s
    h2, ei, gi = _router(x, sc2, sh2, rw, rb)
    tm = min(TM_MOE, t)
    ts = min(TS_COMB, s)
    tile_expert, n_used, src_idx, pos0, pos1 = _moe_plan(ei.reshape(t, HEAD_PAD)[:, :TOP_K], tm, ts)
    y = _experts(tile_expert, n_used, src_idx, h2.reshape(t, d), w1, w3, w2)
    return _combine(alpha, pos0, pos1, y, gi, x, g2, lng, lnb)


def _prep_layer_weights(w_in, w_q_up, w_kv_up, q_norm_g, kv_norm_g):
    n_l, d, _ = w_in.shape
    z96 = jnp.zeros((n_l, d, HEAD_PAD - MLA_ROPE), w_in.dtype)
    o_kr = MLA_Q_LORA + MLA_KV_LORA
    half = MLA_ROPE // 2
    kr_a = w_in[:, :, o_kr:o_kr + half]
    kr_b = w_in[:, :, o_kr + half:o_kr + MLA_ROPE]
    wlat = jnp.concatenate([w_in[:, :, :o_kr + MLA_ROPE], z96, kr_b, kr_a, z96], axis=-1)
    o = o_kr + MLA_ROPE
    nd = DIFF_HEADS * 2 * DIFF_HD
    wdq = w_in[:, :, o:o + nd]
    wdk = w_in[:, :, o + nd:o + 2 * nd]
    wdv = w_in[:, :, o + 2 * nd:o + 3 * nd]
    wga = w_in[:, :, o + 3 * nd:o + 3 * nd + d]
    wgb = w_in[:, :, o + 3 * nd + d:o + 3 * nd + 2 * d]

    hq = MLA_NOPE + MLA_ROPE
    wq4 = w_q_up.reshape(n_l, MLA_Q_LORA, MLA_HEADS, hq)
    zq = jnp.zeros((n_l, MLA_Q_LORA, MLA_HEADS, HEAD_PAD - hq), w_q_up.dtype)
    wq = jnp.concatenate([wq4, zq], axis=-1).reshape(n_l, MLA_Q_LORA, MLA_HEADS * HEAD_PAD)
    wqs = jnp.concatenate([jnp.zeros_like(wq4[..., :MLA_NOPE]), wq4[..., MLA_NOPE + half:],
                           wq4[..., MLA_NOPE:MLA_NOPE + half], zq], axis=-1)
    wqs = wqs.reshape(n_l, MLA_Q_LORA, MLA_HEADS * HEAD_PAD)

    wkv4 = w_kv_up.reshape(n_l, MLA_KV_LORA, MLA_HEADS, MLA_NOPE + MLA_V)
    wkn = jnp.concatenate([wkv4[..., :MLA_NOPE], jnp.zeros_like(wkv4[..., :HEAD_PAD - MLA_NOPE])], axis=-1)
    wkn = wkn.reshape(n_l, MLA_KV_LORA, MLA_HEADS * HEAD_PAD)
    wv = wkv4[..., MLA_NOPE:].reshape(n_l, MLA_KV_LORA, MLA_HEADS * MLA_V)

    rr = jnp.arange(HEAD_PAD)[:, None]
    cc = jnp.arange(MLA_HEADS * HEAD_PAD)[None, :]
    esel = ((rr < MLA_ROPE) & (cc % HEAD_PAD == MLA_NOPE + rr)).astype(BF16)

    cast = lambda a: a.astype(BF16)
    return dict(wlat=cast(wlat), wdq=cast(wdq), wdk=cast(wdk), wdv=cast(wdv), wga=cast(wga), wgb=cast(wgb),
                wq=cast(wq), wqs=cast(wqs), wkn=cast(wkn), wv=cast(wv), esel=esel,
                qg=q_norm_g.reshape(n_l, 1, MLA_Q_LORA), kvg=kv_norm_g.reshape(n_l, 1, MLA_KV_LORA))


def _rope_tables(positions):
    inv_freq = ROPE_BASE ** (-jnp.arange(0, MLA_ROPE, 2, dtype=F32) / MLA_ROPE)
    ang = positions.astype(F32)[..., None] * inv_freq
    cos, sin = jnp.cos(ang), jnp.sin(ang)
    ones = jnp.ones(positions.shape + (MLA_NOPE,), F32)
    tail = HEAD_PAD - MLA_NOPE - MLA_ROPE
    ct = jnp.concatenate([ones, cos, cos, ones[..., :tail]], axis=-1)
    st = jnp.concatenate([0.0 * ones, -sin, sin, 0.0 * ones[..., :tail]], axis=-1)
    return ct, st


def kernel(x, c, positions, w_ada, b_ada, w_in, q_norm_g, w_q_up, kv_norm_g, w_kv_up, lambda_q1, lambda_k1, lambda_q2, lambda_k2, diff_norm_g, w_br_mla, w_br_diff, w_out, ln1_g, ln1_b, ln2_g, ln2_b, ffn_w1, ffn_w3, ffn_w2, router_w, router_b, moe_w1, moe_w3, moe_w2):
    b, s, d = x.shape
    depth = w_in.shape[0]
    alpha = (2.0 * depth) ** 0.25

    mod = _ada_mod(c, w_ada, b_ada)
    ct, st = _rope_tables(positions)
    posf = positions.astype(F32)
    pos_k = posf.reshape(b, s, 1)
    pos_q = posf.reshape(b, 1, s)
    slopes = (2.0 ** (-8.0 * jnp.arange(1, DIFF_HEADS + 1, dtype=F32) / DIFF_HEADS)) * LOG2E

    lw = _prep_layer_weights(w_in, w_q_up, w_kv_up, q_norm_g, kv_norm_g)
    wbm, wbd, wo = w_br_mla.astype(BF16), w_br_diff.astype(BF16), w_out.astype(BF16)
    fw1, fw3, fw2 = ffn_w1.astype(BF16), ffn_w3.astype(BF16), ffn_w2.astype(BF16)
    mw1, mw3, mw2 = moe_w1.astype(BF16), moe_w3.astype(BF16), moe_w2.astype(BF16)
    rw = jnp.pad(router_w, ((0, 0), (0, 0), (0, HEAD_PAD - N_EXPERTS))).astype(BF16)
    rb = jnp.pad(router_b, ((0, 0), (0, HEAD_PAD - N_EXPERTS))).reshape(-1, 1, HEAD_PAD)
    gcol = diff_norm_g.reshape(depth, DIFF_HEADS, DIFF_VD, 1)
    vec = lambda a, l: a[l].reshape(1, -1)

    for l in range(depth):
        sh1, sc1, g1, sh2, sc2, g2 = [m.reshape(b, 1, d) for m in jnp.split(mod[l], 6, axis=-1)]
        w_l = {k: v[l] if k != "esel" else v for k, v in lw.items()}
        qm, km, vt, dq1, dq2, dk, dvt, ga, gb = _inproj(x, sc1, sh1, ct, st, w_l)
        mla_o = _mla_attention(qm, km, vt)
        lam_init = 0.8 - 0.6 * math.exp(-0.3 * l)
        diff_o = _diff_attention(dq1, dq2, dk, dvt, pos_k, pos_q, slopes,
                                 vec(lambda_q1, l), vec(lambda_k1, l), vec(lambda_q2, l), vec(lambda_k2, l),
                                 gcol[l], lam_init)
        x = _mix(alpha, mla_o, diff_o, ga, gb, x, g1, vec(ln1_g, l), vec(ln1_b, l), wbm[l], wbd[l], wo[l])
        if l % 2 == 0:
            x = _ffn(alpha, x, sc2, sh2, g2, vec(ln2_g, l), vec(ln2_b, l), fw1[l // 2], fw3[l // 2], fw2[l // 2])
        else:
            x = _moe(alpha, x, sc2, sh2, g2, vec(ln2_g, l), vec(ln2_b, l),
                     rw[l // 2], rb[l // 2], mw1[l // 2], mw3[l // 2], mw2[l // 2])
    return x
```

```python
import functools
import math

import jax
import jax.numpy as jnp
from jax import lax
from jax.experimental import pallas as pl
from jax.experimental.pallas import tpu as pltpu

BF16 = jnp.bfloat16
F32 = jnp.float32

D_MODEL = 1024
MLA_HEADS = 8
MLA_NOPE = 64
MLA_ROPE = 32
MLA_V = 64
MLA_Q_LORA = 384
MLA_KV_LORA = 256
ROPE_BASE = 10000.0
DIFF_HEADS = 8
DIFF_HD = 64
DIFF_VD = 2 * DIFF_HD
D_FF = 2816
N_EXPERTS = 8
TOP_K = 2
LN_EPS = 1e-5
RMS_EPS = 1e-6

HEAD_PAD = 128
ONES_ROWS = 16
LOG2E = 1.4426950408889634
MLA_QSCALE = (MLA_NOPE + MLA_ROPE) ** -0.5 * LOG2E
DIFF_QSCALE = DIFF_HD ** -0.5 * LOG2E

VMEM_LIMIT = 56 * 1024 * 1024

TS_PROJ = 256
TQ_ATTN = 256
TS_MIX = 512
TS_FFN = 256
TM_MOE = 256
TS_ROUTE = 512
TS_COMB = 512


def _cparams(n_axes):
    return pltpu.CompilerParams(dimension_semantics=("arbitrary",) * n_axes,
                                vmem_limit_bytes=VMEM_LIMIT)


def _const_spec(shape):
    nd = len(shape)
    return pl.BlockSpec(shape, lambda *_: (0,) * nd, pipeline_mode=pl.Buffered(1))


def _dot(a, b):
    return jnp.dot(a, b, preferred_element_type=F32)


def _dot_nt(a, b):
    return lax.dot_general(a, b, (((1,), (1,)), ((), ())), preferred_element_type=F32)


def _sigmoid(v):
    return 1.0 / (1.0 + jnp.exp(-v))


def _layernorm(r, g, b):
    mu = jnp.mean(r, axis=-1, keepdims=True)
    d = r - mu
    var = jnp.mean(d * d, axis=-1, keepdims=True)
    return d * lax.rsqrt(var + LN_EPS) * g + b


def _rms_rows(v, g):
    ms = jnp.mean(v * v, axis=-1, keepdims=True)
    return v * lax.rsqrt(ms + RMS_EPS) * g


def _ada_kernel(c_ref, w_ref, b_ref, o_ref):
    c = c_ref[...]
    cond = c * _sigmoid(c)
    o_ref[0] = _dot(cond.astype(BF16), w_ref[0].astype(BF16)) + b_ref[0]


def _ada_mod(c, w_ada, b_ada):
    n_l, d, n6 = w_ada.shape
    b = c.shape[0]
    tn = 1536
    return pl.pallas_call(
        _ada_kernel,
        grid=(n_l, n6 // tn),
        in_specs=[
            pl.BlockSpec((b, d), lambda l, j: (0, 0)),
            pl.BlockSpec((1, d, tn), lambda l, j: (l, 0, j)),
            pl.BlockSpec((1, 1, tn), lambda l, j: (l, 0, j)),
        ],
        out_specs=pl.BlockSpec((1, b, tn), lambda l, j: (l, 0, j)),
        out_shape=jax.ShapeDtypeStruct((n_l, b, n6), F32),
        compiler_params=_cparams(2),
        name="ada_mod",
    )(c, w_ada, b_ada.reshape(n_l, 1, n6))


def _inproj_kernel(x_ref, sc_ref, sh_ref, ct_ref, st_ref,
                   wlat_ref, wdq_ref, wdk_ref, wdv_ref, wga_ref, wgb_ref,
                   qg_ref, kvg_ref, wq_ref, wqs_ref, wkn_ref, wv_ref, esel_ref,
                   qm_ref, km_ref, vt_ref, dq1_ref, dq2_ref, dk_ref, dvt_ref, ga_ref, gb_ref):
    x = x_ref[0]
    h = (x * (1.0 + sc_ref[0]) + sh_ref[0]).astype(BF16)
    ct = ct_ref[0]
    st = st_ref[0]

    lat = _dot(h, wlat_ref[...])
    q_lat = lat[:, :MLA_Q_LORA]
    kv_lat = lat[:, MLA_Q_LORA:MLA_Q_LORA + MLA_KV_LORA]
    kr = lat[:, 640:768]
    krs = lat[:, 768:896]

    qn = _rms_rows(q_lat, qg_ref[...]).astype(BF16)
    q = _dot(qn, wq_ref[...])
    qs = _dot(qn, wqs_ref[...])
    c8 = jnp.concatenate([ct] * MLA_HEADS, axis=1)
    s8 = jnp.concatenate([st] * MLA_HEADS, axis=1)
    qr = ((q * c8 + qs * s8) * MLA_QSCALE).astype(BF16)
    for hd in range(MLA_HEADS):
        qm_ref[0, hd] = qr[:, hd * HEAD_PAD:(hd + 1) * HEAD_PAD]

    kvn = _rms_rows(kv_lat, kvg_ref[...]).astype(BF16)
    kn = _dot(kvn, wkn_ref[...])
    ck = pltpu.roll(ct, 64, axis=1)
    sk = pltpu.roll(st, 64, axis=1)
    kro = (kr * ck + krs * sk).astype(BF16)
    kcat = (kn + _dot(kro, esel_ref[...])).astype(BF16)
    for hd in range(MLA_HEADS):
        km_ref[0, hd] = kcat[:, hd * HEAD_PAD:(hd + 1) * HEAD_PAD]
    v = _dot(kvn, wv_ref[...])
    vt = v.T.astype(BF16)
    ones = jnp.ones((ONES_ROWS, vt.shape[1]), BF16)
    for hd in range(MLA_HEADS):
        vt_ref[0, hd, :MLA_V, :] = vt[hd * MLA_V:(hd + 1) * MLA_V, :]
        vt_ref[0, hd, MLA_V:, :] = ones

    dq = _dot(h, wdq_ref[...]) * DIFF_QSCALE
    lane = lax.broadcasted_iota(jnp.int32, dq.shape, 1)
    first = (lane % HEAD_PAD) < DIFF_HD
    dq1 = jnp.where(first, dq, 0.0).astype(BF16)
    dq2 = jnp.where(first, 0.0, dq).astype(BF16)
    dk = _dot(h, wdk_ref[...]).astype(BF16)
    dvt = _dot(h, wdv_ref[...]).T.astype(BF16)
    for hd in range(DIFF_HEADS):
        sl = slice(hd * HEAD_PAD, (hd + 1) * HEAD_PAD)
        dq1_ref[0, hd] = dq1[:, sl]
        dq2_ref[0, hd] = dq2[:, sl]
        dk_ref[0, hd] = dk[:, sl]
        dvt_ref[0, hd, :DIFF_VD, :] = dvt[sl, :]
        dvt_ref[0, hd, DIFF_VD:, :] = ones

    ga_ref[0] = _sigmoid(_dot(h, wga_ref[...])).astype(BF16)
    gb_ref[0] = _sigmoid(_dot(h, wgb_ref[...])).astype(BF16)


def _inproj(x, sc1, sh1, ct, st, w):
    b, s, d = x.shape
    ts = min(TS_PROJ, s)
    hh = MLA_HEADS
    row = lambda i, j: (i, j, 0)
    bat = lambda i, j: (i, 0, 0)
    head_rows = pl.BlockSpec((1, hh, ts, HEAD_PAD), lambda i, j: (i, 0, j, 0))
    weights = [w["wlat"], w["wdq"], w["wdk"], w["wdv"], w["wga"], w["wgb"],
               w["qg"], w["kvg"], w["wq"], w["wqs"], w["wkn"], w["wv"], w["esel"]]
    head_shape = jax.ShapeDtypeStruct((b, hh, s, HEAD_PAD), BF16)
    return pl.pallas_call(
        _inproj_kernel,
        grid=(b, s // ts),
        in_specs=[
            pl.BlockSpec((1, ts, d), row),
            pl.BlockSpec((1, 1, d), bat),
            pl.BlockSpec((1, 1, d), bat),
            pl.BlockSpec((1, ts, HEAD_PAD), row),
            pl.BlockSpec((1, ts, HEAD_PAD), row),
        ] + [_const_spec(a.shape) for a in weights],
        out_specs=[
            head_rows, head_rows,
            pl.BlockSpec((1, hh, MLA_V + ONES_ROWS, ts), lambda i, j: (i, 0, 0, j)),
            head_rows, head_rows, head_rows,
            pl.BlockSpec((1, hh, DIFF_VD + ONES_ROWS, ts), lambda i, j: (i, 0, 0, j)),
            pl.BlockSpec((1, ts, d), row),
            pl.BlockSpec((1, ts, d), row),
        ],
        out_shape=[
            head_shape, head_shape,
            jax.ShapeDtypeStruct((b, hh, MLA_V + ONES_ROWS, s), BF16),
            head_shape, head_shape, head_shape,
            jax.ShapeDtypeStruct((b, hh, DIFF_VD + ONES_ROWS, s), BF16),
            jax.ShapeDtypeStruct((b, s, d), BF16),
            jax.ShapeDtypeStruct((b, s, d), BF16),
        ],
        compiler_params=_cparams(2),
        name="inproj",
    )(x, sc1, sh1, ct, st, *weights)


def _mla_kernel(q_ref, k_ref, vt_ref, o_ref, acc_ref, za_ref, zb_ref):
    def scores(hd, z_ref):
        s_t = _dot_nt(k_ref[0, hd], q_ref[0, hd])
        z_ref[...] = s_t
        return jnp.max(s_t, axis=0, keepdims=True)

    def values(hd, z_ref, m):
        p = jnp.exp2(z_ref[...] - m).astype(BF16)
        o_t = _dot(vt_ref[0, hd], p)
        acc_ref[pl.ds(pl.multiple_of(hd * MLA_V, MLA_V), MLA_V), :] = o_t[:MLA_V] / o_t[MLA_V:MLA_V + 1]

    def body(i, ma):
        h0 = 2 * i
        mb = scores(h0 + 1, zb_ref)
        values(h0, za_ref, ma)
        ma = scores(h0 + 2, za_ref)
        values(h0 + 1, zb_ref, mb)
        return ma

    last = MLA_HEADS - 1
    ma = lax.fori_loop(0, MLA_HEADS // 2 - 1, body, scores(0, za_ref))
    mb = scores(last, zb_ref)
    values(last - 1, za_ref, ma)
    values(last, zb_ref, mb)
    o_ref[0] = acc_ref[...].T.astype(BF16)


def _mla_attention(qm, km, vt):
    b, hh, s, _ = qm.shape
    tq = min(TQ_ATTN, s)
    return pl.pallas_call(
        _mla_kernel,
        grid=(b, s // tq),
        in_specs=[
            pl.BlockSpec((1, hh, tq, HEAD_PAD), lambda i, j: (i, 0, j, 0)),
            pl.BlockSpec((1, hh, s, HEAD_PAD), lambda i, j: (i, 0, 0, 0)),
            pl.BlockSpec((1, hh, MLA_V + ONES_ROWS, s), lambda i, j: (i, 0, 0, 0)),
        ],
        out_specs=pl.BlockSpec((1, tq, hh * MLA_V), lambda i, j: (i, j, 0)),
        out_shape=jax.ShapeDtypeStruct((b, s, hh * MLA_V), BF16),
        scratch_shapes=[pltpu.VMEM((hh * MLA_V, tq), F32), pltpu.VMEM((s, tq), F32), pltpu.VMEM((s, tq), F32)],
        compiler_params=_cparams(2),
        name="mla_attn",
    )(qm, km, vt)


def _diff_kernel(lam_init, q1_ref, q2_ref, k_ref, vt_ref, pk_ref, pq_ref, slope_ref,
                 lq1_ref, lk1_ref, lq2_ref, lk2_ref, g_ref, o_ref, acc_ref, dist_ref, za_ref, zb_ref):
    dist_ref[...] = jnp.abs(pk_ref[0] - pq_ref[0])
    lam = (jnp.exp(jnp.sum(lq1_ref[...] * lk1_ref[...], axis=1, keepdims=True))
           - jnp.exp(jnp.sum(lq2_ref[...] * lk2_ref[...], axis=1, keepdims=True))
           + lam_init)

    def scores(hd, z_ref):
        k = k_ref[0, hd]
        bias = slope_ref[hd] * dist_ref[...]
        z1 = _dot_nt(k, q1_ref[0, hd]) - bias
        z2 = _dot_nt(k, q2_ref[0, hd]) - bias
        z_ref[0] = z1
        z_ref[1] = z2
        return jnp.max(z1, axis=0, keepdims=True), jnp.max(z2, axis=0, keepdims=True)

    def values(hd, z_ref, m1, m2):
        e1 = jnp.exp2(z_ref[0] - m1).astype(BF16)
        e2 = jnp.exp2(z_ref[1] - m2).astype(BF16)
        o1 = _dot(vt_ref[0, hd], e1)
        o2 = _dot(vt_ref[0, hd], e2)
        r1 = 1.0 / o1[DIFF_VD:DIFF_VD + 1]
        r2 = lam / o2[DIFF_VD:DIFF_VD + 1]
        o_t = o1[:DIFF_VD] * r1 - o2[:DIFF_VD] * r2
        ms = jnp.mean(o_t * o_t, axis=0, keepdims=True)
        o_t = o_t * lax.rsqrt(ms + RMS_EPS) * g_ref[hd] * (1.0 - lam_init)
        acc_ref[pl.ds(pl.multiple_of(hd * DIFF_VD, DIFF_VD), DIFF_VD), :] = o_t

    def body(i, ma):
        h0 = 2 * i
        mb = scores(h0 + 1, zb_ref)
        values(h0, za_ref, *ma)
        ma = scores(h0 + 2, za_ref)
        values(h0 + 1, zb_ref, *mb)
        return ma

    last = DIFF_HEADS - 1
    ma = lax.fori_loop(0, DIFF_HEADS // 2 - 1, body, scores(0, za_ref))
    mb = scores(last, zb_ref)
    values(last - 1, za_ref, *ma)
    values(last, zb_ref, *mb)
    o_ref[0] = acc_ref[...].T.astype(BF16)


def _diff_attention(dq1, dq2, dk, dvt, pos_k, pos_q, slopes, lq1, lk1, lq2, lk2, gcol, lam_init):
    b, hh, s, _ = dk.shape
    tq = min(TQ_ATTN, s)
    head_q = pl.BlockSpec((1, hh, tq, HEAD_PAD), lambda i, j: (i, 0, j, 0))
    vec = pl.BlockSpec((1, DIFF_HD), lambda i, j: (0, 0))
    return pl.pallas_call(
        functools.partial(_diff_kernel, lam_init),
        grid=(b, s // tq),
        in_specs=[
            head_q, head_q,
            pl.BlockSpec((1, hh, s, HEAD_PAD), lambda i, j: (i, 0, 0, 0)),
            pl.BlockSpec((1, hh, DIFF_VD + ONES_ROWS, s), lambda i, j: (i, 0, 0, 0)),
            pl.BlockSpec((1, s, 1), lambda i, j: (i, 0, 0)),
            pl.BlockSpec((1, 1, tq), lambda i, j: (i, 0, j)),
            pl.BlockSpec(memory_space=pltpu.SMEM),
            vec, vec, vec, vec,
            pl.BlockSpec((hh, DIFF_VD, 1), lambda i, j: (0, 0, 0)),
        ],
        out_specs=pl.BlockSpec((1, tq, hh * DIFF_VD), lambda i, j: (i, j, 0)),
        out_shape=jax.ShapeDtypeStruct((b, s, hh * DIFF_VD), BF16),
        scratch_shapes=[pltpu.VMEM((hh * DIFF_VD, tq), F32), pltpu.VMEM((s, tq), F32),
                        pltpu.VMEM((2, s, tq), F32), pltpu.VMEM((2, s, tq), F32)],
        compiler_params=_cparams(2),
        name="diff_attn",
    )(dq1, dq2, dk, dvt, pos_k, pos_q, slopes, lq1, lk1, lq2, lk2, gcol)


def _mix_kernel(alpha, mo_ref, do_ref, ga_ref, gb_ref, x_ref, g1_ref, lng_ref, lnb_ref,
                wbm_ref, wbd_ref, wo_ref, o_ref):
    ya = _dot(mo_ref[0], wbm_ref[...])
    yb = _dot(do_ref[0], wbd_ref[...])
    gated = (ga_ref[0].astype(F32) * ya + gb_ref[0].astype(F32) * yb).astype(BF16)
    mix = _dot(gated, wo_ref[...])
    r = alpha * x_ref[0] + g1_ref[0] * mix
    o_ref[0] = _layernorm(r, lng_ref[...], lnb_ref[...])


def _mix(alpha, mla_o, diff_o, ga, gb, x, g1, lng, lnb, wbm, wbd, wo):
    b, s, d = x.shape
    ts = min(TS_MIX, s)
    row = lambda i, j: (i, j, 0)
    return pl.pallas_call(
        functools.partial(_mix_kernel, alpha),
        grid=(b, s // ts),
        in_specs=[
            pl.BlockSpec((1, ts, mla_o.shape[-1]), row),
            pl.BlockSpec((1, ts, diff_o.shape[-1]), row),
            pl.BlockSpec((1, ts, d), row),
            pl.BlockSpec((1, ts, d), row),
            pl.BlockSpec((1, ts, d), row),
            pl.BlockSpec((1, 1, d), lambda i, j: (i, 0, 0)),
            _const_spec(lng.shape), _const_spec(lnb.shape),
            _const_spec(wbm.shape), _const_spec(wbd.shape), _const_spec(wo.shape),
        ],
        out_specs=pl.BlockSpec((1, ts, d), row),
        out_shape=jax.ShapeDtypeStruct((b, s, d), F32),
        compiler_params=_cparams(2),
        name="mix_ln1",
    )(mla_o, diff_o, ga, gb, x, g1, lng, lnb, wbm, wbd, wo)


def _swiglu(h, w1, w3, w2):
    a = _dot(h, w1)
    bgate = _dot(h, w3)
    u = (a * _sigmoid(a) * bgate).astype(BF16)
    return _dot(u, w2)


def _ffn_kernel(alpha, x_ref, sc_ref, sh_ref, g2_ref, lng_ref, lnb_ref, w1_ref, w3_ref, w2_ref, o_ref):
    x = x_ref[0]
    h = (x * (1.0 + sc_ref[0]) + sh_ref[0]).astype(BF16)
    f = _swiglu(h, w1_ref[...], w3_ref[...], w2_ref[...])
    r = alpha * x + g2_ref[0] * f
    o_ref[0] = _layernorm(r, lng_ref[...], lnb_ref[...])


def _ffn(alpha, x, sc2, sh2, g2, lng, lnb, w1, w3, w2):
    b, s, d = x.shape
    ts = min(TS_FFN, s)
    row = lambda i, j: (i, j, 0)
    bat = lambda i, j: (i, 0, 0)
    return pl.pallas_call(
        functools.partial(_ffn_kernel, alpha),
        grid=(b, s // ts),
        in_specs=[
            pl.BlockSpec((1, ts, d), row),
            pl.BlockSpec((1, 1, d), bat), pl.BlockSpec((1, 1, d), bat), pl.BlockSpec((1, 1, d), bat),
            _const_spec(lng.shape), _const_spec(lnb.shape),
            _const_spec(w1.shape), _const_spec(w3.shape), _const_spec(w2.shape),
        ],
        out_specs=pl.BlockSpec((1, ts, d), row),
        out_shape=jax.ShapeDtypeStruct((b, s, d), F32),
        compiler_params=_cparams(2),
        name="ffn_ln2",
    )(x, sc2, sh2, g2, lng, lnb, w1, w3, w2)


def _router_kernel(x_ref, sc_ref, sh_ref, rw_ref, rb_ref, h_ref, ei_ref, gi_ref):
    h = x_ref[0] * (1.0 + sc_ref[0]) + sh_ref[0]
    h_ref[0] = h
    logits = _dot(h.astype(BF16), rw_ref[...]) + rb_ref[...]
    lane = lax.broadcasted_iota(jnp.int32, logits.shape, 1)
    lane_f = lane.astype(F32)
    neg = jnp.float32(-jnp.inf)
    lg = jnp.where(lane < N_EXPERTS, logits, neg)
    m1 = jnp.max(lg, axis=1, keepdims=True)
    i1 = jnp.min(jnp.where(lg == m1, lane_f, 128.0), axis=1, keepdims=True)
    lg2 = jnp.where(lane_f == i1, neg, lg)
    m2 = jnp.max(lg2, axis=1, keepdims=True)
    i2 = jnp.min(jnp.where(lg2 == m2, lane_f, 128.0), axis=1, keepdims=True)
    t = jnp.exp(m2 - m1)
    den = 1.0 + t
    ei_ref[0] = jnp.where(lane == 0, i1, jnp.where(lane == 1, i2, 0.0)).astype(jnp.int32)
    gi_ref[0] = jnp.where(lane == 0, 1.0 / den, jnp.where(lane == 1, t / den, 0.0))


def _router(x, sc2, sh2, rw, rb):
    b, s, d = x.shape
    ts = min(TS_ROUTE, s)
    row = lambda i, j: (i, j, 0)
    bat = lambda i, j: (i, 0, 0)
    return pl.pallas_call(
        _router_kernel,
        grid=(b, s // ts),
        in_specs=[
            pl.BlockSpec((1, ts, d), row),
            pl.BlockSpec((1, 1, d), bat), pl.BlockSpec((1, 1, d), bat),
            _const_spec(rw.shape), _const_spec(rb.shape),
        ],
        out_specs=[pl.BlockSpec((1, ts, d), row),
                   pl.BlockSpec((1, ts, HEAD_PAD), row),
                   pl.BlockSpec((1, ts, HEAD_PAD), row)],
        out_shape=[jax.ShapeDtypeStruct((b, s, d), F32),
                   jax.ShapeDtypeStruct((b, s, HEAD_PAD), jnp.int32),
                   jax.ShapeDtypeStruct((b, s, HEAD_PAD), F32)],
        compiler_params=_cparams(2),
        name="moe_router",
    )(x, sc2, sh2, rw, rb)


def _experts_kernel(te_ref, nu_ref, idx_hbm, h_hbm, w1_ref, w3_ref, w2_ref, y_hbm,
                    idx_smem, xbuf, ybuf, sem_idx, sem_rows, sem_out):
    i = pl.program_id(0)
    n_used = nu_ref[0]
    n_tiles = pl.num_programs(0)
    tm = xbuf.shape[1]
    slot = lax.rem(i, 2)
    n_idx = idx_smem.shape[0]

    def index_copy(tile, step):
        return pltpu.make_async_copy(idx_hbm.at[tile], idx_smem.at[pl.ds(lax.rem(step, n_idx), 1)], sem_idx)

    def gather_copy(step, r):
        row, buf = lax.rem(step, n_idx), lax.rem(step, 2)
        return pltpu.make_async_copy(h_hbm.at[pl.ds(idx_smem[row, r], 1)],
                                     xbuf.at[buf, pl.ds(r, 1)], sem_rows.at[buf])

    def scatter_copy(step, r):
        row, buf = lax.rem(step, n_idx), lax.rem(step, 2)
        return pltpu.make_async_copy(ybuf.at[buf, pl.ds(r, 1)],
                                     y_hbm.at[pl.ds(idx_smem[row, tm + r], 1)], sem_out.at[buf])

    def wait_gather(buf):
        pltpu.make_async_copy(h_hbm.at[pl.ds(0, tm)], xbuf.at[buf], sem_rows.at[buf]).wait()

    def wait_scatter(buf):
        pltpu.make_async_copy(ybuf.at[buf], y_hbm.at[pl.ds(0, tm)], sem_out.at[buf]).wait()

    @pl.when(i == 0)
    def _():
        for tile, step in ((0, 0), (jnp.minimum(1, n_tiles - 1), 1), (n_tiles, n_idx - 1)):
            index_copy(tile, step).start()
            index_copy(tile, step).wait()
        lax.fori_loop(0, tm, lambda r, c: (gather_copy(0, r).start(), c)[1], 0)
        ybuf[1] = jnp.zeros(ybuf.shape[1:], F32)

    @pl.when(i < n_used)
    def _():
        wait_gather(slot)
        for r in range(tm):
            gather_copy(i + 1, r).start()
        nxt2 = jnp.minimum(i + 2, n_tiles - 1)
        index_copy(nxt2, i + 2).start()
        for r in range(tm):
            scatter_copy(i + n_idx - 1, r).start()
        ybuf[slot] = _swiglu(xbuf[slot].astype(BF16), w1_ref[0], w3_ref[0], w2_ref[0])
        wait_scatter(1 - slot)
        index_copy(nxt2, i + 2).wait()

        @pl.when(i + 1 == n_used)
        def _():
            lax.fori_loop(0, tm, lambda r, c: (scatter_copy(i, r).start(), c)[1], 0)
            wait_scatter(slot)
            wait_gather(1 - slot)


def _experts(tile_expert, n_used, idx, h2, w1, w3, w2):
    n_tiles, tm2 = idx.shape[0] - 1, idx.shape[-1]
    tm = tm2 // 2
    t, d = h2.shape
    f = w1.shape[-1]
    grid_spec = pltpu.PrefetchScalarGridSpec(
        num_scalar_prefetch=2,
        grid=(n_tiles,),
        in_specs=[
            pl.BlockSpec(memory_space=pl.ANY),
            pl.BlockSpec(memory_space=pl.ANY),
            pl.BlockSpec((1, d, f), lambda i, te, nu: (te[i], 0, 0)),
            pl.BlockSpec((1, d, f), lambda i, te, nu: (te[i], 0, 0)),
            pl.BlockSpec((1, f, d), lambda i, te, nu: (te[i], 0, 0)),
        ],
        out_specs=pl.BlockSpec(memory_space=pl.ANY),
        scratch_shapes=[
            pltpu.SMEM((4, 2 * tm), jnp.int32),
            pltpu.VMEM((2, tm, d), F32),
            pltpu.VMEM((2, tm, d), F32),
            pltpu.SemaphoreType.DMA(()),
            pltpu.SemaphoreType.DMA((2,)),
            pltpu.SemaphoreType.DMA((2,)),
        ],
    )
    return pl.pallas_call(
        _experts_kernel,
        grid_spec=grid_spec,
        out_shape=jax.ShapeDtypeStruct((TOP_K * t + tm, d), F32),
        compiler_params=_cparams(1),
        name="moe_experts",
    )(tile_expert, n_used, idx, h2, w1, w3, w2)


def _combine_kernel(alpha, y0_ref, y1_ref, gi_ref, x_ref, g2_ref, lng_ref, lnb_ref, o_ref):
    gi = gi_ref[0]
    f = gi[:, 0:1] * y0_ref[...] + gi[:, 1:2] * y1_ref[...]
    r = alpha * x_ref[0] + g2_ref[0] * f
    o_ref[0] = _layernorm(r, lng_ref[...], lnb_ref[...])


def _combine(alpha, y, gi, x, g2, lng, lnb):
    b, s, d = x.shape
    ts = min(TS_COMB, s)
    nj = s // ts
    row = lambda i, j: (i, j, 0)
    return pl.pallas_call(
        functools.partial(_combine_kernel, alpha),
        grid=(b, nj),
        in_specs=[
            pl.BlockSpec((ts, d), lambda i, j: (i * nj + j, 0)),
            pl.BlockSpec((ts, d), lambda i, j: (b * nj + i * nj + j, 0)),
            pl.BlockSpec((1, ts, HEAD_PAD), row),
            pl.BlockSpec((1, ts, d), row),
            pl.BlockSpec((1, 1, d), lambda i, j: (i, 0, 0)),
            _const_spec(lng.shape), _const_spec(lnb.shape),
        ],
        out_specs=pl.BlockSpec((1, ts, d), row),
        out_shape=jax.ShapeDtypeStruct((b, s, d), F32),
        compiler_params=_cparams(2),
        name="moe_combine_ln2",
    )(y, y, gi, x, g2, lng, lnb)


def _moe_plan(expert_idx, tm):
    t = expert_idx.shape[0]
    n_slots = t * TOP_K
    flat = expert_idx.reshape(n_slots)
    counts = jnp.sum((flat[:, None] == jnp.arange(N_EXPERTS, dtype=jnp.int32)[None, :]).astype(jnp.int32), axis=0)
    padded = (counts + tm - 1) // tm * tm
    pends = jnp.cumsum(padded)
    pstarts = pends - padded
    starts = jnp.cumsum(counts) - counts
    n_tiles = (n_slots + N_EXPERTS * tm) // tm
    tile_expert = jnp.minimum(
        jnp.searchsorted(pends, jnp.arange(n_tiles, dtype=jnp.int32) * tm, side="right"),
        N_EXPERTS - 1).astype(jnp.int32)
    n_used = (pends[-1] // tm).astype(jnp.int32).reshape(1)
    order = jnp.argsort(flat, stable=True).astype(jnp.int32)
    n_rows = n_tiles * tm
    rows = jnp.arange(n_rows, dtype=jnp.int32)
    order_pad = jnp.concatenate([order, jnp.zeros((n_rows - n_slots,), jnp.int32)])
    slot_of_row = jnp.full((n_rows,), -1, jnp.int32)
    for e in range(N_EXPERTS):
        shifted = jnp.roll(order_pad, pstarts[e] - starts[e])
        slot_of_row = jnp.where((rows >= pstarts[e]) & (rows < pstarts[e] + counts[e]), shifted, slot_of_row)
    valid = slot_of_row >= 0
    tok = slot_of_row // TOP_K
    src = jnp.where(valid, tok, 0)
    dst = jnp.where(valid, (slot_of_row % TOP_K) * t + tok, n_slots + rows % tm)
    idx = jnp.concatenate([src.reshape(n_tiles, 1, tm), dst.reshape(n_tiles, 1, tm)], axis=-1)
    dummy = jnp.concatenate([jnp.zeros((1, 1, tm), jnp.int32),
                             (n_slots + jnp.arange(tm, dtype=jnp.int32)).reshape(1, 1, tm)], axis=-1)
    return tile_expert, n_used, jnp.concatenate([idx.astype(jnp.int32), dummy], axis=0)


def _moe(alpha, x, sc2, sh2, g2, lng, lnb, rw, rb, w1, w3, w2):
    b, s, d = x.shape
    t = b * s
    h2, ei, gi = _router(x, sc2, sh2, rw, rb)
    tm = min(TM_MOE, t)
    tile_expert, n_used, idx = _moe_plan(ei.reshape(t, HEAD_PAD)[:, :TOP_K], tm)
    y = _experts(tile_expert, n_used, idx, h2.reshape(t, d), w1, w3, w2)
    return _combine(alpha, y, gi, x, g2, lng, lnb)


def _prep_layer_weights(w_in, w_q_up, w_kv_up, q_norm_g, kv_norm_g):
    n_l, d, _ = w_in.shape
    z96 = jnp.zeros((n_l, d, HEAD_PAD - MLA_ROPE), w_in.dtype)
    o_kr = MLA_Q_LORA + MLA_KV_LORA
    half = MLA_ROPE // 2
    kr_a = w_in[:, :, o_kr:o_kr + half]
    kr_b = w_in[:, :, o_kr + half:o_kr + MLA_ROPE]
    wlat = jnp.concatenate([w_in[:, :, :o_kr + MLA_ROPE], z96, kr_b, kr_a, z96], axis=-1)
    o = o_kr + MLA_ROPE
    nd = DIFF_HEADS * 2 * DIFF_HD
    wdq = w_in[:, :, o:o + nd]
    wdk = w_in[:, :, o + nd:o + 2 * nd]
    wdv = w_in[:, :, o + 2 * nd:o + 3 * nd]
    wga = w_in[:, :, o + 3 * nd:o + 3 * nd + d]
    wgb = w_in[:, :, o + 3 * nd + d:o + 3 * nd + 2 * d]

    hq = MLA_NOPE + MLA_ROPE
    wq4 = w_q_up.reshape(n_l, MLA_Q_LORA, MLA_HEADS, hq)
    zq = jnp.zeros((n_l, MLA_Q_LORA, MLA_HEADS, HEAD_PAD - hq), w_q_up.dtype)
    wq = jnp.concatenate([wq4, zq], axis=-1).reshape(n_l, MLA_Q_LORA, MLA_HEADS * HEAD_PAD)
    wqs = jnp.concatenate([jnp.zeros_like(wq4[..., :MLA_NOPE]), wq4[..., MLA_NOPE + half:],
                           wq4[..., MLA_NOPE:MLA_NOPE + half], zq], axis=-1)
    wqs = wqs.reshape(n_l, MLA_Q_LORA, MLA_HEADS * HEAD_PAD)

    wkv4 = w_kv_up.reshape(n_l, MLA_KV_LORA, MLA_HEADS, MLA_NOPE + MLA_V)
    wkn = jnp.concatenate([wkv4[..., :MLA_NOPE], jnp.zeros_like(wkv4[..., :HEAD_PAD - MLA_NOPE])], axis=-1)
    wkn = wkn.reshape(n_l, MLA_KV_LORA, MLA_HEADS * HEAD_PAD)
    wv = wkv4[..., MLA_NOPE:].reshape(n_l, MLA_KV_LORA, MLA_HEADS * MLA_V)

    rr = jnp.arange(HEAD_PAD)[:, None]
    cc = jnp.arange(MLA_HEADS * HEAD_PAD)[None, :]
    esel = ((rr < MLA_ROPE) & (cc % HEAD_PAD == MLA_NOPE + rr)).astype(BF16)

    cast = lambda a: a.astype(BF16)
    return dict(wlat=cast(wlat), wdq=cast(wdq), wdk=cast(wdk), wdv=cast(wdv), wga=cast(wga), wgb=cast(wgb),
                wq=cast(wq), wqs=cast(wqs), wkn=cast(wkn), wv=cast(wv), esel=esel,
                qg=q_norm_g.reshape(n_l, 1, MLA_Q_LORA), kvg=kv_norm_g.reshape(n_l, 1, MLA_KV_LORA))


def _rope_tables(positions):
    inv_freq = ROPE_BASE ** (-jnp.arange(0, MLA_ROPE, 2, dtype=F32) / MLA_ROPE)
    ang = positions.astype(F32)[..., None] * inv_freq
    cos, sin = jnp.cos(ang), jnp.sin(ang)
    ones = jnp.ones(positions.shape + (MLA_NOPE,), F32)
    tail = HEAD_PAD - MLA_NOPE - MLA_ROPE
    ct = jnp.concatenate([ones, cos, cos, ones[..., :tail]], axis=-1)
    st = jnp.concatenate([0.0 * ones, -sin, sin, 0.0 * ones[..., :tail]], axis=-1)
    return ct, st


def kernel(x, c, positions, w_ada, b_ada, w_in, q_norm_g, w_q_up, kv_norm_g, w_kv_up, lambda_q1, lambda_k1, lambda_q2, lambda_k2, diff_norm_g, w_br_mla, w_br_diff, w_out, ln1_g, ln1_b, ln2_g, ln2_b, ffn_w1, ffn_w3, ffn_w2, router_w, router_b, moe_w1, moe_w3, moe_w2):
    b, s, d = x.shape
    depth = w_in.shape[0]
    alpha = (2.0 * depth) ** 0.25

    mod = _ada_mod(c, w_ada, b_ada)
    ct, st = _rope_tables(positions)
    posf = positions.astype(F32)
    pos_k = posf.reshape(b, s, 1)
    pos_q = posf.reshape(b, 1, s)
    slopes = (2.0 ** (-8.0 * jnp.arange(1, DIFF_HEADS + 1, dtype=F32) / DIFF_HEADS)) * LOG2E

    lw = _prep_layer_weights(w_in, w_q_up, w_kv_up, q_norm_g, kv_norm_g)
    wbm, wbd, wo = w_br_mla.astype(BF16), w_br_diff.astype(BF16), w_out.astype(BF16)
    fw1, fw3, fw2 = ffn_w1.astype(BF16), ffn_w3.astype(BF16), ffn_w2.astype(BF16)
    mw1, mw3, mw2 = moe_w1.astype(BF16), moe_w3.astype(BF16), moe_w2.astype(BF16)
    rw = jnp.pad(router_w, ((0, 0), (0, 0), (0, HEAD_PAD - N_EXPERTS))).astype(BF16)
    rb = jnp.pad(router_b, ((0, 0), (0, HEAD_PAD - N_EXPERTS))).reshape(-1, 1, HEAD_PAD)
    gcol = diff_norm_g.reshape(depth, DIFF_HEADS, DIFF_VD, 1)
    vec = lambda a, l: a[l].reshape(1, -1)

    for l in range(depth):
        sh1, sc1, g1, sh2, sc2, g2 = [m.reshape(b, 1, d) for m in jnp.split(mod[l], 6, axis=-1)]
        w_l = {k: v[l] if k != "esel" else v for k, v in lw.items()}
        qm, km, vt, dq1, dq2, dk, dvt, ga, gb = _inproj(x, sc1, sh1, ct, st, w_l)
        mla_o = _mla_attention(qm, km, vt)
        lam_init = 0.8 - 0.6 * math.exp(-0.3 * l)
        diff_o = _diff_attention(dq1, dq2, dk, dvt, pos_k, pos_q, slopes,
                                 vec(lambda_q1, l), vec(lambda_k1, l), vec(lambda_q2, l), vec(lambda_k2, l),
                                 gcol[l], lam_init)
        x = _mix(alpha, mla_o, diff_o, ga, gb, x, g1, vec(ln1_g, l), vec(ln1_b, l), wbm[l], wbd[l], wo[l])
        if l % 2 == 0:
            x = _ffn(alpha, x, sc2, sh2, g2, vec(ln2_g, l), vec(ln2_b, l), fw1[l // 2], fw3[l // 2], fw2[l // 2])
        else:
            x = _moe(alpha, x, sc2, sh2, g2, vec(ln2_g, l), vec(ln2_b, l),
                     rw[l // 2], rb[l // 2], mw1[l // 2], mw3[l // 2], mw2[l // 2])
    return x
```

```python
import functools
import math

import jax
import jax.numpy as jnp
from jax import lax
from jax.experimental import pallas as pl
from jax.experimental.pallas import tpu as pltpu

BF16 = jnp.bfloat16
F32 = jnp.float32

D_MODEL = 1024
MLA_HEADS = 8
MLA_NOPE = 64
MLA_ROPE = 32
MLA_V = 64
MLA_Q_LORA = 384
MLA_KV_LORA = 256
ROPE_BASE = 10000.0
DIFF_HEADS = 8
DIFF_HD = 64
DIFF_VD = 2 * DIFF_HD
D_FF = 2816
N_EXPERTS = 8
TOP_K = 2
LN_EPS = 1e-5
RMS_EPS = 1e-6

HEAD_PAD = 128
ONES_ROWS = 16
LOG2E = 1.4426950408889634
MLA_QSCALE = (MLA_NOPE + MLA_ROPE) ** -0.5 * LOG2E
DIFF_QSCALE = DIFF_HD ** -0.5 * LOG2E

VMEM_LIMIT = 56 * 1024 * 1024

TS_PROJ = 256
TQ_MLA = 512
TQ_DIFF = 256
TS_MIX = 512
TS_FFN = 256
TM_MOE = 256
TS_ROUTE = 512
TS_COMB = 512


def _cparams(n_axes):
    return pltpu.CompilerParams(dimension_semantics=("arbitrary",) * n_axes,
                                vmem_limit_bytes=VMEM_LIMIT)


def _const_spec(shape):
    nd = len(shape)
    return pl.BlockSpec(shape, lambda *_: (0,) * nd, pipeline_mode=pl.Buffered(1))


def _dot(a, b):
    return jnp.dot(a, b, preferred_element_type=F32)


def _dot_nt(a, b):
    return lax.dot_general(a, b, (((1,), (1,)), ((), ())), preferred_element_type=F32)


def _sigmoid(v):
    return 1.0 / (1.0 + jnp.exp(-v))


def _layernorm(r, g, b):
    mu = jnp.mean(r, axis=-1, keepdims=True)
    d = r - mu
    var = jnp.mean(d * d, axis=-1, keepdims=True)
    return d * lax.rsqrt(var + LN_EPS) * g + b


def _rms_rows(v, g):
    ms = jnp.mean(v * v, axis=-1, keepdims=True)
    return v * lax.rsqrt(ms + RMS_EPS) * g


LANES = 128
ROW_TILE = D_MODEL // LANES


def _store_row_tiles(ref, v, lead=()):
    n = v.shape[0]
    for c in range(ROW_TILE):
        ref[lead + (pl.ds(c, n, stride=ROW_TILE), slice(None))] = v[:, c * LANES:(c + 1) * LANES]


def _load_row_tiles(ref, n, lead=()):
    return jnp.concatenate(
        [ref[lead + (pl.ds(c, n, stride=ROW_TILE), slice(None))] for c in range(ROW_TILE)], axis=1)


def _ada_kernel(c_ref, w_ref, b_ref, o_ref):
    c = c_ref[...]
    cond = c * _sigmoid(c)
    o_ref[0] = _dot(cond.astype(BF16), w_ref[0].astype(BF16)) + b_ref[0]


def _ada_mod(c, w_ada, b_ada):
    n_l, d, n6 = w_ada.shape
    b = c.shape[0]
    tn = 1536
    return pl.pallas_call(
        _ada_kernel,
        grid=(n_l, n6 // tn),
        in_specs=[
            pl.BlockSpec((b, d), lambda l, j: (0, 0)),
            pl.BlockSpec((1, d, tn), lambda l, j: (l, 0, j)),
            pl.BlockSpec((1, 1, tn), lambda l, j: (l, 0, j)),
        ],
        out_specs=pl.BlockSpec((1, b, tn), lambda l, j: (l, 0, j)),
        out_shape=jax.ShapeDtypeStruct((n_l, b, n6), F32),
        compiler_params=_cparams(2),
        name="ada_mod",
    )(c, w_ada, b_ada.reshape(n_l, 1, n6))


def _inproj_kernel(x_ref, sc_ref, sh_ref, ct_ref, st_ref,
                   wlat_ref, wdq_ref, wdk_ref, wdv_ref, wga_ref, wgb_ref,
                   qg_ref, kvg_ref, wq_ref, wqs_ref, wkn_ref, wv_ref, esel_ref,
                   qm_ref, km_ref, vt_ref, dq1_ref, dq2_ref, dk_ref, dvt_ref, ga_ref, gb_ref):
    x = x_ref[0]
    h = (x * (1.0 + sc_ref[0]) + sh_ref[0]).astype(BF16)
    ct = ct_ref[0]
    st = st_ref[0]

    lat = _dot(h, wlat_ref[...])
    q_lat = lat[:, :MLA_Q_LORA]
    kv_lat = lat[:, MLA_Q_LORA:MLA_Q_LORA + MLA_KV_LORA]
    kr = lat[:, 640:768]
    krs = lat[:, 768:896]

    qn = _rms_rows(q_lat, qg_ref[...]).astype(BF16)
    q = _dot(qn, wq_ref[...])
    qs = _dot(qn, wqs_ref[...])
    c8 = jnp.concatenate([ct] * MLA_HEADS, axis=1)
    s8 = jnp.concatenate([st] * MLA_HEADS, axis=1)
    qr = ((q * c8 + qs * s8) * MLA_QSCALE).astype(BF16)
    for hd in range(MLA_HEADS):
        qm_ref[0, hd] = qr[:, hd * HEAD_PAD:(hd + 1) * HEAD_PAD]

    kvn = _rms_rows(kv_lat, kvg_ref[...]).astype(BF16)
    kn = _dot(kvn, wkn_ref[...])
    ck = pltpu.roll(ct, 64, axis=1)
    sk = pltpu.roll(st, 64, axis=1)
    kro = (kr * ck + krs * sk).astype(BF16)
    kcat = (kn + _dot(kro, esel_ref[...])).astype(BF16)
    for hd in range(MLA_HEADS):
        km_ref[0, hd] = kcat[:, hd * HEAD_PAD:(hd + 1) * HEAD_PAD]
    v = _dot(kvn, wv_ref[...])
    vt = v.T.astype(BF16)
    ones = jnp.ones((ONES_ROWS, vt.shape[1]), BF16)
    for hd in range(MLA_HEADS):
        vt_ref[0, hd, :MLA_V, :] = vt[hd * MLA_V:(hd + 1) * MLA_V, :]
        vt_ref[0, hd, MLA_V:, :] = ones

    dq = _dot(h, wdq_ref[...]) * DIFF_QSCALE
    lane = lax.broadcasted_iota(jnp.int32, dq.shape, 1)
    first = (lane % HEAD_PAD) < DIFF_HD
    dq1 = jnp.where(first, dq, 0.0).astype(BF16)
    dq2 = jnp.where(first, 0.0, dq).astype(BF16)
    dk = _dot(h, wdk_ref[...]).astype(BF16)
    dvt = _dot(h, wdv_ref[...]).T.astype(BF16)
    for hd in range(DIFF_HEADS):
        sl = slice(hd * HEAD_PAD, (hd + 1) * HEAD_PAD)
        dq1_ref[0, hd] = dq1[:, sl]
        dq2_ref[0, hd] = dq2[:, sl]
        dk_ref[0, hd] = dk[:, sl]
        dvt_ref[0, hd, :DIFF_VD, :] = dvt[sl, :]
        dvt_ref[0, hd, DIFF_VD:, :] = ones

    ga_ref[0] = _sigmoid(_dot(h, wga_ref[...])).astype(BF16)
    gb_ref[0] = _sigmoid(_dot(h, wgb_ref[...])).astype(BF16)


def _inproj(x, sc1, sh1, ct, st, w):
    b, s, d = x.shape
    ts = min(TS_PROJ, s)
    hh = MLA_HEADS
    row = lambda i, j: (i, j, 0)
    bat = lambda i, j: (i, 0, 0)
    head_rows = pl.BlockSpec((1, hh, ts, HEAD_PAD), lambda i, j: (i, 0, j, 0))
    weights = [w["wlat"], w["wdq"], w["wdk"], w["wdv"], w["wga"], w["wgb"],
               w["qg"], w["kvg"], w["wq"], w["wqs"], w["wkn"], w["wv"], w["esel"]]
    head_shape = jax.ShapeDtypeStruct((b, hh, s, HEAD_PAD), BF16)
    return pl.pallas_call(
        _inproj_kernel,
        grid=(b, s // ts),
        in_specs=[
            pl.BlockSpec((1, ts, d), row),
            pl.BlockSpec((1, 1, d), bat),
            pl.BlockSpec((1, 1, d), bat),
            pl.BlockSpec((1, ts, HEAD_PAD), row),
            pl.BlockSpec((1, ts, HEAD_PAD), row),
        ] + [_const_spec(a.shape) for a in weights],
        out_specs=[
            head_rows, head_rows,
            pl.BlockSpec((1, hh, MLA_V + ONES_ROWS, ts), lambda i, j: (i, 0, 0, j)),
            head_rows, head_rows, head_rows,
            pl.BlockSpec((1, hh, DIFF_VD + ONES_ROWS, ts), lambda i, j: (i, 0, 0, j)),
            pl.BlockSpec((1, ts, d), row),
            pl.BlockSpec((1, ts, d), row),
        ],
        out_shape=[
            head_shape, head_shape,
            jax.ShapeDtypeStruct((b, hh, MLA_V + ONES_ROWS, s), BF16),
            head_shape, head_shape, head_shape,
            jax.ShapeDtypeStruct((b, hh, DIFF_VD + ONES_ROWS, s), BF16),
            jax.ShapeDtypeStruct((b, s, d), BF16),
            jax.ShapeDtypeStruct((b, s, d), BF16),
        ],
        compiler_params=_cparams(2),
        name="inproj",
    )(x, sc1, sh1, ct, st, *weights)


def _mla_kernel(q_ref, k_ref, vt_ref, o_ref, acc_ref, za_ref, zb_ref):
    def scores(hd, z_ref):
        s_t = _dot_nt(k_ref[0, hd], q_ref[0, hd])
        z_ref[...] = s_t
        return jnp.max(s_t, axis=0, keepdims=True)

    def values(hd, z_ref, m):
        p = jnp.exp2(z_ref[...] - m).astype(BF16)
        o_t = _dot(vt_ref[0, hd], p)
        acc_ref[pl.ds(pl.multiple_of(hd * MLA_V, MLA_V), MLA_V), :] = o_t[:MLA_V] / o_t[MLA_V:MLA_V + 1]

    def body(i, ma):
        h0 = 2 * i
        mb = scores(h0 + 1, zb_ref)
        values(h0, za_ref, ma)
        ma = scores(h0 + 2, za_ref)
        values(h0 + 1, zb_ref, mb)
        return ma

    last = MLA_HEADS - 1
    ma = lax.fori_loop(0, MLA_HEADS // 2 - 1, body, scores(0, za_ref))
    mb = scores(last, zb_ref)
    values(last - 1, za_ref, ma)
    values(last, zb_ref, mb)
    o_ref[0] = acc_ref[...].T.astype(BF16)


def _mla_attention(qm, km, vt):
    b, hh, s, _ = qm.shape
    tq = min(TQ_MLA, s)
    return pl.pallas_call(
        _mla_kernel,
        grid=(b, s // tq),
        in_specs=[
            pl.BlockSpec((1, hh, tq, HEAD_PAD), lambda i, j: (i, 0, j, 0)),
            pl.BlockSpec((1, hh, s, HEAD_PAD), lambda i, j: (i, 0, 0, 0)),
            pl.BlockSpec((1, hh, MLA_V + ONES_ROWS, s), lambda i, j: (i, 0, 0, 0)),
        ],
        out_specs=pl.BlockSpec((1, tq, hh * MLA_V), lambda i, j: (i, j, 0)),
        out_shape=jax.ShapeDtypeStruct((b, s, hh * MLA_V), BF16),
        scratch_shapes=[pltpu.VMEM((hh * MLA_V, tq), F32), pltpu.VMEM((s, tq), F32), pltpu.VMEM((s, tq), F32)],
        compiler_params=_cparams(2),
        name="mla_attn",
    )(qm, km, vt)


def _diff_kernel(lam_init, q1_ref, q2_ref, k_ref, vt_ref, pk_ref, pq_ref, slope_ref,
                 lq1_ref, lk1_ref, lq2_ref, lk2_ref, g_ref, o_ref, acc_ref, dist_ref, za_ref, zb_ref):
    dist_ref[...] = jnp.abs(pk_ref[0] - pq_ref[0])
    lam = (jnp.exp(jnp.sum(lq1_ref[...] * lk1_ref[...], axis=1, keepdims=True))
           - jnp.exp(jnp.sum(lq2_ref[...] * lk2_ref[...], axis=1, keepdims=True))
           + lam_init)

    def scores(hd, z_ref):
        k = k_ref[0, hd]
        bias = slope_ref[hd] * dist_ref[...]
        z1 = _dot_nt(k, q1_ref[0, hd]) - bias
        z2 = _dot_nt(k, q2_ref[0, hd]) - bias
        z_ref[0] = z1
        z_ref[1] = z2
        return jnp.max(z1, axis=0, keepdims=True), jnp.max(z2, axis=0, keepdims=True)

    def values(hd, z_ref, m1, m2):
        e1 = jnp.exp2(z_ref[0] - m1).astype(BF16)
        e2 = jnp.exp2(z_ref[1] - m2).astype(BF16)
        o1 = _dot(vt_ref[0, hd], e1)
        o2 = _dot(vt_ref[0, hd], e2)
        r1 = 1.0 / o1[DIFF_VD:DIFF_VD + 1]
        r2 = lam / o2[DIFF_VD:DIFF_VD + 1]
        o_t = o1[:DIFF_VD] * r1 - o2[:DIFF_VD] * r2
        ms = jnp.mean(o_t * o_t, axis=0, keepdims=True)
        o_t = o_t * lax.rsqrt(ms + RMS_EPS) * g_ref[hd] * (1.0 - lam_init)
        acc_ref[pl.ds(pl.multiple_of(hd * DIFF_VD, DIFF_VD), DIFF_VD), :] = o_t

    def body(i, ma):
        h0 = 2 * i
        mb = scores(h0 + 1, zb_ref)
        values(h0, za_ref, *ma)
        ma = scores(h0 + 2, za_ref)
        values(h0 + 1, zb_ref, *mb)
        return ma

    last = DIFF_HEADS - 1
    ma = lax.fori_loop(0, DIFF_HEADS // 2 - 1, body, scores(0, za_ref))
    mb = scores(last, zb_ref)
    values(last - 1, za_ref, *ma)
    values(last, zb_ref, *mb)
    o_ref[0] = acc_ref[...].T.astype(BF16)


def _diff_attention(dq1, dq2, dk, dvt, pos_k, pos_q, slopes, lq1, lk1, lq2, lk2, gcol, lam_init):
    b, hh, s, _ = dk.shape
    tq = min(TQ_DIFF, s)
    head_q = pl.BlockSpec((1, hh, tq, HEAD_PAD), lambda i, j: (i, 0, j, 0))
    vec = pl.BlockSpec((1, DIFF_HD), lambda i, j: (0, 0))
    return pl.pallas_call(
        functools.partial(_diff_kernel, lam_init),
        grid=(b, s // tq),
        in_specs=[
            head_q, head_q,
            pl.BlockSpec((1, hh, s, HEAD_PAD), lambda i, j: (i, 0, 0, 0)),
            pl.BlockSpec((1, hh, DIFF_VD + ONES_ROWS, s), lambda i, j: (i, 0, 0, 0)),
            pl.BlockSpec((1, s, 1), lambda i, j: (i, 0, 0)),
            pl.BlockSpec((1, 1, tq), lambda i, j: (i, 0, j)),
            pl.BlockSpec(memory_space=pltpu.SMEM),
            vec, vec, vec, vec,
            pl.BlockSpec((hh, DIFF_VD, 1), lambda i, j: (0, 0, 0)),
        ],
        out_specs=pl.BlockSpec((1, tq, hh * DIFF_VD), lambda i, j: (i, j, 0)),
        out_shape=jax.ShapeDtypeStruct((b, s, hh * DIFF_VD), BF16),
        scratch_shapes=[pltpu.VMEM((hh * DIFF_VD, tq), F32), pltpu.VMEM((s, tq), F32),
                        pltpu.VMEM((2, s, tq), F32), pltpu.VMEM((2, s, tq), F32)],
        compiler_params=_cparams(2),
        name="diff_attn",
    )(dq1, dq2, dk, dvt, pos_k, pos_q, slopes, lq1, lk1, lq2, lk2, gcol)


def _mix_kernel(alpha, mo_ref, do_ref, ga_ref, gb_ref, x_ref, g1_ref, lng_ref, lnb_ref,
                wbm_ref, wbd_ref, wo_ref, o_ref):
    ya = _dot(mo_ref[0], wbm_ref[...])
    yb = _dot(do_ref[0], wbd_ref[...])
    gated = (ga_ref[0].astype(F32) * ya + gb_ref[0].astype(F32) * yb).astype(BF16)
    mix = _dot(gated, wo_ref[...])
    r = alpha * x_ref[0] + g1_ref[0] * mix
    o_ref[0] = _layernorm(r, lng_ref[...], lnb_ref[...])


def _mix(alpha, mla_o, diff_o, ga, gb, x, g1, lng, lnb, wbm, wbd, wo):
    b, s, d = x.shape
    ts = min(TS_MIX, s)
    row = lambda i, j: (i, j, 0)
    return pl.pallas_call(
        functools.partial(_mix_kernel, alpha),
        grid=(b, s // ts),
        in_specs=[
            pl.BlockSpec((1, ts, mla_o.shape[-1]), row),
            pl.BlockSpec((1, ts, diff_o.shape[-1]), row),
            pl.BlockSpec((1, ts, d), row),
            pl.BlockSpec((1, ts, d), row),
            pl.BlockSpec((1, ts, d), row),
            pl.BlockSpec((1, 1, d), lambda i, j: (i, 0, 0)),
            _const_spec(lng.shape), _const_spec(lnb.shape),
            _const_spec(wbm.shape), _const_spec(wbd.shape), _const_spec(wo.shape),
        ],
        out_specs=pl.BlockSpec((1, ts, d), row),
        out_shape=jax.ShapeDtypeStruct((b, s, d), F32),
        compiler_params=_cparams(2),
        name="mix_ln1",
    )(mla_o, diff_o, ga, gb, x, g1, lng, lnb, wbm, wbd, wo)


def _swiglu(h, w1, w3, w2):
    a = _dot(h, w1)
    bgate = _dot(h, w3)
    u = (a * _sigmoid(a) * bgate).astype(BF16)
    return _dot(u, w2)


def _ffn_kernel(alpha, x_ref, sc_ref, sh_ref, g2_ref, lng_ref, lnb_ref, w1_ref, w3_ref, w2_ref, o_ref):
    x = x_ref[0]
    h = (x * (1.0 + sc_ref[0]) + sh_ref[0]).astype(BF16)
    f = _swiglu(h, w1_ref[...], w3_ref[...], w2_ref[...])
    r = alpha * x + g2_ref[0] * f
    o_ref[0] = _layernorm(r, lng_ref[...], lnb_ref[...])


def _ffn(alpha, x, sc2, sh2, g2, lng, lnb, w1, w3, w2):
    b, s, d = x.shape
    ts = min(TS_FFN, s)
    row = lambda i, j: (i, j, 0)
    bat = lambda i, j: (i, 0, 0)
    return pl.pallas_call(
        functools.partial(_ffn_kernel, alpha),
        grid=(b, s // ts),
        in_specs=[
            pl.BlockSpec((1, ts, d), row),
            pl.BlockSpec((1, 1, d), bat), pl.BlockSpec((1, 1, d), bat), pl.BlockSpec((1, 1, d), bat),
            _const_spec(lng.shape), _const_spec(lnb.shape),
            _const_spec(w1.shape), _const_spec(w3.shape), _const_spec(w2.shape),
        ],
        out_specs=pl.BlockSpec((1, ts, d), row),
        out_shape=jax.ShapeDtypeStruct((b, s, d), F32),
        compiler_params=_cparams(2),
        name="ffn_ln2",
    )(x, sc2, sh2, g2, lng, lnb, w1, w3, w2)


def _router_kernel(x_ref, sc_ref, sh_ref, rw_ref, rb_ref, h_ref, ei_ref, gi_ref):
    h = x_ref[0] * (1.0 + sc_ref[0]) + sh_ref[0]
    _store_row_tiles(h_ref, h)
    logits = _dot(h.astype(BF16), rw_ref[...]) + rb_ref[...]
    lane = lax.broadcasted_iota(jnp.int32, logits.shape, 1)
    lane_f = lane.astype(F32)
    neg = jnp.float32(-jnp.inf)
    lg = jnp.where(lane < N_EXPERTS, logits, neg)
    m1 = jnp.max(lg, axis=1, keepdims=True)
    i1 = jnp.min(jnp.where(lg == m1, lane_f, 128.0), axis=1, keepdims=True)
    lg2 = jnp.where(lane_f == i1, neg, lg)
    m2 = jnp.max(lg2, axis=1, keepdims=True)
    i2 = jnp.min(jnp.where(lg2 == m2, lane_f, 128.0), axis=1, keepdims=True)
    t = jnp.exp(m2 - m1)
    den = 1.0 + t
    ei_ref[0] = jnp.where(lane == 0, i1, jnp.where(lane == 1, i2, 0.0)).astype(jnp.int32)
    gi_ref[0] = jnp.where(lane == 0, 1.0 / den, jnp.where(lane == 1, t / den, 0.0))


def _router(x, sc2, sh2, rw, rb):
    b, s, d = x.shape
    ts = min(TS_ROUTE, s)
    nj = s // ts
    row = lambda i, j: (i, j, 0)
    bat = lambda i, j: (i, 0, 0)
    return pl.pallas_call(
        _router_kernel,
        grid=(b, s // ts),
        in_specs=[
            pl.BlockSpec((1, ts, d), row),
            pl.BlockSpec((1, 1, d), bat), pl.BlockSpec((1, 1, d), bat),
            _const_spec(rw.shape), _const_spec(rb.shape),
        ],
        out_specs=[pl.BlockSpec((ts * ROW_TILE, LANES), lambda i, j: (i * nj + j, 0)),
                   pl.BlockSpec((1, ts, HEAD_PAD), row),
                   pl.BlockSpec((1, ts, HEAD_PAD), row)],
        out_shape=[jax.ShapeDtypeStruct((b * s * ROW_TILE, LANES), F32),
                   jax.ShapeDtypeStruct((b, s, HEAD_PAD), jnp.int32),
                   jax.ShapeDtypeStruct((b, s, HEAD_PAD), F32)],
        compiler_params=_cparams(2),
        name="moe_router",
    )(x, sc2, sh2, rw, rb)


def _experts_kernel(te_ref, nu_ref, idx_hbm, h_hbm, w1_ref, w3_ref, w2_ref, y_hbm,
                    idx_smem, xbuf, ybuf, sem_idx, sem_rows, sem_out):
    i = pl.program_id(0)
    n_used = nu_ref[0]
    n_tiles = pl.num_programs(0)
    tm = xbuf.shape[1] // ROW_TILE
    slot = lax.rem(i, 2)
    n_idx = idx_smem.shape[0]

    def index_copy(tile, step):
        return pltpu.make_async_copy(idx_hbm.at[tile], idx_smem.at[pl.ds(lax.rem(step, n_idx), 1)], sem_idx)

    def row_tile(r):
        start = r * ROW_TILE
        return pl.ds(start if isinstance(r, int) else pl.multiple_of(start, ROW_TILE), ROW_TILE)

    def gather_copy(step, r, buf=None):
        row = lax.rem(step, n_idx)
        buf = lax.rem(step, 2) if buf is None else buf
        return pltpu.make_async_copy(h_hbm.at[row_tile(idx_smem[row, r])],
                                     xbuf.at[buf, row_tile(r)], sem_rows.at[buf])

    def scatter_copy(step, r, buf=None):
        row = lax.rem(step, n_idx)
        buf = lax.rem(step, 2) if buf is None else buf
        return pltpu.make_async_copy(ybuf.at[buf, row_tile(r)],
                                     y_hbm.at[row_tile(idx_smem[row, tm + r])], sem_out.at[buf])

    def wait_gather(buf):
        pltpu.make_async_copy(h_hbm.at[pl.ds(0, tm * ROW_TILE)], xbuf.at[buf], sem_rows.at[buf]).wait()

    def wait_scatter(buf):
        pltpu.make_async_copy(ybuf.at[buf], y_hbm.at[pl.ds(0, tm * ROW_TILE)], sem_out.at[buf]).wait()

    @pl.when(i == 0)
    def _():
        for tile, step in ((0, 0), (jnp.minimum(1, n_tiles - 1), 1), (n_tiles, n_idx - 1)):
            index_copy(tile, step).start()
            index_copy(tile, step).wait()
        lax.fori_loop(0, tm, lambda r, c: (gather_copy(0, r).start(), c)[1], 0)
        ybuf[1] = jnp.zeros(ybuf.shape[1:], F32)

    def step(slot):
        wait_gather(slot)
        for r in range(tm):
            gather_copy(i + 1, r, 1 - slot).start()
        nxt2 = jnp.minimum(i + 2, n_tiles - 1)
        index_copy(nxt2, i + 2).start()
        for r in range(tm):
            scatter_copy(i + n_idx - 1, r, 1 - slot).start()
        x = _load_row_tiles(xbuf, tm, (slot,))
        _store_row_tiles(ybuf, _swiglu(x.astype(BF16), w1_ref[0], w3_ref[0], w2_ref[0]), (slot,))
        wait_scatter(1 - slot)
        index_copy(nxt2, i + 2).wait()

    for s in (0, 1):
        pl.when((i < n_used) & (slot == s))(functools.partial(step, s))

    @pl.when(i + 1 == n_used)
    def _():
        lax.fori_loop(0, tm, lambda r, c: (scatter_copy(i, r).start(), c)[1], 0)
        wait_scatter(slot)
        wait_gather(1 - slot)


def _experts(tile_expert, n_used, idx, h2, w1, w3, w2):
    n_tiles, tm2 = idx.shape[0] - 1, idx.shape[-1]
    tm = tm2 // 2
    t = h2.shape[0] // ROW_TILE
    _, d, f = w1.shape
    grid_spec = pltpu.PrefetchScalarGridSpec(
        num_scalar_prefetch=2,
        grid=(n_tiles,),
        in_specs=[
            pl.BlockSpec(memory_space=pl.ANY),
            pl.BlockSpec(memory_space=pl.ANY),
            pl.BlockSpec((1, d, f), lambda i, te, nu: (te[i], 0, 0)),
            pl.BlockSpec((1, d, f), lambda i, te, nu: (te[i], 0, 0)),
            pl.BlockSpec((1, f, d), lambda i, te, nu: (te[i], 0, 0)),
        ],
        out_specs=pl.BlockSpec(memory_space=pl.ANY),
        scratch_shapes=[
            pltpu.SMEM((4, 2 * tm), jnp.int32),
            pltpu.VMEM((2, tm * ROW_TILE, LANES), F32),
            pltpu.VMEM((2, tm * ROW_TILE, LANES), F32),
            pltpu.SemaphoreType.DMA(()),
            pltpu.SemaphoreType.DMA((2,)),
            pltpu.SemaphoreType.DMA((2,)),
        ],
    )
    return pl.pallas_call(
        _experts_kernel,
        grid_spec=grid_spec,
        out_shape=jax.ShapeDtypeStruct(((TOP_K * t + tm) * ROW_TILE, LANES), F32),
        compiler_params=_cparams(1),
        name="moe_experts",
    )(tile_expert, n_used, idx, h2, w1, w3, w2)


def _combine_kernel(alpha, y0_ref, y1_ref, gi_ref, x_ref, g2_ref, lng_ref, lnb_ref, o_ref):
    gi = gi_ref[0]
    n = gi.shape[0]
    f = gi[:, 0:1] * _load_row_tiles(y0_ref, n) + gi[:, 1:2] * _load_row_tiles(y1_ref, n)
    r = alpha * x_ref[0] + g2_ref[0] * f
    o_ref[0] = _layernorm(r, lng_ref[...], lnb_ref[...])


def _combine(alpha, y, gi, x, g2, lng, lnb):
    b, s, d = x.shape
    ts = min(TS_COMB, s)
    nj = s // ts
    row = lambda i, j: (i, j, 0)
    return pl.pallas_call(
        functools.partial(_combine_kernel, alpha),
        grid=(b, nj),
        in_specs=[
            pl.BlockSpec((ts * ROW_TILE, LANES), lambda i, j: (i * nj + j, 0)),
            pl.BlockSpec((ts * ROW_TILE, LANES), lambda i, j: (b * nj + i * nj + j, 0)),
            pl.BlockSpec((1, ts, HEAD_PAD), row),
            pl.BlockSpec((1, ts, d), row),
            pl.BlockSpec((1, 1, d), lambda i, j: (i, 0, 0)),
            _const_spec(lng.shape), _const_spec(lnb.shape),
        ],
        out_specs=pl.BlockSpec((1, ts, d), row),
        out_shape=jax.ShapeDtypeStruct((b, s, d), F32),
        compiler_params=_cparams(2),
        name="moe_combine_ln2",
    )(y, y, gi, x, g2, lng, lnb)


def _moe_plan(expert_idx, tm):
    t = expert_idx.shape[0]
    n_slots = t * TOP_K
    flat = expert_idx.reshape(n_slots)
    counts = jnp.sum((flat[:, None] == jnp.arange(N_EXPERTS, dtype=jnp.int32)[None, :]).astype(jnp.int32), axis=0)
    padded = (counts + tm - 1) // tm * tm
    pends = jnp.cumsum(padded)
    pstarts = pends - padded
    starts = jnp.cumsum(counts) - counts
    n_tiles = (n_slots + N_EXPERTS * tm) // tm
    tile_expert = jnp.minimum(
        jnp.searchsorted(pends, jnp.arange(n_tiles, dtype=jnp.int32) * tm, side="right"),
        N_EXPERTS - 1).astype(jnp.int32)
    n_used = (pends[-1] // tm).astype(jnp.int32).reshape(1)
    order = jnp.argsort(flat, stable=True).astype(jnp.int32)
    n_rows = n_tiles * tm
    rows = jnp.arange(n_rows, dtype=jnp.int32)
    order_pad = jnp.concatenate([order, jnp.zeros((n_rows - n_slots,), jnp.int32)])
    slot_of_row = jnp.full((n_rows,), -1, jnp.int32)
    for e in range(N_EXPERTS):
        shifted = jnp.roll(order_pad, pstarts[e] - starts[e])
        slot_of_row = jnp.where((rows >= pstarts[e]) & (rows < pstarts[e] + counts[e]), shifted, slot_of_row)
    valid = slot_of_row >= 0
    tok = slot_of_row // TOP_K
    src = jnp.where(valid, tok, 0)
    dst = jnp.where(valid, (slot_of_row % TOP_K) * t + tok, n_slots + rows % tm)
    idx = jnp.concatenate([src.reshape(n_tiles, 1, tm), dst.reshape(n_tiles, 1, tm)], axis=-1)
    dummy = jnp.concatenate([jnp.zeros((1, 1, tm), jnp.int32),
                             (n_slots + jnp.arange(tm, dtype=jnp.int32)).reshape(1, 1, tm)], axis=-1)
    return tile_expert, n_used, jnp.concatenate([idx.astype(jnp.int32), dummy], axis=0)


def _moe(alpha, x, sc2, sh2, g2, lng, lnb, rw, rb, w1, w3, w2):
    b, s, d = x.shape
    t = b * s
    h2, ei, gi = _router(x, sc2, sh2, rw, rb)
    tm = min(TM_MOE, t)
    tile_expert, n_used, idx = _moe_plan(ei.reshape(t, HEAD_PAD)[:, :TOP_K], tm)
    y = _experts(tile_expert, n_used, idx, h2, w1, w3, w2)
    return _combine(alpha, y, gi, x, g2, lng, lnb)


def _prep_layer_weights(w_in, w_q_up, w_kv_up, q_norm_g, kv_norm_g):
    n_l, d, _ = w_in.shape
    z96 = jnp.zeros((n_l, d, HEAD_PAD - MLA_ROPE), w_in.dtype)
    o_kr = MLA_Q_LORA + MLA_KV_LORA
    half = MLA_ROPE // 2
    kr_a = w_in[:, :, o_kr:o_kr + half]
    kr_b = w_in[:, :, o_kr + half:o_kr + MLA_ROPE]
    wlat = jnp.concatenate([w_in[:, :, :o_kr + MLA_ROPE], z96, kr_b, kr_a, z96], axis=-1)
    o = o_kr + MLA_ROPE
    nd = DIFF_HEADS * 2 * DIFF_HD
    wdq = w_in[:, :, o:o + nd]
    wdk = w_in[:, :, o + nd:o + 2 * nd]
    wdv = w_in[:, :, o + 2 * nd:o + 3 * nd]
    wga = w_in[:, :, o + 3 * nd:o + 3 * nd + d]
    wgb = w_in[:, :, o + 3 * nd + d:o + 3 * nd + 2 * d]

    hq = MLA_NOPE + MLA_ROPE
    wq4 = w_q_up.reshape(n_l, MLA_Q_LORA, MLA_HEADS, hq)
    zq = jnp.zeros((n_l, MLA_Q_LORA, MLA_HEADS, HEAD_PAD - hq), w_q_up.dtype)
    wq = jnp.concatenate([wq4, zq], axis=-1).reshape(n_l, MLA_Q_LORA, MLA_HEADS * HEAD_PAD)
    wqs = jnp.concatenate([jnp.zeros_like(wq4[..., :MLA_NOPE]), wq4[..., MLA_NOPE + half:],
                           wq4[..., MLA_NOPE:MLA_NOPE + half], zq], axis=-1)
    wqs = wqs.reshape(n_l, MLA_Q_LORA, MLA_HEADS * HEAD_PAD)

    wkv4 = w_kv_up.reshape(n_l, MLA_KV_LORA, MLA_HEADS, MLA_NOPE + MLA_V)
    wkn = jnp.concatenate([wkv4[..., :MLA_NOPE], jnp.zeros_like(wkv4[..., :HEAD_PAD - MLA_NOPE])], axis=-1)
    wkn = wkn.reshape(n_l, MLA_KV_LORA, MLA_HEADS * HEAD_PAD)
    wv = wkv4[..., MLA_NOPE:].reshape(n_l, MLA_KV_LORA, MLA_HEADS * MLA_V)

    rr = jnp.arange(HEAD_PAD)[:, None]
    cc = jnp.arange(MLA_HEADS * HEAD_PAD)[None, :]
    esel = ((rr < MLA_ROPE) & (cc % HEAD_PAD == MLA_NOPE + rr)).astype(BF16)

    cast = lambda a: a.astype(BF16)
    return dict(wlat=cast(wlat), wdq=cast(wdq), wdk=cast(wdk), wdv=cast(wdv), wga=cast(wga), wgb=cast(wgb),
                wq=cast(wq), wqs=cast(wqs), wkn=cast(wkn), wv=cast(wv), esel=esel,
                qg=q_norm_g.reshape(n_l, 1, MLA_Q_LORA), kvg=kv_norm_g.reshape(n_l, 1, MLA_KV_LORA))


def _rope_tables(positions):
    inv_freq = ROPE_BASE ** (-jnp.arange(0, MLA_ROPE, 2, dtype=F32) / MLA_ROPE)
    ang = positions.astype(F32)[..., None] * inv_freq
    cos, sin = jnp.cos(ang), jnp.sin(ang)
    ones = jnp.ones(positions.shape + (MLA_NOPE,), F32)
    tail = HEAD_PAD - MLA_NOPE - MLA_ROPE
    ct = jnp.concatenate([ones, cos, cos, ones[..., :tail]], axis=-1)
    st = jnp.concatenate([0.0 * ones, -sin, sin, 0.0 * ones[..., :tail]], axis=-1)
    return ct, st


def kernel(x, c, positions, w_ada, b_ada, w_in, q_norm_g, w_q_up, kv_norm_g, w_kv_up, lambda_q1, lambda_k1, lambda_q2, lambda_k2, diff_norm_g, w_br_mla, w_br_diff, w_out, ln1_g, ln1_b, ln2_g, ln2_b, ffn_w1, ffn_w3, ffn_w2, router_w, router_b, moe_w1, moe_w3, moe_w2):
    b, s, d = x.shape
    depth = w_in.shape[0]
    alpha = (2.0 * depth) ** 0.25

    mod = _ada_mod(c, w_ada, b_ada)
    ct, st = _rope_tables(positions)
    posf = positions.astype(F32)
    pos_k = posf.reshape(b, s, 1)
    pos_q = posf.reshape(b, 1, s)
    slopes = (2.0 ** (-8.0 * jnp.arange(1, DIFF_HEADS + 1, dtype=F32) / DIFF_HEADS)) * LOG2E

    lw = _prep_layer_weights(w_in, w_q_up, w_kv_up, q_norm_g, kv_norm_g)
    wbm, wbd, wo = w_br_mla.astype(BF16), w_br_diff.astype(BF16), w_out.astype(BF16)
    fw1, fw3, fw2 = ffn_w1.astype(BF16), ffn_w3.astype(BF16), ffn_w2.astype(BF16)
    mw1, mw3, mw2 = moe_w1.astype(BF16), moe_w3.astype(BF16), moe_w2.astype(BF16)
    rw = jnp.pad(router_w, ((0, 0), (0, 0), (0, HEAD_PAD - N_EXPERTS))).astype(BF16)
    rb = jnp.pad(router_b, ((0, 0), (0, HEAD_PAD - N_EXPERTS))).reshape(-1, 1, HEAD_PAD)
    gcol = diff_norm_g.reshape(depth, DIFF_HEADS, DIFF_VD, 1)
    vec = lambda a, l: a[l].reshape(1, -1)

    for l in range(depth):
        sh1, sc1, g1, sh2, sc2, g2 = [m.reshape(b, 1, d) for m in jnp.split(mod[l], 6, axis=-1)]
        w_l = {k: v[l] if k != "esel" else v for k, v in lw.items()}
        qm, km, vt, dq1, dq2, dk, dvt, ga, gb = _inproj(x, sc1, sh1, ct, st, w_l)
        mla_o = _mla_attention(qm, km, vt)
        lam_init = 0.8 - 0.6 * math.exp(-0.3 * l)
        diff_o = _diff_attention(dq1, dq2, dk, dvt, pos_k, pos_q, slopes,
                                 vec(lambda_q1, l), vec(lambda_k1, l), vec(lambda_q2, l), vec(lambda_k2, l),
                                 gcol[l], lam_init)
        x = _mix(alpha, mla_o, diff_o, ga, gb, x, g1, vec(ln1_g, l), vec(ln1_b, l), wbm[l], wbd[l], wo[l])
        if l % 2 == 0:
            x = _ffn(alpha, x, sc2, sh2, g2, vec(ln2_g, l), vec(ln2_b, l), fw1[l // 2], fw3[l // 2], fw2[l // 2])
        else:
            x = _moe(alpha, x, sc2, sh2, g2, vec(ln2_g, l), vec(ln2_b, l),
                     rw[l // 2], rb[l // 2], mw1[l // 2], mw3[l // 2], mw2[l // 2])
    return x
```

```python
import functools
import math

import jax
import jax.numpy as jnp
from jax import lax
from jax.experimental import pallas as pl
from jax.experimental.pallas import tpu as pltpu

BF16 = jnp.bfloat16
F32 = jnp.float32

D_MODEL = 1024
MLA_HEADS = 8
MLA_NOPE = 64
MLA_ROPE = 32
MLA_V = 64
MLA_Q_LORA = 384
MLA_KV_LORA = 256
ROPE_BASE = 10000.0
DIFF_HEADS = 8
DIFF_HD = 64
DIFF_VD = 2 * DIFF_HD
D_FF = 2816
N_EXPERTS = 8
TOP_K = 2
LN_EPS = 1e-5
RMS_EPS = 1e-6

HEAD_PAD = 128
ONES_ROWS = 16
LOG2E = 1.4426950408889634
MLA_QSCALE = (MLA_NOPE + MLA_ROPE) ** -0.5 * LOG2E
DIFF_QSCALE = DIFF_HD ** -0.5 * LOG2E

VMEM_LIMIT = 56 * 1024 * 1024

TS_PROJ = 256
TQ_MLA = 512
TQ_DIFF = 256
TS_MIX = 1024
TS_FFN = 512
TM_MOE = 256
TS_ROUTE = 512
TS_COMB = 512


def _cparams(n_axes):
    return pltpu.CompilerParams(dimension_semantics=("arbitrary",) * n_axes,
                                vmem_limit_bytes=VMEM_LIMIT)


def _const_spec(shape):
    nd = len(shape)
    return pl.BlockSpec(shape, lambda *_: (0,) * nd, pipeline_mode=pl.Buffered(1))


def _dot(a, b):
    return jnp.dot(a, b, preferred_element_type=F32)


def _dot_nt(a, b):
    return lax.dot_general(a, b, (((1,), (1,)), ((), ())), preferred_element_type=F32)


def _sigmoid(v):
    return 1.0 / (1.0 + jnp.exp(-v))


def _layernorm(r, g, b):
    mu = jnp.mean(r, axis=-1, keepdims=True)
    d = r - mu
    var = jnp.mean(d * d, axis=-1, keepdims=True)
    return d * lax.rsqrt(var + LN_EPS) * g + b


def _rms_rows(v, g):
    ms = jnp.mean(v * v, axis=-1, keepdims=True)
    return v * lax.rsqrt(ms + RMS_EPS) * g


LANES = 128
ROW_TILE = D_MODEL // LANES


def _store_row_tiles(ref, v, lead=()):
    n = v.shape[0]
    for c in range(ROW_TILE):
        ref[lead + (pl.ds(c, n, stride=ROW_TILE), slice(None))] = v[:, c * LANES:(c + 1) * LANES]


def _load_row_tiles(ref, n, lead=()):
    return jnp.concatenate(
        [ref[lead + (pl.ds(c, n, stride=ROW_TILE), slice(None))] for c in range(ROW_TILE)], axis=1)


def _ada_kernel(c_ref, w_ref, b_ref, o_ref):
    c = c_ref[...]
    cond = c * _sigmoid(c)
    o_ref[0] = _dot(cond.astype(BF16), w_ref[0].astype(BF16)) + b_ref[0]


def _ada_mod(c, w_ada, b_ada):
    n_l, d, n6 = w_ada.shape
    b = c.shape[0]
    tn = 1536
    return pl.pallas_call(
        _ada_kernel,
        grid=(n_l, n6 // tn),
        in_specs=[
            pl.BlockSpec((b, d), lambda l, j: (0, 0)),
            pl.BlockSpec((1, d, tn), lambda l, j: (l, 0, j)),
            pl.BlockSpec((1, 1, tn), lambda l, j: (l, 0, j)),
        ],
        out_specs=pl.BlockSpec((1, b, tn), lambda l, j: (l, 0, j)),
        out_shape=jax.ShapeDtypeStruct((n_l, b, n6), F32),
        compiler_params=_cparams(2),
        name="ada_mod",
    )(c, w_ada, b_ada.reshape(n_l, 1, n6))


def _inproj_kernel(x_ref, sc_ref, sh_ref, ct_ref, st_ref,
                   wlat_ref, wdq_ref, wdk_ref, wdv_ref, wga_ref, wgb_ref,
                   qg_ref, kvg_ref, wq_ref, wkn_ref, wv_ref, esel_ref,
                   qm_ref, km_ref, vt_ref, dq1_ref, dq2_ref, dk_ref, dvt_ref, ga_ref, gb_ref):
    x = x_ref[0]
    h = (x * (1.0 + sc_ref[0]) + sh_ref[0]).astype(BF16)
    half = MLA_ROPE // 2
    ct = ct_ref[0]
    st = st_ref[0]
    lane = lax.broadcasted_iota(jnp.int32, st.shape, 1)
    sa = jnp.where(lane < MLA_NOPE + half, st, 0.0)
    sb = st - sa

    def rotate(v, c, s_first, s_second):
        n = v.shape[1]
        return v * c + pltpu.roll(v, n - half, axis=1) * s_first + pltpu.roll(v, half, axis=1) * s_second

    lat = _dot(h, wlat_ref[...])
    q_lat = lat[:, :MLA_Q_LORA]
    kv_lat = lat[:, MLA_Q_LORA:MLA_Q_LORA + MLA_KV_LORA]
    kr = lat[:, MLA_Q_LORA + MLA_KV_LORA:]

    qn = _rms_rows(q_lat, qg_ref[...]).astype(BF16)
    q = _dot(qn, wq_ref[...])
    tile8 = lambda t: jnp.concatenate([t] * MLA_HEADS, axis=1)
    qr = (rotate(q, tile8(ct), tile8(sa), tile8(sb)) * MLA_QSCALE).astype(BF16)
    for hd in range(MLA_HEADS):
        qm_ref[0, hd] = qr[:, hd * HEAD_PAD:(hd + 1) * HEAD_PAD]

    kvn = _rms_rows(kv_lat, kvg_ref[...]).astype(BF16)
    kn = _dot(kvn, wkn_ref[...])
    to_front = lambda t: pltpu.roll(t, HEAD_PAD - MLA_NOPE, axis=1)
    kro = rotate(kr, to_front(ct), to_front(sa), to_front(sb)).astype(BF16)
    kcat = (kn + _dot(kro, esel_ref[...])).astype(BF16)
    for hd in range(MLA_HEADS):
        km_ref[0, hd] = kcat[:, hd * HEAD_PAD:(hd + 1) * HEAD_PAD]
    v = _dot(kvn, wv_ref[...])
    vt = v.T.astype(BF16)
    ones = jnp.ones((ONES_ROWS, vt.shape[1]), BF16)
    for hd in range(MLA_HEADS):
        vt_ref[0, hd, :MLA_V, :] = vt[hd * MLA_V:(hd + 1) * MLA_V, :]
        vt_ref[0, hd, MLA_V:, :] = ones

    dq = _dot(h, wdq_ref[...]) * DIFF_QSCALE
    lane = lax.broadcasted_iota(jnp.int32, dq.shape, 1)
    first = (lane % HEAD_PAD) < DIFF_HD
    dq1 = jnp.where(first, dq, 0.0).astype(BF16)
    dq2 = jnp.where(first, 0.0, dq).astype(BF16)
    dk = _dot(h, wdk_ref[...]).astype(BF16)
    dvt = _dot(h, wdv_ref[...]).T.astype(BF16)
    for hd in range(DIFF_HEADS):
        sl = slice(hd * HEAD_PAD, (hd + 1) * HEAD_PAD)
        dq1_ref[0, hd] = dq1[:, sl]
        dq2_ref[0, hd] = dq2[:, sl]
        dk_ref[0, hd] = dk[:, sl]
        dvt_ref[0, hd, :DIFF_VD, :] = dvt[sl, :]
        dvt_ref[0, hd, DIFF_VD:, :] = ones

    ga_ref[0] = _sigmoid(_dot(h, wga_ref[...])).astype(BF16)
    gb_ref[0] = _sigmoid(_dot(h, wgb_ref[...])).astype(BF16)


def _inproj(x, sc1, sh1, ct, st, w):
    b, s, d = x.shape
    ts = min(TS_PROJ, s)
    hh = MLA_HEADS
    row = lambda i, j: (i, j, 0)
    bat = lambda i, j: (i, 0, 0)
    head_rows = pl.BlockSpec((1, hh, ts, HEAD_PAD), lambda i, j: (i, 0, j, 0))
    weights = [w["wlat"], w["wdq"], w["wdk"], w["wdv"], w["wga"], w["wgb"],
               w["qg"], w["kvg"], w["wq"], w["wkn"], w["wv"], w["esel"]]
    head_shape = jax.ShapeDtypeStruct((b, hh, s, HEAD_PAD), BF16)
    return pl.pallas_call(
        _inproj_kernel,
        grid=(b, s // ts),
        in_specs=[
            pl.BlockSpec((1, ts, d), row),
            pl.BlockSpec((1, 1, d), bat),
            pl.BlockSpec((1, 1, d), bat),
            pl.BlockSpec((1, ts, HEAD_PAD), row),
            pl.BlockSpec((1, ts, HEAD_PAD), row),
        ] + [_const_spec(a.shape) for a in weights],
        out_specs=[
            head_rows, head_rows,
            pl.BlockSpec((1, hh, MLA_V + ONES_ROWS, ts), lambda i, j: (i, 0, 0, j)),
            head_rows, head_rows, head_rows,
            pl.BlockSpec((1, hh, DIFF_VD + ONES_ROWS, ts), lambda i, j: (i, 0, 0, j)),
            pl.BlockSpec((1, ts, d), row),
            pl.BlockSpec((1, ts, d), row),
        ],
        out_shape=[
            head_shape, head_shape,
            jax.ShapeDtypeStruct((b, hh, MLA_V + ONES_ROWS, s), BF16),
            head_shape, head_shape, head_shape,
            jax.ShapeDtypeStruct((b, hh, DIFF_VD + ONES_ROWS, s), BF16),
            jax.ShapeDtypeStruct((b, s, d), BF16),
            jax.ShapeDtypeStruct((b, s, d), BF16),
        ],
        compiler_params=_cparams(2),
        name="inproj",
    )(x, sc1, sh1, ct, st, *weights)


def _mla_kernel(q_ref, k_ref, vt_ref, o_ref, acc_ref, za_ref, zb_ref):
    def scores(hd, z_ref):
        s_t = _dot_nt(k_ref[0, hd], q_ref[0, hd])
        z_ref[...] = s_t
        return jnp.max(s_t, axis=0, keepdims=True)

    def values(hd, z_ref, m):
        p = jnp.exp2(z_ref[...] - m).astype(BF16)
        o_t = _dot(vt_ref[0, hd], p)
        acc_ref[pl.ds(pl.multiple_of(hd * MLA_V, MLA_V), MLA_V), :] = o_t[:MLA_V] / o_t[MLA_V:MLA_V + 1]

    def body(i, ma):
        h0 = 2 * i
        mb = scores(h0 + 1, zb_ref)
        values(h0, za_ref, ma)
        ma = scores(h0 + 2, za_ref)
        values(h0 + 1, zb_ref, mb)
        return ma

    last = MLA_HEADS - 1
    ma = lax.fori_loop(0, MLA_HEADS // 2 - 1, body, scores(0, za_ref))
    mb = scores(last, zb_ref)
    values(last - 1, za_ref, ma)
    values(last, zb_ref, mb)
    o_ref[0] = acc_ref[...].T.astype(BF16)


def _mla_attention(qm, km, vt):
    b, hh, s, _ = qm.shape
    tq = min(TQ_MLA, s)
    return pl.pallas_call(
        _mla_kernel,
        grid=(b, s // tq),
        in_specs=[
            pl.BlockSpec((1, hh, tq, HEAD_PAD), lambda i, j: (i, 0, j, 0)),
            pl.BlockSpec((1, hh, s, HEAD_PAD), lambda i, j: (i, 0, 0, 0)),
            pl.BlockSpec((1, hh, MLA_V + ONES_ROWS, s), lambda i, j: (i, 0, 0, 0)),
        ],
        out_specs=pl.BlockSpec((1, tq, hh * MLA_V), lambda i, j: (i, j, 0)),
        out_shape=jax.ShapeDtypeStruct((b, s, hh * MLA_V), BF16),
        scratch_shapes=[pltpu.VMEM((hh * MLA_V, tq), F32), pltpu.VMEM((s, tq), F32), pltpu.VMEM((s, tq), F32)],
        compiler_params=_cparams(2),
        name="mla_attn",
    )(qm, km, vt)


def _diff_kernel(lam_init, q1_ref, q2_ref, k_ref, vt_ref, pk_ref, pq_ref, slope_ref,
                 lq1_ref, lk1_ref, lq2_ref, lk2_ref, g_ref, o_ref, acc_ref, dist_ref, za_ref, zb_ref):
    dist_ref[...] = jnp.abs(pk_ref[0] - pq_ref[0])
    lam = (jnp.exp(jnp.sum(lq1_ref[...] * lk1_ref[...], axis=1, keepdims=True))
           - jnp.exp(jnp.sum(lq2_ref[...] * lk2_ref[...], axis=1, keepdims=True))
           + lam_init)

    def scores(hd, z_ref):
        k = k_ref[0, hd]
        bias = slope_ref[hd] * dist_ref[...]
        z1 = _dot_nt(k, q1_ref[0, hd]) - bias
        z2 = _dot_nt(k, q2_ref[0, hd]) - bias
        z_ref[0] = z1
        z_ref[1] = z2
        return jnp.max(z1, axis=0, keepdims=True), jnp.max(z2, axis=0, keepdims=True)

    def values(hd, z_ref, m1, m2):
        e1 = jnp.exp2(z_ref[0] - m1).astype(BF16)
        e2 = jnp.exp2(z_ref[1] - m2).astype(BF16)
        o1 = _dot(vt_ref[0, hd], e1)
        o2 = _dot(vt_ref[0, hd], e2)
        r1 = 1.0 / o1[DIFF_VD:DIFF_VD + 1]
        r2 = lam / o2[DIFF_VD:DIFF_VD + 1]
        o_t = o1[:DIFF_VD] * r1 - o2[:DIFF_VD] * r2
        ms = jnp.mean(o_t * o_t, axis=0, keepdims=True)
        o_t = o_t * lax.rsqrt(ms + RMS_EPS) * g_ref[hd] * (1.0 - lam_init)
        acc_ref[pl.ds(pl.multiple_of(hd * DIFF_VD, DIFF_VD), DIFF_VD), :] = o_t

    def body(i, ma):
        h0 = 2 * i
        mb = scores(h0 + 1, zb_ref)
        values(h0, za_ref, *ma)
        ma = scores(h0 + 2, za_ref)
        values(h0 + 1, zb_ref, *mb)
        return ma

    last = DIFF_HEADS - 1
    ma = lax.fori_loop(0, DIFF_HEADS // 2 - 1, body, scores(0, za_ref))
    mb = scores(last, zb_ref)
    values(last - 1, za_ref, *ma)
    values(last, zb_ref, *mb)
    o_ref[0] = acc_ref[...].T.astype(BF16)


def _diff_attention(dq1, dq2, dk, dvt, pos_k, pos_q, slopes, lq1, lk1, lq2, lk2, gcol, lam_init):
    b, hh, s, _ = dk.shape
    tq = min(TQ_DIFF, s)
    head_q = pl.BlockSpec((1, hh, tq, HEAD_PAD), lambda i, j: (i, 0, j, 0))
    vec = pl.BlockSpec((1, DIFF_HD), lambda i, j: (0, 0))
    return pl.pallas_call(
        functools.partial(_diff_kernel, lam_init),
        grid=(b, s // tq),
        in_specs=[
            head_q, head_q,
            pl.BlockSpec((1, hh, s, HEAD_PAD), lambda i, j: (i, 0, 0, 0)),
            pl.BlockSpec((1, hh, DIFF_VD + ONES_ROWS, s), lambda i, j: (i, 0, 0, 0)),
            pl.BlockSpec((1, s, 1), lambda i, j: (i, 0, 0)),
            pl.BlockSpec((1, 1, tq), lambda i, j: (i, 0, j)),
            pl.BlockSpec(memory_space=pltpu.SMEM),
            vec, vec, vec, vec,
            pl.BlockSpec((hh, DIFF_VD, 1), lambda i, j: (0, 0, 0)),
        ],
        out_specs=pl.BlockSpec((1, tq, hh * DIFF_VD), lambda i, j: (i, j, 0)),
        out_shape=jax.ShapeDtypeStruct((b, s, hh * DIFF_VD), BF16),
        scratch_shapes=[pltpu.VMEM((hh * DIFF_VD, tq), F32), pltpu.VMEM((s, tq), F32),
                        pltpu.VMEM((2, s, tq), F32), pltpu.VMEM((2, s, tq), F32)],
        compiler_params=_cparams(2),
        name="diff_attn",
    )(dq1, dq2, dk, dvt, pos_k, pos_q, slopes, lq1, lk1, lq2, lk2, gcol)


def _mix_kernel(alpha, mo_ref, do_ref, ga_ref, gb_ref, x_ref, g1_ref, lng_ref, lnb_ref,
                wbm_ref, wbd_ref, wo_ref, o_ref):
    ya = _dot(mo_ref[0], wbm_ref[...])
    yb = _dot(do_ref[0], wbd_ref[...])
    gated = (ga_ref[0].astype(F32) * ya + gb_ref[0].astype(F32) * yb).astype(BF16)
    mix = _dot(gated, wo_ref[...])
    r = alpha * x_ref[0] + g1_ref[0] * mix
    o_ref[0] = _layernorm(r, lng_ref[...], lnb_ref[...])


def _mix(alpha, mla_o, diff_o, ga, gb, x, g1, lng, lnb, wbm, wbd, wo):
    b, s, d = x.shape
    ts = min(TS_MIX, s)
    row = lambda i, j: (i, j, 0)
    return pl.pallas_call(
        functools.partial(_mix_kernel, alpha),
        grid=(b, s // ts),
        in_specs=[
            pl.BlockSpec((1, ts, mla_o.shape[-1]), row),
            pl.BlockSpec((1, ts, diff_o.shape[-1]), row),
            pl.BlockSpec((1, ts, d), row),
            pl.BlockSpec((1, ts, d), row),
            pl.BlockSpec((1, ts, d), row),
            pl.BlockSpec((1, 1, d), lambda i, j: (i, 0, 0)),
            _const_spec(lng.shape), _const_spec(lnb.shape),
            _const_spec(wbm.shape), _const_spec(wbd.shape), _const_spec(wo.shape),
        ],
        out_specs=pl.BlockSpec((1, ts, d), row),
        out_shape=jax.ShapeDtypeStruct((b, s, d), F32),
        compiler_params=_cparams(2),
        name="mix_ln1",
    )(mla_o, diff_o, ga, gb, x, g1, lng, lnb, wbm, wbd, wo)


def _swiglu(h, w1, w3, w2):
    a = _dot(h, w1)
    bgate = _dot(h, w3)
    u = (a * _sigmoid(a) * bgate).astype(BF16)
    return _dot(u, w2)


def _ffn_kernel(alpha, x_ref, sc_ref, sh_ref, g2_ref, lng_ref, lnb_ref, w1_ref, w3_ref, w2_ref, o_ref):
    x = x_ref[0]
    h = (x * (1.0 + sc_ref[0]) + sh_ref[0]).astype(BF16)
    f = _swiglu(h, w1_ref[...], w3_ref[...], w2_ref[...])
    r = alpha * x + g2_ref[0] * f
    o_ref[0] = _layernorm(r, lng_ref[...], lnb_ref[...])


def _ffn(alpha, x, sc2, sh2, g2, lng, lnb, w1, w3, w2):
    b, s, d = x.shape
    ts = min(TS_FFN, s)
    row = lambda i, j: (i, j, 0)
    bat = lambda i, j: (i, 0, 0)
    return pl.pallas_call(
        functools.partial(_ffn_kernel, alpha),
        grid=(b, s // ts),
        in_specs=[
            pl.BlockSpec((1, ts, d), row),
            pl.BlockSpec((1, 1, d), bat), pl.BlockSpec((1, 1, d), bat), pl.BlockSpec((1, 1, d), bat),
            _const_spec(lng.shape), _const_spec(lnb.shape),
            _const_spec(w1.shape), _const_spec(w3.shape), _const_spec(w2.shape),
        ],
        out_specs=pl.BlockSpec((1, ts, d), row),
        out_shape=jax.ShapeDtypeStruct((b, s, d), F32),
        compiler_params=_cparams(2),
        name="ffn_ln2",
    )(x, sc2, sh2, g2, lng, lnb, w1, w3, w2)


def _router_kernel(x_ref, sc_ref, sh_ref, rw_ref, rb_ref, h_ref, ei_ref, gi_ref):
    h = x_ref[0] * (1.0 + sc_ref[0]) + sh_ref[0]
    _store_row_tiles(h_ref, h)
    logits = _dot(h.astype(BF16), rw_ref[...]) + rb_ref[...]
    lane = lax.broadcasted_iota(jnp.int32, logits.shape, 1)
    lane_f = lane.astype(F32)
    neg = jnp.float32(-jnp.inf)
    lg = jnp.where(lane < N_EXPERTS, logits, neg)
    m1 = jnp.max(lg, axis=1, keepdims=True)
    i1 = jnp.min(jnp.where(lg == m1, lane_f, 128.0), axis=1, keepdims=True)
    lg2 = jnp.where(lane_f == i1, neg, lg)
    m2 = jnp.max(lg2, axis=1, keepdims=True)
    i2 = jnp.min(jnp.where(lg2 == m2, lane_f, 128.0), axis=1, keepdims=True)
    t = jnp.exp(m2 - m1)
    den = 1.0 + t
    ei_ref[0] = jnp.where(lane == 0, i1, jnp.where(lane == 1, i2, 0.0)).astype(jnp.int32)
    gi_ref[0] = jnp.where(lane == 0, 1.0 / den, jnp.where(lane == 1, t / den, 0.0))


def _router(x, sc2, sh2, rw, rb):
    b, s, d = x.shape
    ts = min(TS_ROUTE, s)
    nj = s // ts
    row = lambda i, j: (i, j, 0)
    bat = lambda i, j: (i, 0, 0)
    return pl.pallas_call(
        _router_kernel,
        grid=(b, s // ts),
        in_specs=[
            pl.BlockSpec((1, ts, d), row),
            pl.BlockSpec((1, 1, d), bat), pl.BlockSpec((1, 1, d), bat),
            _const_spec(rw.shape), _const_spec(rb.shape),
        ],
        out_specs=[pl.BlockSpec((ts * ROW_TILE, LANES), lambda i, j: (i * nj + j, 0)),
                   pl.BlockSpec((1, ts, HEAD_PAD), row),
                   pl.BlockSpec((1, ts, HEAD_PAD), row)],
        out_shape=[jax.ShapeDtypeStruct((b * s * ROW_TILE, LANES), F32),
                   jax.ShapeDtypeStruct((b, s, HEAD_PAD), jnp.int32),
                   jax.ShapeDtypeStruct((b, s, HEAD_PAD), F32)],
        compiler_params=_cparams(2),
        name="moe_router",
    )(x, sc2, sh2, rw, rb)


def _experts_kernel(te_ref, nu_ref, idx_hbm, h_hbm, w1_ref, w3_ref, w2_ref, y_hbm,
                    idx_smem, xbuf, ybuf, sem_idx, sem_rows, sem_out):
    i = pl.program_id(0)
    n_used = nu_ref[0]
    n_tiles = pl.num_programs(0)
    tm = xbuf.shape[1] // ROW_TILE
    slot = lax.rem(i, 2)
    n_idx = idx_smem.shape[0]

    def index_copy(tile, step):
        return pltpu.make_async_copy(idx_hbm.at[tile], idx_smem.at[pl.ds(lax.rem(step, n_idx), 1)], sem_idx)

    def row_tile(r):
        start = r * ROW_TILE
        return pl.ds(start if isinstance(r, int) else pl.multiple_of(start, ROW_TILE), ROW_TILE)

    def gather_copy(step, r, buf=None):
        row = lax.rem(step, n_idx)
        buf = lax.rem(step, 2) if buf is None else buf
        return pltpu.make_async_copy(h_hbm.at[row_tile(idx_smem[row, r])],
                                     xbuf.at[buf, row_tile(r)], sem_rows.at[buf])

    def scatter_copy(step, r, buf=None):
        row = lax.rem(step, n_idx)
        buf = lax.rem(step, 2) if buf is None else buf
        return pltpu.make_async_copy(ybuf.at[buf, row_tile(r)],
                                     y_hbm.at[row_tile(idx_smem[row, tm + r])], sem_out.at[buf])

    def wait_gather(buf):
        pltpu.make_async_copy(h_hbm.at[pl.ds(0, tm * ROW_TILE)], xbuf.at[buf], sem_rows.at[buf]).wait()

    def wait_scatter(buf):
        pltpu.make_async_copy(ybuf.at[buf], y_hbm.at[pl.ds(0, tm * ROW_TILE)], sem_out.at[buf]).wait()

    @pl.when(i == 0)
    def _():
        for tile, step in ((0, 0), (jnp.minimum(1, n_tiles - 1), 1), (n_tiles, n_idx - 1)):
            index_copy(tile, step).start()
            index_copy(tile, step).wait()
        lax.fori_loop(0, tm, lambda r, c: (gather_copy(0, r).start(), c)[1], 0)
        ybuf[1] = jnp.zeros(ybuf.shape[1:], F32)

    def step(slot):
        wait_gather(slot)
        for r in range(tm):
            gather_copy(i + 1, r, 1 - slot).start()
        nxt2 = jnp.minimum(i + 2, n_tiles - 1)
        index_copy(nxt2, i + 2).start()
        for r in range(tm):
            scatter_copy(i + n_idx - 1, r, 1 - slot).start()
        x = _load_row_tiles(xbuf, tm, (slot,))
        _store_row_tiles(ybuf, _swiglu(x.astype(BF16), w1_ref[0], w3_ref[0], w2_ref[0]), (slot,))
        wait_scatter(1 - slot)
        index_copy(nxt2, i + 2).wait()

    for s in (0, 1):
        pl.when((i < n_used) & (slot == s))(functools.partial(step, s))

    @pl.when(i + 1 == n_used)
    def _():
        lax.fori_loop(0, tm, lambda r, c: (scatter_copy(i, r).start(), c)[1], 0)
        wait_scatter(slot)
        wait_gather(1 - slot)


def _experts(tile_expert, n_used, idx, h2, w1, w3, w2):
    n_tiles, tm2 = idx.shape[0] - 1, idx.shape[-1]
    tm = tm2 // 2
    t = h2.shape[0] // ROW_TILE
    _, d, f = w1.shape
    grid_spec = pltpu.PrefetchScalarGridSpec(
        num_scalar_prefetch=2,
        grid=(n_tiles,),
        in_specs=[
            pl.BlockSpec(memory_space=pl.ANY),
            pl.BlockSpec(memory_space=pl.ANY),
            pl.BlockSpec((1, d, f), lambda i, te, nu: (te[i], 0, 0)),
            pl.BlockSpec((1, d, f), lambda i, te, nu: (te[i], 0, 0)),
            pl.BlockSpec((1, f, d), lambda i, te, nu: (te[i], 0, 0)),
        ],
        out_specs=pl.BlockSpec(memory_space=pl.ANY),
        scratch_shapes=[
            pltpu.SMEM((4, 2 * tm), jnp.int32),
            pltpu.VMEM((2, tm * ROW_TILE, LANES), F32),
            pltpu.VMEM((2, tm * ROW_TILE, LANES), F32),
            pltpu.SemaphoreType.DMA(()),
            pltpu.SemaphoreType.DMA((2,)),
            pltpu.SemaphoreType.DMA((2,)),
        ],
    )
    return pl.pallas_call(
        _experts_kernel,
        grid_spec=grid_spec,
        out_shape=jax.ShapeDtypeStruct(((TOP_K * t + tm) * ROW_TILE, LANES), F32),
        compiler_params=_cparams(1),
        name="moe_experts",
    )(tile_expert, n_used, idx, h2, w1, w3, w2)


def _combine_kernel(alpha, y0_ref, y1_ref, gi_ref, x_ref, g2_ref, lng_ref, lnb_ref, o_ref):
    gi = gi_ref[0]
    n = gi.shape[0]
    f = gi[:, 0:1] * _load_row_tiles(y0_ref, n) + gi[:, 1:2] * _load_row_tiles(y1_ref, n)
    r = alpha * x_ref[0] + g2_ref[0] * f
    o_ref[0] = _layernorm(r, lng_ref[...], lnb_ref[...])


def _combine(alpha, y, gi, x, g2, lng, lnb):
    b, s, d = x.shape
    ts = min(TS_COMB, s)
    nj = s // ts
    row = lambda i, j: (i, j, 0)
    return pl.pallas_call(
        functools.partial(_combine_kernel, alpha),
        grid=(b, nj),
        in_specs=[
            pl.BlockSpec((ts * ROW_TILE, LANES), lambda i, j: (i * nj + j, 0)),
            pl.BlockSpec((ts * ROW_TILE, LANES), lambda i, j: (b * nj + i * nj + j, 0)),
            pl.BlockSpec((1, ts, HEAD_PAD), row),
            pl.BlockSpec((1, ts, d), row),
            pl.BlockSpec((1, 1, d), lambda i, j: (i, 0, 0)),
            _const_spec(lng.shape), _const_spec(lnb.shape),
        ],
        out_specs=pl.BlockSpec((1, ts, d), row),
        out_shape=jax.ShapeDtypeStruct((b, s, d), F32),
        compiler_params=_cparams(2),
        name="moe_combine_ln2",
    )(y, y, gi, x, g2, lng, lnb)


def _moe_plan(expert_idx, tm):
    t = expert_idx.shape[0]
    n_slots = t * TOP_K
    flat = expert_idx.reshape(n_slots)
    counts = jnp.sum((flat[:, None] == jnp.arange(N_EXPERTS, dtype=jnp.int32)[None, :]).astype(jnp.int32), axis=0)
    padded = (counts + tm - 1) // tm * tm
    pends = jnp.cumsum(padded)
    pstarts = pends - padded
    starts = jnp.cumsum(counts) - counts
    n_tiles = (n_slots + N_EXPERTS * tm) // tm
    tile_expert = jnp.minimum(
        jnp.searchsorted(pends, jnp.arange(n_tiles, dtype=jnp.int32) * tm, side="right"),
        N_EXPERTS - 1).astype(jnp.int32)
    n_used = (pends[-1] // tm).astype(jnp.int32).reshape(1)
    order = jnp.argsort(flat, stable=True).astype(jnp.int32)
    n_rows = n_tiles * tm
    rows = jnp.arange(n_rows, dtype=jnp.int32)
    order_pad = jnp.concatenate([order, jnp.zeros((n_rows - n_slots,), jnp.int32)])
    slot_of_row = jnp.full((n_rows,), -1, jnp.int32)
    for e in range(N_EXPERTS):
        shifted = jnp.roll(order_pad, pstarts[e] - starts[e])
        slot_of_row = jnp.where((rows >= pstarts[e]) & (rows < pstarts[e] + counts[e]), shifted, slot_of_row)
    valid = slot_of_row >= 0
    tok = slot_of_row // TOP_K
    src = jnp.where(valid, tok, 0)
    dst = jnp.where(valid, (slot_of_row % TOP_K) * t + tok, n_slots + rows % tm)
    idx = jnp.concatenate([src.reshape(n_tiles, 1, tm), dst.reshape(n_tiles, 1, tm)], axis=-1)
    dummy = jnp.concatenate([jnp.zeros((1, 1, tm), jnp.int32),
                             (n_slots + jnp.arange(tm, dtype=jnp.int32)).reshape(1, 1, tm)], axis=-1)
    return tile_expert, n_used, jnp.concatenate([idx.astype(jnp.int32), dummy], axis=0)


def _moe(alpha, x, sc2, sh2, g2, lng, lnb, rw, rb, w1, w3, w2):
    b, s, d = x.shape
    t = b * s
    h2, ei, gi = _router(x, sc2, sh2, rw, rb)
    tm = min(TM_MOE, t)
    tile_expert, n_used, idx = _moe_plan(ei.reshape(t, HEAD_PAD)[:, :TOP_K], tm)
    y = _experts(tile_expert, n_used, idx, h2, w1, w3, w2)
    return _combine(alpha, y, gi, x, g2, lng, lnb)


def _prep_layer_weights(w_in, w_q_up, w_kv_up, q_norm_g, kv_norm_g):
    n_l, d, _ = w_in.shape
    z96 = jnp.zeros((n_l, d, HEAD_PAD - MLA_ROPE), w_in.dtype)
    o = MLA_Q_LORA + MLA_KV_LORA + MLA_ROPE
    wlat = jnp.concatenate([w_in[:, :, :o], z96], axis=-1)
    nd = DIFF_HEADS * 2 * DIFF_HD
    wdq = w_in[:, :, o:o + nd]
    wdk = w_in[:, :, o + nd:o + 2 * nd]
    wdv = w_in[:, :, o + 2 * nd:o + 3 * nd]
    wga = w_in[:, :, o + 3 * nd:o + 3 * nd + d]
    wgb = w_in[:, :, o + 3 * nd + d:o + 3 * nd + 2 * d]

    hq = MLA_NOPE + MLA_ROPE
    wq4 = w_q_up.reshape(n_l, MLA_Q_LORA, MLA_HEADS, hq)
    zq = jnp.zeros((n_l, MLA_Q_LORA, MLA_HEADS, HEAD_PAD - hq), w_q_up.dtype)
    wq = jnp.concatenate([wq4, zq], axis=-1).reshape(n_l, MLA_Q_LORA, MLA_HEADS * HEAD_PAD)

    wkv4 = w_kv_up.reshape(n_l, MLA_KV_LORA, MLA_HEADS, MLA_NOPE + MLA_V)
    wkn = jnp.concatenate([wkv4[..., :MLA_NOPE], jnp.zeros_like(wkv4[..., :HEAD_PAD - MLA_NOPE])], axis=-1)
    wkn = wkn.reshape(n_l, MLA_KV_LORA, MLA_HEADS * HEAD_PAD)
    wv = wkv4[..., MLA_NOPE:].reshape(n_l, MLA_KV_LORA, MLA_HEADS * MLA_V)

    rr = jnp.arange(HEAD_PAD)[:, None]
    cc = jnp.arange(MLA_HEADS * HEAD_PAD)[None, :]
    esel = ((rr < MLA_ROPE) & (cc % HEAD_PAD == MLA_NOPE + rr)).astype(BF16)

    cast = lambda a: a.astype(BF16)
    return dict(wlat=cast(wlat), wdq=cast(wdq), wdk=cast(wdk), wdv=cast(wdv), wga=cast(wga), wgb=cast(wgb),
                wq=cast(wq), wkn=cast(wkn), wv=cast(wv), esel=esel,
                qg=q_norm_g.reshape(n_l, 1, MLA_Q_LORA), kvg=kv_norm_g.reshape(n_l, 1, MLA_KV_LORA))


def _rope_tables(positions):
    inv_freq = ROPE_BASE ** (-jnp.arange(0, MLA_ROPE, 2, dtype=F32) / MLA_ROPE)
    ang = positions.astype(F32)[..., None] * inv_freq
    cos, sin = jnp.cos(ang), jnp.sin(ang)
    ones = jnp.ones(positions.shape + (MLA_NOPE,), F32)
    tail = HEAD_PAD - MLA_NOPE - MLA_ROPE
    ct = jnp.concatenate([ones, cos, cos, ones[..., :tail]], axis=-1)
    st = jnp.concatenate([0.0 * ones, -sin, sin, 0.0 * ones[..., :tail]], axis=-1)
    return ct, st


def kernel(x, c, positions, w_ada, b_ada, w_in, q_norm_g, w_q_up, kv_norm_g, w_kv_up, lambda_q1, lambda_k1, lambda_q2, lambda_k2, diff_norm_g, w_br_mla, w_br_diff, w_out, ln1_g, ln1_b, ln2_g, ln2_b, ffn_w1, ffn_w3, ffn_w2, router_w, router_b, moe_w1, moe_w3, moe_w2):
    b, s, d = x.shape
    depth = w_in.shape[0]
    alpha = (2.0 * depth) ** 0.25

    mod = _ada_mod(c, w_ada, b_ada)
    ct, st = _rope_tables(positions)
    posf = positions.astype(F32)
    pos_k = posf.reshape(b, s, 1)
    pos_q = posf.reshape(b, 1, s)
    slopes = (2.0 ** (-8.0 * jnp.arange(1, DIFF_HEADS + 1, dtype=F32) / DIFF_HEADS)) * LOG2E

    lw = _prep_layer_weights(w_in, w_q_up, w_kv_up, q_norm_g, kv_norm_g)
    wbm, wbd, wo = w_br_mla.astype(BF16), w_br_diff.astype(BF16), w_out.astype(BF16)
    fw1, fw3, fw2 = ffn_w1.astype(BF16), ffn_w3.astype(BF16), ffn_w2.astype(BF16)
    mw1, mw3, mw2 = moe_w1.astype(BF16), moe_w3.astype(BF16), moe_w2.astype(BF16)
    rw = jnp.pad(router_w, ((0, 0), (0, 0), (0, HEAD_PAD - N_EXPERTS))).astype(BF16)
    rb = jnp.pad(router_b, ((0, 0), (0, HEAD_PAD - N_EXPERTS))).reshape(-1, 1, HEAD_PAD)
    gcol = diff_norm_g.reshape(depth, DIFF_HEADS, DIFF_VD, 1)
    vec = lambda a, l: a[l].reshape(1, -1)

    for l in range(depth):
        sh1, sc1, g1, sh2, sc2, g2 = [m.reshape(b, 1, d) for m in jnp.split(mod[l], 6, axis=-1)]
        w_l = {k: v[l] if k != "esel" else v for k, v in lw.items()}
        qm, km, vt, dq1, dq2, dk, dvt, ga, gb = _inproj(x, sc1, sh1, ct, st, w_l)
        mla_o = _mla_attention(qm, km, vt)
        lam_init = 0.8 - 0.6 * math.exp(-0.3 * l)
        diff_o = _diff_attention(dq1, dq2, dk, dvt, pos_k, pos_q, slopes,
                                 vec(lambda_q1, l), vec(lambda_k1, l), vec(lambda_q2, l), vec(lambda_k2, l),
                                 gcol[l], lam_init)
        x = _mix(alpha, mla_o, diff_o, ga, gb, x, g1, vec(ln1_g, l), vec(ln1_b, l), wbm[l], wbd[l], wo[l])
        if l % 2 == 0:
            x = _ffn(alpha, x, sc2, sh2, g2, vec(ln2_g, l), vec(ln2_b, l), fw1[l // 2], fw3[l // 2], fw2[l // 2])
        else:
            x = _moe(alpha, x, sc2, sh2, g2, vec(ln2_g, l), vec(ln2_b, l),
                     rw[l // 2], rb[l // 2], mw1[l // 2], mw3[l // 2], mw2[l // 2])
    return x
```

```python
import functools
import math

import jax
import jax.numpy as jnp
from jax import lax
from jax.experimental import pallas as pl
from jax.experimental.pallas import tpu as pltpu

BF16 = jnp.bfloat16
F32 = jnp.float32

D_MODEL = 1024
MLA_HEADS = 8
MLA_NOPE = 64
MLA_ROPE = 32
MLA_V = 64
MLA_Q_LORA = 384
MLA_KV_LORA = 256
ROPE_BASE = 10000.0
DIFF_HEADS = 8
DIFF_HD = 64
DIFF_VD = 2 * DIFF_HD
D_FF = 2816
N_EXPERTS = 8
TOP_K = 2
LN_EPS = 1e-5
RMS_EPS = 1e-6

HEAD_PAD = 128
ONES_ROWS = 16
LOG2E = 1.4426950408889634
MLA_QSCALE = (MLA_NOPE + MLA_ROPE) ** -0.5 * LOG2E
DIFF_QSCALE = DIFF_HD ** -0.5 * LOG2E

VMEM_LIMIT = 56 * 1024 * 1024
VMEM_LIMIT_ATTN = 60 * 1024 * 1024

TS_PROJ = 256
TQ_ATTN = 256
TS_MIX = 1024
TS_FFN = 512
TM_MOE = 256
TS_ROUTE = 512
TS_COMB = 512


def _cparams(n_axes):
    return pltpu.CompilerParams(dimension_semantics=("arbitrary",) * n_axes,
                                vmem_limit_bytes=VMEM_LIMIT)


def _const_spec(shape):
    nd = len(shape)
    return pl.BlockSpec(shape, lambda *_: (0,) * nd, pipeline_mode=pl.Buffered(1))


def _dot(a, b):
    return jnp.dot(a, b, preferred_element_type=F32)


def _dot_nt(a, b):
    return lax.dot_general(a, b, (((1,), (1,)), ((), ())), preferred_element_type=F32)


def _sigmoid(v):
    return 1.0 / (1.0 + jnp.exp(-v))


def _layernorm(r, g, b):
    mu = jnp.mean(r, axis=-1, keepdims=True)
    d = r - mu
    var = jnp.mean(d * d, axis=-1, keepdims=True)
    return d * lax.rsqrt(var + LN_EPS) * g + b


def _rms_rows(v, g):
    ms = jnp.mean(v * v, axis=-1, keepdims=True)
    return v * lax.rsqrt(ms + RMS_EPS) * g


LANES = 128
ROW_TILE = D_MODEL // LANES


def _store_row_tiles(ref, v, lead=()):
    n = v.shape[0]
    for c in range(ROW_TILE):
        ref[lead + (pl.ds(c, n, stride=ROW_TILE), slice(None))] = v[:, c * LANES:(c + 1) * LANES]


def _load_row_tiles(ref, n, lead=()):
    return jnp.concatenate(
        [ref[lead + (pl.ds(c, n, stride=ROW_TILE), slice(None))] for c in range(ROW_TILE)], axis=1)


def _ada_kernel(c_ref, w_ref, b_ref, o_ref):
    c = c_ref[...]
    cond = c * _sigmoid(c)
    o_ref[0] = _dot(cond.astype(BF16), w_ref[0].astype(BF16)) + b_ref[0]


def _ada_mod(c, w_ada, b_ada):
    n_l, d, n6 = w_ada.shape
    b = c.shape[0]
    tn = 1536
    return pl.pallas_call(
        _ada_kernel,
        grid=(n_l, n6 // tn),
        in_specs=[
            pl.BlockSpec((b, d), lambda l, j: (0, 0)),
            pl.BlockSpec((1, d, tn), lambda l, j: (l, 0, j)),
            pl.BlockSpec((1, 1, tn), lambda l, j: (l, 0, j)),
        ],
        out_specs=pl.BlockSpec((1, b, tn), lambda l, j: (l, 0, j)),
        out_shape=jax.ShapeDtypeStruct((n_l, b, n6), F32),
        compiler_params=_cparams(2),
        name="ada_mod",
    )(c, w_ada, b_ada.reshape(n_l, 1, n6))


def _inproj_kernel(x_ref, sc_ref, sh_ref, ct_ref, st_ref,
                   wlat_ref, wdq_ref, wdk_ref, wdv_ref, wga_ref, wgb_ref,
                   qg_ref, kvg_ref, wq_ref, wkn_ref, wv_ref, esel_ref,
                   qm_ref, km_ref, vt_ref, dq1_ref, dq2_ref, dk_ref, dvt_ref, ga_ref, gb_ref):
    x = x_ref[0]
    h = (x * (1.0 + sc_ref[0]) + sh_ref[0]).astype(BF16)
    half = MLA_ROPE // 2
    ct = ct_ref[0]
    st = st_ref[0]
    lane = lax.broadcasted_iota(jnp.int32, st.shape, 1)
    sa = jnp.where(lane < MLA_NOPE + half, st, 0.0)
    sb = st - sa

    def rotate(v, c, s_first, s_second):
        n = v.shape[1]
        return v * c + pltpu.roll(v, n - half, axis=1) * s_first + pltpu.roll(v, half, axis=1) * s_second

    lat = _dot(h, wlat_ref[...])
    q_lat = lat[:, :MLA_Q_LORA]
    kv_lat = lat[:, MLA_Q_LORA:MLA_Q_LORA + MLA_KV_LORA]
    kr = lat[:, MLA_Q_LORA + MLA_KV_LORA:]

    qn = _rms_rows(q_lat, qg_ref[...]).astype(BF16)
    q = _dot(qn, wq_ref[...])
    tile8 = lambda t: jnp.concatenate([t] * MLA_HEADS, axis=1)
    qr = (rotate(q, tile8(ct), tile8(sa), tile8(sb)) * MLA_QSCALE).astype(BF16)
    for hd in range(MLA_HEADS):
        qm_ref[0, hd] = qr[:, hd * HEAD_PAD:(hd + 1) * HEAD_PAD]

    kvn = _rms_rows(kv_lat, kvg_ref[...]).astype(BF16)
    kn = _dot(kvn, wkn_ref[...])
    to_front = lambda t: pltpu.roll(t, HEAD_PAD - MLA_NOPE, axis=1)
    kro = rotate(kr, to_front(ct), to_front(sa), to_front(sb)).astype(BF16)
    kcat = (kn + _dot(kro, esel_ref[...])).astype(BF16)
    for hd in range(MLA_HEADS):
        km_ref[0, hd] = kcat[:, hd * HEAD_PAD:(hd + 1) * HEAD_PAD]
    v = _dot(kvn, wv_ref[...])
    vt = v.T.astype(BF16)
    ones = jnp.ones((ONES_ROWS, vt.shape[1]), BF16)
    for hd in range(MLA_HEADS):
        vt_ref[0, hd, :MLA_V, :] = vt[hd * MLA_V:(hd + 1) * MLA_V, :]
        vt_ref[0, hd, MLA_V:, :] = ones

    dq = _dot(h, wdq_ref[...]) * DIFF_QSCALE
    lane = lax.broadcasted_iota(jnp.int32, dq.shape, 1)
    first = (lane % HEAD_PAD) < DIFF_HD
    dq1 = jnp.where(first, dq, 0.0).astype(BF16)
    dq2 = jnp.where(first, 0.0, dq).astype(BF16)
    dk = _dot(h, wdk_ref[...]).astype(BF16)
    dvt = _dot(h, wdv_ref[...]).T.astype(BF16)
    for hd in range(DIFF_HEADS):
        sl = slice(hd * HEAD_PAD, (hd + 1) * HEAD_PAD)
        dq1_ref[0, hd] = dq1[:, sl]
        dq2_ref[0, hd] = dq2[:, sl]
        dk_ref[0, hd] = dk[:, sl]
        dvt_ref[0, hd, :DIFF_VD, :] = dvt[sl, :]
        dvt_ref[0, hd, DIFF_VD:, :] = ones

    ga_ref[0] = _sigmoid(_dot(h, wga_ref[...])).astype(BF16)
    gb_ref[0] = _sigmoid(_dot(h, wgb_ref[...])).astype(BF16)


def _inproj(x, sc1, sh1, ct, st, w):
    b, s, d = x.shape
    ts = min(TS_PROJ, s)
    hh = MLA_HEADS
    row = lambda i, j: (i, j, 0)
    bat = lambda i, j: (i, 0, 0)
    head_rows = pl.BlockSpec((1, hh, ts, HEAD_PAD), lambda i, j: (i, 0, j, 0))
    weights = [w["wlat"], w["wdq"], w["wdk"], w["wdv"], w["wga"], w["wgb"],
               w["qg"], w["kvg"], w["wq"], w["wkn"], w["wv"], w["esel"]]
    head_shape = jax.ShapeDtypeStruct((b, hh, s, HEAD_PAD), BF16)
    return pl.pallas_call(
        _inproj_kernel,
        grid=(b, s // ts),
        in_specs=[
            pl.BlockSpec((1, ts, d), row),
            pl.BlockSpec((1, 1, d), bat),
            pl.BlockSpec((1, 1, d), bat),
            pl.BlockSpec((1, ts, HEAD_PAD), row),
            pl.BlockSpec((1, ts, HEAD_PAD), row),
        ] + [_const_spec(a.shape) for a in weights],
        out_specs=[
            head_rows, head_rows,
            pl.BlockSpec((1, hh, MLA_V + ONES_ROWS, ts), lambda i, j: (i, 0, 0, j)),
            head_rows, head_rows, head_rows,
            pl.BlockSpec((1, hh, DIFF_VD + ONES_ROWS, ts), lambda i, j: (i, 0, 0, j)),
            pl.BlockSpec((1, ts, d), row),
            pl.BlockSpec((1, ts, d), row),
        ],
        out_shape=[
            head_shape, head_shape,
            jax.ShapeDtypeStruct((b, hh, MLA_V + ONES_ROWS, s), BF16),
            head_shape, head_shape, head_shape,
            jax.ShapeDtypeStruct((b, hh, DIFF_VD + ONES_ROWS, s), BF16),
            jax.ShapeDtypeStruct((b, s, d), BF16),
            jax.ShapeDtypeStruct((b, s, d), BF16),
        ],
        compiler_params=_cparams(2),
        name="inproj",
    )(x, sc1, sh1, ct, st, *weights)


def _attn_kernel(lam_init, mq_ref, mk_ref, mvt_ref, q1_ref, q2_ref, k_ref, vt_ref, pk_ref, pq_ref,
                 slope_ref, lq1_ref, lk1_ref, lq2_ref, lk2_ref, g_ref, mo_ref, o_ref,
                 macc_ref, acc_ref, dist_ref, za_ref, zb_ref, ya_ref, yb_ref):
    dist_ref[...] = jnp.abs(pk_ref[0] - pq_ref[0])
    lam = (jnp.exp(jnp.sum(lq1_ref[...] * lk1_ref[...], axis=1, keepdims=True))
           - jnp.exp(jnp.sum(lq2_ref[...] * lk2_ref[...], axis=1, keepdims=True))
           + lam_init)

    def diff_scores(hd, z_ref):
        k = k_ref[0, hd]
        bias = slope_ref[hd] * dist_ref[...]
        z1 = _dot_nt(k, q1_ref[0, hd]) - bias
        z2 = _dot_nt(k, q2_ref[0, hd]) - bias
        z_ref[0] = z1
        z_ref[1] = z2
        return jnp.max(z1, axis=0, keepdims=True), jnp.max(z2, axis=0, keepdims=True)

    def diff_values(hd, z_ref, m1, m2):
        e1 = jnp.exp2(z_ref[0] - m1).astype(BF16)
        e2 = jnp.exp2(z_ref[1] - m2).astype(BF16)
        o1 = _dot(vt_ref[0, hd], e1)
        o2 = _dot(vt_ref[0, hd], e2)
        r1 = 1.0 / o1[DIFF_VD:DIFF_VD + 1]
        r2 = lam / o2[DIFF_VD:DIFF_VD + 1]
        o_t = o1[:DIFF_VD] * r1 - o2[:DIFF_VD] * r2
        ms = jnp.mean(o_t * o_t, axis=0, keepdims=True)
        o_t = o_t * lax.rsqrt(ms + RMS_EPS) * g_ref[hd] * (1.0 - lam_init)
        acc_ref[pl.ds(pl.multiple_of(hd * DIFF_VD, DIFF_VD), DIFF_VD), :] = o_t

    def mla_scores(hd, y_ref):
        s_t = _dot_nt(mk_ref[0, hd], mq_ref[0, hd])
        y_ref[...] = s_t
        return jnp.max(s_t, axis=0, keepdims=True)

    def mla_values(hd, y_ref, m):
        p = jnp.exp2(y_ref[...] - m).astype(BF16)
        o_t = _dot(mvt_ref[0, hd], p)
        macc_ref[pl.ds(pl.multiple_of(hd * MLA_V, MLA_V), MLA_V), :] = o_t[:MLA_V] / o_t[MLA_V:MLA_V + 1]

    def scores(hd, z_ref, y_ref):
        return diff_scores(hd, z_ref), mla_scores(hd, y_ref)

    def values(hd, z_ref, y_ref, m):
        diff_values(hd, z_ref, *m[0])
        mla_values(hd, y_ref, m[1])

    def body(i, ma):
        h0 = 2 * i
        mb = scores(h0 + 1, zb_ref, yb_ref)
        values(h0, za_ref, ya_ref, ma)
        ma = scores(h0 + 2, za_ref, ya_ref)
        values(h0 + 1, zb_ref, yb_ref, mb)
        return ma

    last = DIFF_HEADS - 1
    ma = lax.fori_loop(0, DIFF_HEADS // 2 - 1, body, scores(0, za_ref, ya_ref))
    mb = scores(last, zb_ref, yb_ref)
    values(last - 1, za_ref, ya_ref, ma)
    values(last, zb_ref, yb_ref, mb)
    o_ref[0] = acc_ref[...].T.astype(BF16)
    mo_ref[0] = macc_ref[...].T.astype(BF16)


def _attention(qm, km, vt, dq1, dq2, dk, dvt, pos_k, pos_q, slopes, lq1, lk1, lq2, lk2, gcol, lam_init):
    assert MLA_HEADS == DIFF_HEADS
    b, hh, s, _ = dk.shape
    tq = min(TQ_ATTN, s)
    head_q = pl.BlockSpec((1, hh, tq, HEAD_PAD), lambda i, j: (i, 0, j, 0))
    head_k = pl.BlockSpec((1, hh, s, HEAD_PAD), lambda i, j: (i, 0, 0, 0))
    vec = pl.BlockSpec((1, DIFF_HD), lambda i, j: (0, 0))
    return pl.pallas_call(
        functools.partial(_attn_kernel, lam_init),
        grid=(b, s // tq),
        in_specs=[
            head_q, head_k,
            pl.BlockSpec((1, hh, MLA_V + ONES_ROWS, s), lambda i, j: (i, 0, 0, 0)),
            head_q, head_q, head_k,
            pl.BlockSpec((1, hh, DIFF_VD + ONES_ROWS, s), lambda i, j: (i, 0, 0, 0)),
            pl.BlockSpec((1, s, 1), lambda i, j: (i, 0, 0)),
            pl.BlockSpec((1, 1, tq), lambda i, j: (i, 0, j)),
            pl.BlockSpec(memory_space=pltpu.SMEM),
            vec, vec, vec, vec,
            pl.BlockSpec((hh, DIFF_VD, 1), lambda i, j: (0, 0, 0)),
        ],
        out_specs=[pl.BlockSpec((1, tq, hh * MLA_V), lambda i, j: (i, j, 0)),
                   pl.BlockSpec((1, tq, hh * DIFF_VD), lambda i, j: (i, j, 0))],
        out_shape=[jax.ShapeDtypeStruct((b, s, hh * MLA_V), BF16),
                   jax.ShapeDtypeStruct((b, s, hh * DIFF_VD), BF16)],
        scratch_shapes=[pltpu.VMEM((hh * MLA_V, tq), F32), pltpu.VMEM((hh * DIFF_VD, tq), F32),
                        pltpu.VMEM((s, tq), F32),
                        pltpu.VMEM((2, s, tq), F32), pltpu.VMEM((2, s, tq), F32),
                        pltpu.VMEM((s, tq), F32), pltpu.VMEM((s, tq), F32)],
        compiler_params=pltpu.CompilerParams(dimension_semantics=("arbitrary", "arbitrary"),
                                             vmem_limit_bytes=VMEM_LIMIT_ATTN),
        name="attention",
    )(qm, km, vt, dq1, dq2, dk, dvt, pos_k, pos_q, slopes, lq1, lk1, lq2, lk2, gcol)


def _mix_kernel(alpha, mo_ref, do_ref, ga_ref, gb_ref, x_ref, g1_ref, lng_ref, lnb_ref,
                wbm_ref, wbd_ref, wo_ref, o_ref):
    ya = _dot(mo_ref[0], wbm_ref[...])
    yb = _dot(do_ref[0], wbd_ref[...])
    gated = (ga_ref[0].astype(F32) * ya + gb_ref[0].astype(F32) * yb).astype(BF16)
    mix = _dot(gated, wo_ref[...])
    r = alpha * x_ref[0] + g1_ref[0] * mix
    o_ref[0] = _layernorm(r, lng_ref[...], lnb_ref[...])


def _mix(alpha, mla_o, diff_o, ga, gb, x, g1, lng, lnb, wbm, wbd, wo):
    b, s, d = x.shape
    ts = min(TS_MIX, s)
    row = lambda i, j: (i, j, 0)
    return pl.pallas_call(
        functools.partial(_mix_kernel, alpha),
        grid=(b, s // ts),
        in_specs=[
            pl.BlockSpec((1, ts, mla_o.shape[-1]), row),
            pl.BlockSpec((1, ts, diff_o.shape[-1]), row),
            pl.BlockSpec((1, ts, d), row),
            pl.BlockSpec((1, ts, d), row),
            pl.BlockSpec((1, ts, d), row),
            pl.BlockSpec((1, 1, d), lambda i, j: (i, 0, 0)),
            _const_spec(lng.shape), _const_spec(lnb.shape),
            _const_spec(wbm.shape), _const_spec(wbd.shape), _const_spec(wo.shape),
        ],
        out_specs=pl.BlockSpec((1, ts, d), row),
        out_shape=jax.ShapeDtypeStruct((b, s, d), F32),
        compiler_params=_cparams(2),
        name="mix_ln1",
    )(mla_o, diff_o, ga, gb, x, g1, lng, lnb, wbm, wbd, wo)


def _swiglu(h, w1, w3, w2):
    a = _dot(h, w1)
    bgate = _dot(h, w3)
    u = (a * _sigmoid(a) * bgate).astype(BF16)
    return _dot(u, w2)


def _ffn_kernel(alpha, x_ref, sc_ref, sh_ref, g2_ref, lng_ref, lnb_ref, w1_ref, w3_ref, w2_ref, o_ref):
    x = x_ref[0]
    h = (x * (1.0 + sc_ref[0]) + sh_ref[0]).astype(BF16)
    f = _swiglu(h, w1_ref[...], w3_ref[...], w2_ref[...])
    r = alpha * x + g2_ref[0] * f
    o_ref[0] = _layernorm(r, lng_ref[...], lnb_ref[...])


def _ffn(alpha, x, sc2, sh2, g2, lng, lnb, w1, w3, w2):
    b, s, d = x.shape
    ts = min(TS_FFN, s)
    row = lambda i, j: (i, j, 0)
    bat = lambda i, j: (i, 0, 0)
    return pl.pallas_call(
        functools.partial(_ffn_kernel, alpha),
        grid=(b, s // ts),
        in_specs=[
            pl.BlockSpec((1, ts, d), row),
            pl.BlockSpec((1, 1, d), bat), pl.BlockSpec((1, 1, d), bat), pl.BlockSpec((1, 1, d), bat),
            _const_spec(lng.shape), _const_spec(lnb.shape),
            _const_spec(w1.shape), _const_spec(w3.shape), _const_spec(w2.shape),
        ],
        out_specs=pl.BlockSpec((1, ts, d), row),
        out_shape=jax.ShapeDtypeStruct((b, s, d), F32),
        compiler_params=_cparams(2),
        name="ffn_ln2",
    )(x, sc2, sh2, g2, lng, lnb, w1, w3, w2)


def _router_kernel(x_ref, sc_ref, sh_ref, rw_ref, rb_ref, h_ref, ei_ref, gi_ref):
    h = x_ref[0] * (1.0 + sc_ref[0]) + sh_ref[0]
    _store_row_tiles(h_ref, h)
    logits = _dot(h.astype(BF16), rw_ref[...]) + rb_ref[...]
    lane = lax.broadcasted_iota(jnp.int32, logits.shape, 1)
    lane_f = lane.astype(F32)
    neg = jnp.float32(-jnp.inf)
    lg = jnp.where(lane < N_EXPERTS, logits, neg)
    m1 = jnp.max(lg, axis=1, keepdims=True)
    i1 = jnp.min(jnp.where(lg == m1, lane_f, 128.0), axis=1, keepdims=True)
    lg2 = jnp.where(lane_f == i1, neg, lg)
    m2 = jnp.max(lg2, axis=1, keepdims=True)
    i2 = jnp.min(jnp.where(lg2 == m2, lane_f, 128.0), axis=1, keepdims=True)
    t = jnp.exp(m2 - m1)
    den = 1.0 + t
    ei_ref[0] = jnp.where(lane == 0, i1, jnp.where(lane == 1, i2, 0.0)).astype(jnp.int32)
    gi_ref[0] = jnp.where(lane == 0, 1.0 / den, jnp.where(lane == 1, t / den, 0.0))


def _router(x, sc2, sh2, rw, rb):
    b, s, d = x.shape
    ts = min(TS_ROUTE, s)
    nj = s // ts
    row = lambda i, j: (i, j, 0)
    bat = lambda i, j: (i, 0, 0)
    return pl.pallas_call(
        _router_kernel,
        grid=(b, s // ts),
        in_specs=[
            pl.BlockSpec((1, ts, d), row),
            pl.BlockSpec((1, 1, d), bat), pl.BlockSpec((1, 1, d), bat),
            _const_spec(rw.shape), _const_spec(rb.shape),
        ],
        out_specs=[pl.BlockSpec((ts * ROW_TILE, LANES), lambda i, j: (i * nj + j, 0)),
                   pl.BlockSpec((1, ts, HEAD_PAD), row),
                   pl.BlockSpec((1, ts, HEAD_PAD), row)],
        out_shape=[jax.ShapeDtypeStruct((b * s * ROW_TILE, LANES), F32),
                   jax.ShapeDtypeStruct((b, s, HEAD_PAD), jnp.int32),
                   jax.ShapeDtypeStruct((b, s, HEAD_PAD), F32)],
        compiler_params=_cparams(2),
        name="moe_router",
    )(x, sc2, sh2, rw, rb)


def _experts_kernel(te_ref, nu_ref, idx_hbm, h_hbm, w1_ref, w3_ref, w2_ref, y_hbm,
                    idx_smem, xbuf, ybuf, sem_idx, sem_rows, sem_out):
    i = pl.program_id(0)
    n_used = nu_ref[0]
    n_tiles = pl.num_programs(0)
    tm = xbuf.shape[1] // ROW_TILE
    slot = lax.rem(i, 2)
    n_idx = idx_smem.shape[0]

    def index_copy(tile, step):
        return pltpu.make_async_copy(idx_hbm.at[tile], idx_smem.at[pl.ds(lax.rem(step, n_idx), 1)], sem_idx)

    def row_tile(r):
        start = r * ROW_TILE
        return pl.ds(start if isinstance(r, int) else pl.multiple_of(start, ROW_TILE), ROW_TILE)

    def gather_copy(step, r, buf=None):
        row = lax.rem(step, n_idx)
        buf = lax.rem(step, 2) if buf is None else buf
        return pltpu.make_async_copy(h_hbm.at[row_tile(idx_smem[row, r])],
                                     xbuf.at[buf, row_tile(r)], sem_rows.at[buf])

    def scatter_copy(step, r, buf=None):
        row = lax.rem(step, n_idx)
        buf = lax.rem(step, 2) if buf is None else buf
        return pltpu.make_async_copy(ybuf.at[buf, row_tile(r)],
                                     y_hbm.at[row_tile(idx_smem[row, tm + r])], sem_out.at[buf])

    def wait_gather(buf):
        pltpu.make_async_copy(h_hbm.at[pl.ds(0, tm * ROW_TILE)], xbuf.at[buf], sem_rows.at[buf]).wait()

    def wait_scatter(buf):
        pltpu.make_async_copy(ybuf.at[buf], y_hbm.at[pl.ds(0, tm * ROW_TILE)], sem_out.at[buf]).wait()

    @pl.when(i == 0)
    def _():
        for tile, step in ((0, 0), (jnp.minimum(1, n_tiles - 1), 1), (n_tiles, n_idx - 1)):
            index_copy(tile, step).start()
            index_copy(tile, step).wait()
        lax.fori_loop(0, tm, lambda r, c: (gather_copy(0, r).start(), c)[1], 0)
        ybuf[1] = jnp.zeros(ybuf.shape[1:], F32)

    def step(slot):
        wait_gather(slot)
        for r in range(tm):
            gather_copy(i + 1, r, 1 - slot).start()
        nxt2 = jnp.minimum(i + 2, n_tiles - 1)
        index_copy(nxt2, i + 2).start()
        for r in range(tm):
            scatter_copy(i + n_idx - 1, r, 1 - slot).start()
        x = _load_row_tiles(xbuf, tm, (slot,))
        _store_row_tiles(ybuf, _swiglu(x.astype(BF16), w1_ref[0], w3_ref[0], w2_ref[0]), (slot,))
        wait_scatter(1 - slot)
        index_copy(nxt2, i + 2).wait()

    for s in (0, 1):
        pl.when((i < n_used) & (slot == s))(functools.partial(step, s))

    @pl.when(i + 1 == n_used)
    def _():
        lax.fori_loop(0, tm, lambda r, c: (scatter_copy(i, r).start(), c)[1], 0)
        wait_scatter(slot)
        wait_gather(1 - slot)


def _experts(tile_expert, n_used, idx, h2, w1, w3, w2):
    n_tiles, tm2 = idx.shape[0] - 1, idx.shape[-1]
    tm = tm2 // 2
    t = h2.shape[0] // ROW_TILE
    _, d, f = w1.shape
    grid_spec = pltpu.PrefetchScalarGridSpec(
        num_scalar_prefetch=2,
        grid=(n_tiles,),
        in_specs=[
            pl.BlockSpec(memory_space=pl.ANY),
            pl.BlockSpec(memory_space=pl.ANY),
            pl.BlockSpec((1, d, f), lambda i, te, nu: (te[i], 0, 0)),
            pl.BlockSpec((1, d, f), lambda i, te, nu: (te[i], 0, 0)),
            pl.BlockSpec((1, f, d), lambda i, te, nu: (te[i], 0, 0)),
        ],
        out_specs=pl.BlockSpec(memory_space=pl.ANY),
        scratch_shapes=[
            pltpu.SMEM((4, 2 * tm), jnp.int32),
            pltpu.VMEM((2, tm * ROW_TILE, LANES), F32),
            pltpu.VMEM((2, tm * ROW_TILE, LANES), F32),
            pltpu.SemaphoreType.DMA(()),
            pltpu.SemaphoreType.DMA((2,)),
            pltpu.SemaphoreType.DMA((2,)),
        ],
    )
    return pl.pallas_call(
        _experts_kernel,
        grid_spec=grid_spec,
        out_shape=jax.ShapeDtypeStruct(((TOP_K * t + tm) * ROW_TILE, LANES), F32),
        compiler_params=_cparams(1),
        name="moe_experts",
    )(tile_expert, n_used, idx, h2, w1, w3, w2)


def _combine_kernel(alpha, y0_ref, y1_ref, gi_ref, x_ref, g2_ref, lng_ref, lnb_ref, o_ref):
    gi = gi_ref[0]
    n = gi.shape[0]
    f = gi[:, 0:1] * _load_row_tiles(y0_ref, n) + gi[:, 1:2] * _load_row_tiles(y1_ref, n)
    r = alpha * x_ref[0] + g2_ref[0] * f
    o_ref[0] = _layernorm(r, lng_ref[...], lnb_ref[...])


def _combine(alpha, y, gi, x, g2, lng, lnb):
    b, s, d = x.shape
    ts = min(TS_COMB, s)
    nj = s // ts
    row = lambda i, j: (i, j, 0)
    return pl.pallas_call(
        functools.partial(_combine_kernel, alpha),
        grid=(b, nj),
        in_specs=[
            pl.BlockSpec((ts * ROW_TILE, LANES), lambda i, j: (i * nj + j, 0)),
            pl.BlockSpec((ts * ROW_TILE, LANES), lambda i, j: (b * nj + i * nj + j, 0)),
            pl.BlockSpec((1, ts, HEAD_PAD), row),
            pl.BlockSpec((1, ts, d), row),
            pl.BlockSpec((1, 1, d), lambda i, j: (i, 0, 0)),
            _const_spec(lng.shape), _const_spec(lnb.shape),
        ],
        out_specs=pl.BlockSpec((1, ts, d), row),
        out_shape=jax.ShapeDtypeStruct((b, s, d), F32),
        compiler_params=_cparams(2),
        name="moe_combine_ln2",
    )(y, y, gi, x, g2, lng, lnb)


def _moe_plan(expert_idx, tm):
    t = expert_idx.shape[0]
    n_slots = t * TOP_K
    flat = expert_idx.reshape(n_slots)
    counts = jnp.sum((flat[:, None] == jnp.arange(N_EXPERTS, dtype=jnp.int32)[None, :]).astype(jnp.int32), axis=0)
    padded = (counts + tm - 1) // tm * tm
    pends = jnp.cumsum(padded)
    pstarts = pends - padded
    starts = jnp.cumsum(counts) - counts
    n_tiles = (n_slots + N_EXPERTS * tm) // tm
    tile_expert = jnp.minimum(
        jnp.searchsorted(pends, jnp.arange(n_tiles, dtype=jnp.int32) * tm, side="right"),
        N_EXPERTS - 1).astype(jnp.int32)
    n_used = (pends[-1] // tm).astype(jnp.int32).reshape(1)
    order = jnp.argsort(flat, stable=True).astype(jnp.int32)
    n_rows = n_tiles * tm
    rows = jnp.arange(n_rows, dtype=jnp.int32)
    order_pad = jnp.concatenate([order, jnp.zeros((n_rows - n_slots,), jnp.int32)])
    slot_of_row = jnp.full((n_rows,), -1, jnp.int32)
    for e in range(N_EXPERTS):
        shifted = jnp.roll(order_pad, pstarts[e] - starts[e])
        slot_of_row = jnp.where((rows >= pstarts[e]) & (rows < pstarts[e] + counts[e]), shifted, slot_of_row)
    valid = slot_of_row >= 0
    tok = slot_of_row // TOP_K
    src = jnp.where(valid, tok, 0)
    dst = jnp.where(valid, (slot_of_row % TOP_K) * t + tok, n_slots + rows % tm)
    idx = jnp.concatenate([src.reshape(n_tiles, 1, tm), dst.reshape(n_tiles, 1, tm)], axis=-1)
    dummy = jnp.concatenate([jnp.zeros((1, 1, tm), jnp.int32),
                             (n_slots + jnp.arange(tm, dtype=jnp.int32)).reshape(1, 1, tm)], axis=-1)
    return tile_expert, n_used, jnp.concatenate([idx.astype(jnp.int32), dummy], axis=0)


def _moe(alpha, x, sc2, sh2, g2, lng, lnb, rw, rb, w1, w3, w2):
    b, s, d = x.shape
    t = b * s
    h2, ei, gi = _router(x, sc2, sh2, rw, rb)
    tm = min(TM_MOE, t)
    tile_expert, n_used, idx = _moe_plan(ei.reshape(t, HEAD_PAD)[:, :TOP_K], tm)
    y = _experts(tile_expert, n_used, idx, h2, w1, w3, w2)
    return _combine(alpha, y, gi, x, g2, lng, lnb)


def _prep_layer_weights(w_in, w_q_up, w_kv_up, q_norm_g, kv_norm_g):
    n_l, d, _ = w_in.shape
    z96 = jnp.zeros((n_l, d, HEAD_PAD - MLA_ROPE), w_in.dtype)
    o = MLA_Q_LORA + MLA_KV_LORA + MLA_ROPE
    wlat = jnp.concatenate([w_in[:, :, :o], z96], axis=-1)
    nd = DIFF_HEADS * 2 * DIFF_HD
    wdq = w_in[:, :, o:o + nd]
    wdk = w_in[:, :, o + nd:o + 2 * nd]
    wdv = w_in[:, :, o + 2 * nd:o + 3 * nd]
    wga = w_in[:, :, o + 3 * nd:o + 3 * nd + d]
    wgb = w_in[:, :, o + 3 * nd + d:o + 3 * nd + 2 * d]

    hq = MLA_NOPE + MLA_ROPE
    wq4 = w_q_up.reshape(n_l, MLA_Q_LORA, MLA_HEADS, hq)
    zq = jnp.zeros((n_l, MLA_Q_LORA, MLA_HEADS, HEAD_PAD - hq), w_q_up.dtype)
    wq = jnp.concatenate([wq4, zq], axis=-1).reshape(n_l, MLA_Q_LORA, MLA_HEADS * HEAD_PAD)

    wkv4 = w_kv_up.reshape(n_l, MLA_KV_LORA, MLA_HEADS, MLA_NOPE + MLA_V)
    wkn = jnp.concatenate([wkv4[..., :MLA_NOPE], jnp.zeros_like(wkv4[..., :HEAD_PAD - MLA_NOPE])], axis=-1)
    wkn = wkn.reshape(n_l, MLA_KV_LORA, MLA_HEADS * HEAD_PAD)
    wv = wkv4[..., MLA_NOPE:].reshape(n_l, MLA_KV_LORA, MLA_HEADS * MLA_V)

    rr = jnp.arange(HEAD_PAD)[:, None]
    cc = jnp.arange(MLA_HEADS * HEAD_PAD)[None, :]
    esel = ((rr < MLA_ROPE) & (cc % HEAD_PAD == MLA_NOPE + rr)).astype(BF16)

    cast = lambda a: a.astype(BF16)
    return dict(wlat=cast(wlat), wdq=cast(wdq), wdk=cast(wdk), wdv=cast(wdv), wga=cast(wga), wgb=cast(wgb),
                wq=cast(wq), wkn=cast(wkn), wv=cast(wv), esel=esel,
                qg=q_norm_g.reshape(n_l, 1, MLA_Q_LORA), kvg=kv_norm_g.reshape(n_l, 1, MLA_KV_LORA))


def _rope_tables(positions):
    inv_freq = ROPE_BASE ** (-jnp.arange(0, MLA_ROPE, 2, dtype=F32) / MLA_ROPE)
    ang = positions.astype(F32)[..., None] * inv_freq
    cos, sin = jnp.cos(ang), jnp.sin(ang)
    ones = jnp.ones(positions.shape + (MLA_NOPE,), F32)
    tail = HEAD_PAD - MLA_NOPE - MLA_ROPE
    ct = jnp.concatenate([ones, cos, cos, ones[..., :tail]], axis=-1)
    st = jnp.concatenate([0.0 * ones, -sin, sin, 0.0 * ones[..., :tail]], axis=-1)
    return ct, st


def kernel(x, c, positions, w_ada, b_ada, w_in, q_norm_g, w_q_up, kv_norm_g, w_kv_up, lambda_q1, lambda_k1, lambda_q2, lambda_k2, diff_norm_g, w_br_mla, w_br_diff, w_out, ln1_g, ln1_b, ln2_g, ln2_b, ffn_w1, ffn_w3, ffn_w2, router_w, router_b, moe_w1, moe_w3, moe_w2):
    b, s, d = x.shape
    depth = w_in.shape[0]
    alpha = (2.0 * depth) ** 0.25

    mod = _ada_mod(c, w_ada, b_ada)
    ct, st = _rope_tables(positions)
    posf = positions.astype(F32)
    pos_k = posf.reshape(b, s, 1)
    pos_q = posf.reshape(b, 1, s)
    slopes = (2.0 ** (-8.0 * jnp.arange(1, DIFF_HEADS + 1, dtype=F32) / DIFF_HEADS)) * LOG2E

    lw = _prep_layer_weights(w_in, w_q_up, w_kv_up, q_norm_g, kv_norm_g)
    wbm, wbd, wo = w_br_mla.astype(BF16), w_br_diff.astype(BF16), w_out.astype(BF16)
    fw1, fw3, fw2 = ffn_w1.astype(BF16), ffn_w3.astype(BF16), ffn_w2.astype(BF16)
    mw1, mw3, mw2 = moe_w1.astype(BF16), moe_w3.astype(BF16), moe_w2.astype(BF16)
    rw = jnp.pad(router_w, ((0, 0), (0, 0), (0, HEAD_PAD - N_EXPERTS))).astype(BF16)
    rb = jnp.pad(router_b, ((0, 0), (0, HEAD_PAD - N_EXPERTS))).reshape(-1, 1, HEAD_PAD)
    gcol = diff_norm_g.reshape(depth, DIFF_HEADS, DIFF_VD, 1)
    vec = lambda a, l: a[l].reshape(1, -1)

    for l in range(depth):
        sh1, sc1, g1, sh2, sc2, g2 = [m.reshape(b, 1, d) for m in jnp.split(mod[l], 6, axis=-1)]
        w_l = {k: v[l] if k != "esel" else v for k, v in lw.items()}
        qm, km, vt, dq1, dq2, dk, dvt, ga, gb = _inproj(x, sc1, sh1, ct, st, w_l)
        lam_init = 0.8 - 0.6 * math.exp(-0.3 * l)
        mla_o, diff_o = _attention(qm, km, vt, dq1, dq2, dk, dvt, pos_k, pos_q, slopes,
                                   vec(lambda_q1, l), vec(lambda_k1, l), vec(lambda_q2, l), vec(lambda_k2, l),
                                   gcol[l], lam_init)
        x = _mix(alpha, mla_o, diff_o, ga, gb, x, g1, vec(ln1_g, l), vec(ln1_b, l), wbm[l], wbd[l], wo[l])
        if l % 2 == 0:
            x = _ffn(alpha, x, sc2, sh2, g2, vec(ln2_g, l), vec(ln2_b, l), fw1[l // 2], fw3[l // 2], fw2[l // 2])
        else:
            x = _moe(alpha, x, sc2, sh2, g2, vec(ln2_g, l), vec(ln2_b, l),
                     rw[l // 2], rb[l // 2], mw1[l // 2], mw3[l // 2], mw2[l // 2])
    return x
```

```python
import functools
import math

import jax
import jax.numpy as jnp
from jax import lax
from jax.experimental import pallas as pl
from jax.experimental.pallas import tpu as pltpu

BF16 = jnp.bfloat16
F32 = jnp.float32

D_MODEL = 1024
MLA_HEADS = 8
MLA_NOPE = 64
MLA_ROPE = 32
MLA_V = 64
MLA_Q_LORA = 384
MLA_KV_LORA = 256
ROPE_BASE = 10000.0
DIFF_HEADS = 8
DIFF_HD = 64
DIFF_VD = 2 * DIFF_HD
N_EXPERTS = 8
TOP_K = 2
LN_EPS = 1e-5
RMS_EPS = 1e-6

HEAD_PAD = 128
ONES_ROWS = 16
LOG2E = 1.4426950408889634
MLA_QSCALE = (MLA_NOPE + MLA_ROPE) ** -0.5 * LOG2E
DIFF_QSCALE = DIFF_HD ** -0.5 * LOG2E

VMEM_LIMIT = 56 * 1024 * 1024
VMEM_LIMIT_ATTN = 60 * 1024 * 1024

TS_PROJ = 512
TQ_ATTN = 256
TS_MIX = 1024
TS_FFN = 512
TM_MOE = 256
TS_ROUTE = 512
TS_COMB = 512


def _cparams(n_axes):
    return pltpu.CompilerParams(dimension_semantics=("arbitrary",) * n_axes,
                                vmem_limit_bytes=VMEM_LIMIT)


def _const_spec(shape):
    nd = len(shape)
    return pl.BlockSpec(shape, lambda *_: (0,) * nd, pipeline_mode=pl.Buffered(1))


def _dot(a, b):
    return jnp.dot(a, b, preferred_element_type=F32)


def _dot_nt(a, b):
    return lax.dot_general(a, b, (((1,), (1,)), ((), ())), preferred_element_type=F32)


def _sigmoid(v):
    return 1.0 / (1.0 + jnp.exp(-v))


def _layernorm(r, g, b):
    mu = jnp.mean(r, axis=-1, keepdims=True)
    d = r - mu
    var = jnp.mean(d * d, axis=-1, keepdims=True)
    return d * lax.rsqrt(var + LN_EPS) * g + b


def _rms_rows(v, g):
    ms = jnp.mean(v * v, axis=-1, keepdims=True)
    return v * lax.rsqrt(ms + RMS_EPS) * g


LANES = 128
ROW_TILE = D_MODEL // LANES


def _store_row_tiles(ref, v, lead=()):
    n = v.shape[0]
    for c in range(ROW_TILE):
        ref[lead + (pl.ds(c, n, stride=ROW_TILE), slice(None))] = v[:, c * LANES:(c + 1) * LANES]


def _load_row_tiles(ref, n, lead=()):
    return jnp.concatenate(
        [ref[lead + (pl.ds(c, n, stride=ROW_TILE), slice(None))] for c in range(ROW_TILE)], axis=1)


def _ada_kernel(c_ref, w_ref, b_ref, o_ref):
    c = c_ref[...]
    cond = c * _sigmoid(c)
    o_ref[0] = _dot(cond.astype(BF16), w_ref[0].astype(BF16)) + b_ref[0]


def _ada_mod(c, w_ada, b_ada):
    n_l, d, n6 = w_ada.shape
    b = c.shape[0]
    tn = 1536
    return pl.pallas_call(
        _ada_kernel,
        grid=(n_l, n6 // tn),
        in_specs=[
            pl.BlockSpec((b, d), lambda l, j: (0, 0)),
            pl.BlockSpec((1, d, tn), lambda l, j: (l, 0, j)),
            pl.BlockSpec((1, 1, tn), lambda l, j: (l, 0, j)),
        ],
        out_specs=pl.BlockSpec((1, b, tn), lambda l, j: (l, 0, j)),
        out_shape=jax.ShapeDtypeStruct((n_l, b, n6), F32),
        compiler_params=_cparams(2),
        name="ada_mod",
    )(c, w_ada, b_ada.reshape(n_l, 1, n6))


def _inproj_kernel(x_ref, sc_ref, sh_ref, ct_ref, st_ref,
                   wlat_ref, wdq_ref, wdk_ref, wdv_ref, wga_ref, wgb_ref,
                   qg_ref, kvg_ref, wq_ref, wkn_ref, wv_ref, esel_ref,
                   qm_ref, km_ref, vt_ref, dq1_ref, dq2_ref, dk_ref, dvt_ref, ga_ref, gb_ref):
    x = x_ref[0]
    h = (x * (1.0 + sc_ref[0]) + sh_ref[0]).astype(BF16)
    half = MLA_ROPE // 2
    ct = ct_ref[0]
    st = st_ref[0]
    lane = lax.broadcasted_iota(jnp.int32, st.shape, 1)
    sa = jnp.where(lane < MLA_NOPE + half, st, 0.0)
    sb = st - sa

    def rotate(v, c, s_first, s_second):
        n = v.shape[1]
        return v * c + pltpu.roll(v, n - half, axis=1) * s_first + pltpu.roll(v, half, axis=1) * s_second

    lat = _dot(h, wlat_ref[...])
    q_lat = lat[:, :MLA_Q_LORA]
    kv_lat = lat[:, MLA_Q_LORA:MLA_Q_LORA + MLA_KV_LORA]
    kr = lat[:, MLA_Q_LORA + MLA_KV_LORA:]

    qn = _rms_rows(q_lat, qg_ref[...]).astype(BF16)
    q = _dot(qn, wq_ref[...])
    tile8 = lambda t: jnp.concatenate([t] * MLA_HEADS, axis=1)
    qr = (rotate(q, tile8(ct), tile8(sa), tile8(sb)) * MLA_QSCALE).astype(BF16)
    for hd in range(MLA_HEADS):
        qm_ref[0, hd] = qr[:, hd * HEAD_PAD:(hd + 1) * HEAD_PAD]

    kvn = _rms_rows(kv_lat, kvg_ref[...]).astype(BF16)
    kn = _dot(kvn, wkn_ref[...])
    to_front = lambda t: pltpu.roll(t, HEAD_PAD - MLA_NOPE, axis=1)
    kro = rotate(kr, to_front(ct), to_front(sa), to_front(sb)).astype(BF16)
    kcat = (kn + _dot(kro, esel_ref[...])).astype(BF16)
    for hd in range(MLA_HEADS):
        km_ref[0, hd] = kcat[:, hd * HEAD_PAD:(hd + 1) * HEAD_PAD]
    v = _dot(kvn, wv_ref[...])
    vt = v.T.astype(BF16)
    ones = jnp.ones((ONES_ROWS, vt.shape[1]), BF16)
    for hd in range(MLA_HEADS):
        vt_ref[0, hd, :MLA_V, :] = vt[hd * MLA_V:(hd + 1) * MLA_V, :]
        vt_ref[0, hd, MLA_V:, :] = ones

    dq = _dot(h, wdq_ref[...]) * DIFF_QSCALE
    lane = lax.broadcasted_iota(jnp.int32, dq.shape, 1)
    first = (lane % HEAD_PAD) < DIFF_HD
    dq1 = jnp.where(first, dq, 0.0).astype(BF16)
    dq2 = jnp.where(first, 0.0, dq).astype(BF16)
    dk = _dot(h, wdk_ref[...]).astype(BF16)
    dvt = _dot(h, wdv_ref[...]).T.astype(BF16)
    for hd in range(DIFF_HEADS):
        sl = slice(hd * HEAD_PAD, (hd + 1) * HEAD_PAD)
        dq1_ref[0, hd] = dq1[:, sl]
        dq2_ref[0, hd] = dq2[:, sl]
        dk_ref[0, hd] = dk[:, sl]
        dvt_ref[0, hd, :DIFF_VD, :] = dvt[sl, :]
        dvt_ref[0, hd, DIFF_VD:, :] = ones

    ga_ref[0] = _sigmoid(_dot(h, wga_ref[...])).astype(BF16)
    gb_ref[0] = _sigmoid(_dot(h, wgb_ref[...])).astype(BF16)


def _inproj(x, sc1, sh1, ct, st, w):
    b, s, d = x.shape
    ts = min(TS_PROJ, s)
    hh = MLA_HEADS
    row = lambda i, j: (i, j, 0)
    bat = lambda i, j: (i, 0, 0)
    head_rows = pl.BlockSpec((1, hh, ts, HEAD_PAD), lambda i, j: (i, 0, j, 0))
    weights = [w["wlat"], w["wdq"], w["wdk"], w["wdv"], w["wga"], w["wgb"],
               w["qg"], w["kvg"], w["wq"], w["wkn"], w["wv"], w["esel"]]
    head_shape = jax.ShapeDtypeStruct((b, hh, s, HEAD_PAD), BF16)
    return pl.pallas_call(
        _inproj_kernel,
        grid=(b, s // ts),
        in_specs=[
            pl.BlockSpec((1, ts, d), row),
            pl.BlockSpec((1, 1, d), bat),
            pl.BlockSpec((1, 1, d), bat),
            pl.BlockSpec((1, ts, HEAD_PAD), row),
            pl.BlockSpec((1, ts, HEAD_PAD), row),
        ] + [_const_spec(a.shape) for a in weights],
        out_specs=[
            head_rows, head_rows,
            pl.BlockSpec((1, hh, MLA_V + ONES_ROWS, ts), lambda i, j: (i, 0, 0, j)),
            head_rows, head_rows, head_rows,
            pl.BlockSpec((1, hh, DIFF_VD + ONES_ROWS, ts), lambda i, j: (i, 0, 0, j)),
            pl.BlockSpec((1, ts, d), row),
            pl.BlockSpec((1, ts, d), row),
        ],
        out_shape=[
            head_shape, head_shape,
            jax.ShapeDtypeStruct((b, hh, MLA_V + ONES_ROWS, s), BF16),
            head_shape, head_shape, head_shape,
            jax.ShapeDtypeStruct((b, hh, DIFF_VD + ONES_ROWS, s), BF16),
            jax.ShapeDtypeStruct((b, s, d), BF16),
            jax.ShapeDtypeStruct((b, s, d), BF16),
        ],
        compiler_params=_cparams(2),
        name="inproj",
    )(x, sc1, sh1, ct, st, *weights)


def _attn_kernel(lam_init, mq_ref, mk_ref, mvt_ref, q1_ref, q2_ref, k_ref, vt_ref, pk_ref, pq_ref,
                 slope_ref, lq1_ref, lk1_ref, lq2_ref, lk2_ref, g_ref, mo_ref, o_ref,
                 macc_ref, acc_ref, dist_ref, za_ref, zb_ref, ya_ref, yb_ref):
    dist_ref[...] = jnp.abs(pk_ref[0] - pq_ref[0])
    lam = (jnp.exp(jnp.sum(lq1_ref[...] * lk1_ref[...], axis=1, keepdims=True))
           - jnp.exp(jnp.sum(lq2_ref[...] * lk2_ref[...], axis=1, keepdims=True))
           + lam_init)

    def diff_scores(hd, z_ref):
        k = k_ref[0, hd]
        bias = slope_ref[hd] * dist_ref[...]
        z1 = _dot_nt(k, q1_ref[0, hd]) - bias
        z2 = _dot_nt(k, q2_ref[0, hd]) - bias
        z_ref[0] = z1
        z_ref[1] = z2
        return jnp.max(z1, axis=0, keepdims=True), jnp.max(z2, axis=0, keepdims=True)

    def diff_values(hd, z_ref, m1, m2):
        e1 = jnp.exp2(z_ref[0] - m1).astype(BF16)
        e2 = jnp.exp2(z_ref[1] - m2).astype(BF16)
        o1 = _dot(vt_ref[0, hd], e1)
        o2 = _dot(vt_ref[0, hd], e2)
        r1 = 1.0 / o1[DIFF_VD:DIFF_VD + 1]
        r2 = lam / o2[DIFF_VD:DIFF_VD + 1]
        o_t = o1[:DIFF_VD] * r1 - o2[:DIFF_VD] * r2
        ms = jnp.mean(o_t * o_t, axis=0, keepdims=True)
        o_t = o_t * lax.rsqrt(ms + RMS_EPS) * g_ref[hd] * (1.0 - lam_init)
        acc_ref[pl.ds(pl.multiple_of(hd * DIFF_VD, DIFF_VD), DIFF_VD), :] = o_t

    def mla_scores(hd, y_ref):
        s_t = _dot_nt(mk_ref[0, hd], mq_ref[0, hd])
        y_ref[...] = s_t
        return jnp.max(s_t, axis=0, keepdims=True)

    def mla_values(hd, y_ref, m):
        p = jnp.exp2(y_ref[...] - m).astype(BF16)
        o_t = _dot(mvt_ref[0, hd], p)
        macc_ref[pl.ds(pl.multiple_of(hd * MLA_V, MLA_V), MLA_V), :] = o_t[:MLA_V] / o_t[MLA_V:MLA_V + 1]

    def scores(hd, z_ref, y_ref):
        return diff_scores(hd, z_ref), mla_scores(hd, y_ref)

    def values(hd, z_ref, y_ref, m):
        diff_values(hd, z_ref, *m[0])
        mla_values(hd, y_ref, m[1])

    def body(i, ma):
        h0 = 2 * i
        mb = scores(h0 + 1, zb_ref, yb_ref)
        values(h0, za_ref, ya_ref, ma)
        ma = scores(h0 + 2, za_ref, ya_ref)
        values(h0 + 1, zb_ref, yb_ref, mb)
        return ma

    last = DIFF_HEADS - 1
    ma = lax.fori_loop(0, DIFF_HEADS // 2 - 1, body, scores(0, za_ref, ya_ref))
    mb = scores(last, zb_ref, yb_ref)
    values(last - 1, za_ref, ya_ref, ma)
    values(last, zb_ref, yb_ref, mb)
    o_ref[0] = acc_ref[...].T.astype(BF16)
    mo_ref[0] = macc_ref[...].T.astype(BF16)


def _attention(qm, km, vt, dq1, dq2, dk, dvt, pos_k, pos_q, slopes, lq1, lk1, lq2, lk2, gcol, lam_init):
    assert MLA_HEADS == DIFF_HEADS
    b, hh, s, _ = dk.shape
    tq = min(TQ_ATTN, s)
    head_q = pl.BlockSpec((1, hh, tq, HEAD_PAD), lambda i, j: (i, 0, j, 0))
    head_k = pl.BlockSpec((1, hh, s, HEAD_PAD), lambda i, j: (i, 0, 0, 0))
    vec = pl.BlockSpec((1, DIFF_HD), lambda i, j: (0, 0))
    return pl.pallas_call(
        functools.partial(_attn_kernel, lam_init),
        grid=(b, s // tq),
        in_specs=[
            head_q, head_k,
            pl.BlockSpec((1, hh, MLA_V + ONES_ROWS, s), lambda i, j: (i, 0, 0, 0)),
            head_q, head_q, head_k,
            pl.BlockSpec((1, hh, DIFF_VD + ONES_ROWS, s), lambda i, j: (i, 0, 0, 0)),
            pl.BlockSpec((1, s, 1), lambda i, j: (i, 0, 0)),
            pl.BlockSpec((1, 1, tq), lambda i, j: (i, 0, j)),
            pl.BlockSpec(memory_space=pltpu.SMEM),
            vec, vec, vec, vec,
            pl.BlockSpec((hh, DIFF_VD, 1), lambda i, j: (0, 0, 0)),
        ],
        out_specs=[pl.BlockSpec((1, tq, hh * MLA_V), lambda i, j: (i, j, 0)),
                   pl.BlockSpec((1, tq, hh * DIFF_VD), lambda i, j: (i, j, 0))],
        out_shape=[jax.ShapeDtypeStruct((b, s, hh * MLA_V), BF16),
                   jax.ShapeDtypeStruct((b, s, hh * DIFF_VD), BF16)],
        scratch_shapes=[pltpu.VMEM((hh * MLA_V, tq), F32), pltpu.VMEM((hh * DIFF_VD, tq), F32),
                        pltpu.VMEM((s, tq), F32),
                        pltpu.VMEM((2, s, tq), F32), pltpu.VMEM((2, s, tq), F32),
                        pltpu.VMEM((s, tq), F32), pltpu.VMEM((s, tq), F32)],
        compiler_params=pltpu.CompilerParams(dimension_semantics=("arbitrary", "arbitrary"),
                                             vmem_limit_bytes=VMEM_LIMIT_ATTN),
        name="attention",
    )(qm, km, vt, dq1, dq2, dk, dvt, pos_k, pos_q, slopes, lq1, lk1, lq2, lk2, gcol)


def _mix_kernel(alpha, mo_ref, do_ref, ga_ref, gb_ref, x_ref, g1_ref, lng_ref, lnb_ref,
                wbm_ref, wbd_ref, wo_ref, o_ref):
    ya = _dot(mo_ref[0], wbm_ref[...])
    yb = _dot(do_ref[0], wbd_ref[...])
    gated = (ga_ref[0].astype(F32) * ya + gb_ref[0].astype(F32) * yb).astype(BF16)
    mix = _dot(gated, wo_ref[...])
    r = alpha * x_ref[0] + g1_ref[0] * mix
    o_ref[0] = _layernorm(r, lng_ref[...], lnb_ref[...])


def _mix(alpha, mla_o, diff_o, ga, gb, x, g1, lng, lnb, wbm, wbd, wo):
    b, s, d = x.shape
    ts = min(TS_MIX, s)
    row = lambda i, j: (i, j, 0)
    return pl.pallas_call(
        functools.partial(_mix_kernel, alpha),
        grid=(b, s // ts),
        in_specs=[
            pl.BlockSpec((1, ts, mla_o.shape[-1]), row),
            pl.BlockSpec((1, ts, diff_o.shape[-1]), row),
            pl.BlockSpec((1, ts, d), row),
            pl.BlockSpec((1, ts, d), row),
            pl.BlockSpec((1, ts, d), row),
            pl.BlockSpec((1, 1, d), lambda i, j: (i, 0, 0)),
            _const_spec(lng.shape), _const_spec(lnb.shape),
            _const_spec(wbm.shape), _const_spec(wbd.shape), _const_spec(wo.shape),
        ],
        out_specs=pl.BlockSpec((1, ts, d), row),
        out_shape=jax.ShapeDtypeStruct((b, s, d), F32),
        compiler_params=_cparams(2),
        name="mix_ln1",
    )(mla_o, diff_o, ga, gb, x, g1, lng, lnb, wbm, wbd, wo)


def _swiglu(h, w1, w3, w2):
    a = _dot(h, w1)
    bgate = _dot(h, w3)
    u = (a * _sigmoid(a) * bgate).astype(BF16)
    return _dot(u, w2)


def _ffn_kernel(alpha, x_ref, sc_ref, sh_ref, g2_ref, lng_ref, lnb_ref, w1_ref, w3_ref, w2_ref, o_ref):
    x = x_ref[0]
    h = (x * (1.0 + sc_ref[0]) + sh_ref[0]).astype(BF16)
    f = _swiglu(h, w1_ref[...], w3_ref[...], w2_ref[...])
    r = alpha * x + g2_ref[0] * f
    o_ref[0] = _layernorm(r, lng_ref[...], lnb_ref[...])


def _ffn(alpha, x, sc2, sh2, g2, lng, lnb, w1, w3, w2):
    b, s, d = x.shape
    ts = min(TS_FFN, s)
    row = lambda i, j: (i, j, 0)
    bat = lambda i, j: (i, 0, 0)
    return pl.pallas_call(
        functools.partial(_ffn_kernel, alpha),
        grid=(b, s // ts),
        in_specs=[
            pl.BlockSpec((1, ts, d), row),
            pl.BlockSpec((1, 1, d), bat), pl.BlockSpec((1, 1, d), bat), pl.BlockSpec((1, 1, d), bat),
            _const_spec(lng.shape), _const_spec(lnb.shape),
            _const_spec(w1.shape), _const_spec(w3.shape), _const_spec(w2.shape),
        ],
        out_specs=pl.BlockSpec((1, ts, d), row),
        out_shape=jax.ShapeDtypeStruct((b, s, d), F32),
        compiler_params=_cparams(2),
        name="ffn_ln2",
    )(x, sc2, sh2, g2, lng, lnb, w1, w3, w2)


def _router_kernel(x_ref, sc_ref, sh_ref, rw_ref, rb_ref, h_ref, ei_ref, gi_ref):
    h = x_ref[0] * (1.0 + sc_ref[0]) + sh_ref[0]
    _store_row_tiles(h_ref, h)
    logits = _dot(h.astype(BF16), rw_ref[...]) + rb_ref[...]
    lane = lax.broadcasted_iota(jnp.int32, logits.shape, 1)
    lane_f = lane.astype(F32)
    neg = jnp.float32(-jnp.inf)
    lg = jnp.where(lane < N_EXPERTS, logits, neg)
    m1 = jnp.max(lg, axis=1, keepdims=True)
    i1 = jnp.min(jnp.where(lg == m1, lane_f, 128.0), axis=1, keepdims=True)
    lg2 = jnp.where(lane_f == i1, neg, lg)
    m2 = jnp.max(lg2, axis=1, keepdims=True)
    i2 = jnp.min(jnp.where(lg2 == m2, lane_f, 128.0), axis=1, keepdims=True)
    t = jnp.exp(m2 - m1)
    den = 1.0 + t
    ei_ref[0] = jnp.where(lane == 0, i1, jnp.where(lane == 1, i2, 0.0)).astype(jnp.int32)
    gi_ref[0] = jnp.where(lane == 0, 1.0 / den, jnp.where(lane == 1, t / den, 0.0))


def _router(x, sc2, sh2, rw, rb):
    b, s, d = x.shape
    ts = min(TS_ROUTE, s)
    nj = s // ts
    row = lambda i, j: (i, j, 0)
    bat = lambda i, j: (i, 0, 0)
    return pl.pallas_call(
        _router_kernel,
        grid=(b, s // ts),
        in_specs=[
            pl.BlockSpec((1, ts, d), row),
            pl.BlockSpec((1, 1, d), bat), pl.BlockSpec((1, 1, d), bat),
            _const_spec(rw.shape), _const_spec(rb.shape),
        ],
        out_specs=[pl.BlockSpec((ts * ROW_TILE, LANES), lambda i, j: (i * nj + j, 0)),
                   pl.BlockSpec((1, ts, HEAD_PAD), row),
                   pl.BlockSpec((1, ts, HEAD_PAD), row)],
        out_shape=[jax.ShapeDtypeStruct((b * s * ROW_TILE, LANES), F32),
                   jax.ShapeDtypeStruct((b, s, HEAD_PAD), jnp.int32),
                   jax.ShapeDtypeStruct((b, s, HEAD_PAD), F32)],
        compiler_params=_cparams(2),
        name="moe_router",
    )(x, sc2, sh2, rw, rb)


def _experts_kernel(te_ref, nu_ref, idx_hbm, h_hbm, w1_ref, w3_ref, w2_ref, y_hbm,
                    idx_smem, xbuf, ybuf, sem_idx, sem_rows, sem_out):
    i = pl.program_id(0)
    n_used = nu_ref[0]
    n_tiles = pl.num_programs(0)
    tm = xbuf.shape[1] // ROW_TILE
    slot = lax.rem(i, 2)
    n_idx = idx_smem.shape[0]

    def index_copy(tile, step):
        return pltpu.make_async_copy(idx_hbm.at[tile], idx_smem.at[pl.ds(lax.rem(step, n_idx), 1)], sem_idx)

    def row_tile(r):
        start = r * ROW_TILE
        return pl.ds(start if isinstance(r, int) else pl.multiple_of(start, ROW_TILE), ROW_TILE)

    def gather_copy(step, r, buf=None):
        row = lax.rem(step, n_idx)
        buf = lax.rem(step, 2) if buf is None else buf
        return pltpu.make_async_copy(h_hbm.at[row_tile(idx_smem[row, r])],
                                     xbuf.at[buf, row_tile(r)], sem_rows.at[buf])

    def scatter_copy(step, r, buf=None):
        row = lax.rem(step, n_idx)
        buf = lax.rem(step, 2) if buf is None else buf
        return pltpu.make_async_copy(ybuf.at[buf, row_tile(r)],
                                     y_hbm.at[row_tile(idx_smem[row, tm + r])], sem_out.at[buf])

    def wait_gather(buf):
        pltpu.make_async_copy(h_hbm.at[pl.ds(0, tm * ROW_TILE)], xbuf.at[buf], sem_rows.at[buf]).wait()

    def wait_scatter(buf):
        pltpu.make_async_copy(ybuf.at[buf], y_hbm.at[pl.ds(0, tm * ROW_TILE)], sem_out.at[buf]).wait()

    @pl.when(i == 0)
    def _():
        for tile, step in ((0, 0), (jnp.minimum(1, n_tiles - 1), 1), (n_tiles, n_idx - 1)):
            index_copy(tile, step).start()
            index_copy(tile, step).wait()
        lax.fori_loop(0, tm, lambda r, c: (gather_copy(0, r).start(), c)[1], 0)
        ybuf[1] = jnp.zeros(ybuf.shape[1:], F32)

    def step(slot):
        wait_gather(slot)
        for r in range(tm):
            gather_copy(i + 1, r, 1 - slot).start()
        nxt2 = jnp.minimum(i + 2, n_tiles - 1)
        index_copy(nxt2, i + 2).start()
        for r in range(tm):
            scatter_copy(i + n_idx - 1, r, 1 - slot).start()
        x = _load_row_tiles(xbuf, tm, (slot,))
        _store_row_tiles(ybuf, _swiglu(x.astype(BF16), w1_ref[0], w3_ref[0], w2_ref[0]), (slot,))
        wait_scatter(1 - slot)
        index_copy(nxt2, i + 2).wait()

    for s in (0, 1):
        pl.when((i < n_used) & (slot == s))(functools.partial(step, s))

    @pl.when(i + 1 == n_used)
    def _():
        lax.fori_loop(0, tm, lambda r, c: (scatter_copy(i, r).start(), c)[1], 0)
        wait_scatter(slot)
        wait_gather(1 - slot)


def _experts(tile_expert, n_used, idx, h2, w1, w3, w2):
    n_tiles, tm2 = idx.shape[0] - 1, idx.shape[-1]
    tm = tm2 // 2
    t = h2.shape[0] // ROW_TILE
    _, d, f = w1.shape
    grid_spec = pltpu.PrefetchScalarGridSpec(
        num_scalar_prefetch=2,
        grid=(n_tiles,),
        in_specs=[
            pl.BlockSpec(memory_space=pl.ANY),
            pl.BlockSpec(memory_space=pl.ANY),
            pl.BlockSpec((1, d, f), lambda i, te, nu: (te[i], 0, 0)),
            pl.BlockSpec((1, d, f), lambda i, te, nu: (te[i], 0, 0)),
            pl.BlockSpec((1, f, d), lambda i, te, nu: (te[i], 0, 0)),
        ],
        out_specs=pl.BlockSpec(memory_space=pl.ANY),
        scratch_shapes=[
            pltpu.SMEM((4, 2 * tm), jnp.int32),
            pltpu.VMEM((2, tm * ROW_TILE, LANES), F32),
            pltpu.VMEM((2, tm * ROW_TILE, LANES), F32),
            pltpu.SemaphoreType.DMA(()),
            pltpu.SemaphoreType.DMA((2,)),
            pltpu.SemaphoreType.DMA((2,)),
        ],
    )
    return pl.pallas_call(
        _experts_kernel,
        grid_spec=grid_spec,
        out_shape=jax.ShapeDtypeStruct(((TOP_K * t + tm) * ROW_TILE, LANES), F32),
        compiler_params=_cparams(1),
        name="moe_experts",
    )(tile_expert, n_used, idx, h2, w1, w3, w2)


def _combine_kernel(alpha, y0_ref, y1_ref, gi_ref, x_ref, g2_ref, lng_ref, lnb_ref, o_ref):
    gi = gi_ref[0]
    n = gi.shape[0]
    f = gi[:, 0:1] * _load_row_tiles(y0_ref, n) + gi[:, 1:2] * _load_row_tiles(y1_ref, n)
    r = alpha * x_ref[0] + g2_ref[0] * f
    o_ref[0] = _layernorm(r, lng_ref[...], lnb_ref[...])


def _combine(alpha, y, gi, x, g2, lng, lnb):
    b, s, d = x.shape
    ts = min(TS_COMB, s)
    nj = s // ts
    row = lambda i, j: (i, j, 0)
    return pl.pallas_call(
        functools.partial(_combine_kernel, alpha),
        grid=(b, nj),
        in_specs=[
            pl.BlockSpec((ts * ROW_TILE, LANES), lambda i, j: (i * nj + j, 0)),
            pl.BlockSpec((ts * ROW_TILE, LANES), lambda i, j: (b * nj + i * nj + j, 0)),
            pl.BlockSpec((1, ts, HEAD_PAD), row),
            pl.BlockSpec((1, ts, d), row),
            pl.BlockSpec((1, 1, d), lambda i, j: (i, 0, 0)),
            _const_spec(lng.shape), _const_spec(lnb.shape),
        ],
        out_specs=pl.BlockSpec((1, ts, d), row),
        out_shape=jax.ShapeDtypeStruct((b, s, d), F32),
        compiler_params=_cparams(2),
        name="moe_combine_ln2",
    )(y, y, gi, x, g2, lng, lnb)


def _moe_plan(expert_idx, tm):
    t = expert_idx.shape[0]
    n_slots = t * TOP_K
    flat = expert_idx.reshape(n_slots)
    counts = jnp.sum((flat[:, None] == jnp.arange(N_EXPERTS, dtype=jnp.int32)[None, :]).astype(jnp.int32), axis=0)
    padded = (counts + tm - 1) // tm * tm
    pends = jnp.cumsum(padded)
    pstarts = pends - padded
    starts = jnp.cumsum(counts) - counts
    n_tiles = (n_slots + N_EXPERTS * tm) // tm
    tile_expert = jnp.minimum(
        jnp.searchsorted(pends, jnp.arange(n_tiles, dtype=jnp.int32) * tm, side="right"),
        N_EXPERTS - 1).astype(jnp.int32)
    n_used = (pends[-1] // tm).astype(jnp.int32).reshape(1)
    order = jnp.argsort(flat, stable=True).astype(jnp.int32)
    n_rows = n_tiles * tm
    rows = jnp.arange(n_rows, dtype=jnp.int32)
    order_pad = jnp.concatenate([order, jnp.zeros((n_rows - n_slots,), jnp.int32)])
    slot_of_row = jnp.full((n_rows,), -1, jnp.int32)
    for e in range(N_EXPERTS):
        shifted = jnp.roll(order_pad, pstarts[e] - starts[e])
        slot_of_row = jnp.where((rows >= pstarts[e]) & (rows < pstarts[e] + counts[e]), shifted, slot_of_row)
    valid = slot_of_row >= 0
    tok = slot_of_row // TOP_K
    src = jnp.where(valid, tok, 0)
    dst = jnp.where(valid, (slot_of_row % TOP_K) * t + tok, n_slots + rows % tm)
    idx = jnp.concatenate([src.reshape(n_tiles, 1, tm), dst.reshape(n_tiles, 1, tm)], axis=-1)
    dummy = jnp.concatenate([jnp.zeros((1, 1, tm), jnp.int32),
                             (n_slots + jnp.arange(tm, dtype=jnp.int32)).reshape(1, 1, tm)], axis=-1)
    return tile_expert, n_used, jnp.concatenate([idx.astype(jnp.int32), dummy], axis=0)


def _moe(alpha, x, sc2, sh2, g2, lng, lnb, rw, rb, w1, w3, w2):
    b, s, d = x.shape
    t = b * s
    h2, ei, gi = _router(x, sc2, sh2, rw, rb)
    tm = min(TM_MOE, t)
    tile_expert, n_used, idx = _moe_plan(ei.reshape(t, HEAD_PAD)[:, :TOP_K], tm)
    y = _experts(tile_expert, n_used, idx, h2, w1, w3, w2)
    return _combine(alpha, y, gi, x, g2, lng, lnb)


def _prep_layer_weights(w_in, w_q_up, w_kv_up, q_norm_g, kv_norm_g):
    n_l, d, _ = w_in.shape
    z96 = jnp.zeros((n_l, d, HEAD_PAD - MLA_ROPE), w_in.dtype)
    o = MLA_Q_LORA + MLA_KV_LORA + MLA_ROPE
    wlat = jnp.concatenate([w_in[:, :, :o], z96], axis=-1)
    nd = DIFF_HEADS * 2 * DIFF_HD
    wdq = w_in[:, :, o:o + nd]
    wdk = w_in[:, :, o + nd:o + 2 * nd]
    wdv = w_in[:, :, o + 2 * nd:o + 3 * nd]
    wga = w_in[:, :, o + 3 * nd:o + 3 * nd + d]
    wgb = w_in[:, :, o + 3 * nd + d:o + 3 * nd + 2 * d]

    hq = MLA_NOPE + MLA_ROPE
    wq4 = w_q_up.reshape(n_l, MLA_Q_LORA, MLA_HEADS, hq)
    zq = jnp.zeros((n_l, MLA_Q_LORA, MLA_HEADS, HEAD_PAD - hq), w_q_up.dtype)
    wq = jnp.concatenate([wq4, zq], axis=-1).reshape(n_l, MLA_Q_LORA, MLA_HEADS * HEAD_PAD)

    wkv4 = w_kv_up.reshape(n_l, MLA_KV_LORA, MLA_HEADS, MLA_NOPE + MLA_V)
    wkn = jnp.concatenate([wkv4[..., :MLA_NOPE], jnp.zeros_like(wkv4[..., :HEAD_PAD - MLA_NOPE])], axis=-1)
    wkn = wkn.reshape(n_l, MLA_KV_LORA, MLA_HEADS * HEAD_PAD)
    wv = wkv4[..., MLA_NOPE:].reshape(n_l, MLA_KV_LORA, MLA_HEADS * MLA_V)

    rr = jnp.arange(HEAD_PAD)[:, None]
    cc = jnp.arange(MLA_HEADS * HEAD_PAD)[None, :]
    esel = ((rr < MLA_ROPE) & (cc % HEAD_PAD == MLA_NOPE + rr)).astype(BF16)

    cast = lambda a: a.astype(BF16)
    return dict(wlat=cast(wlat), wdq=cast(wdq), wdk=cast(wdk), wdv=cast(wdv), wga=cast(wga), wgb=cast(wgb),
                wq=cast(wq), wkn=cast(wkn), wv=cast(wv), esel=esel,
                qg=q_norm_g.reshape(n_l, 1, MLA_Q_LORA), kvg=kv_norm_g.reshape(n_l, 1, MLA_KV_LORA))


def _rope_tables(positions):
    inv_freq = ROPE_BASE ** (-jnp.arange(0, MLA_ROPE, 2, dtype=F32) / MLA_ROPE)
    ang = positions.astype(F32)[..., None] * inv_freq
    cos, sin = jnp.cos(ang), jnp.sin(ang)
    ones = jnp.ones(positions.shape + (MLA_NOPE,), F32)
    tail = HEAD_PAD - MLA_NOPE - MLA_ROPE
    ct = jnp.concatenate([ones, cos, cos, ones[..., :tail]], axis=-1)
    st = jnp.concatenate([0.0 * ones, -sin, sin, 0.0 * ones[..., :tail]], axis=-1)
    return ct, st


def kernel(x, c, positions, w_ada, b_ada, w_in, q_norm_g, w_q_up, kv_norm_g, w_kv_up, lambda_q1, lambda_k1, lambda_q2, lambda_k2, diff_norm_g, w_br_mla, w_br_diff, w_out, ln1_g, ln1_b, ln2_g, ln2_b, ffn_w1, ffn_w3, ffn_w2, router_w, router_b, moe_w1, moe_w3, moe_w2):
    b, s, d = x.shape
    depth = w_in.shape[0]
    alpha = (2.0 * depth) ** 0.25

    mod = _ada_mod(c, w_ada, b_ada)
    ct, st = _rope_tables(positions)
    posf = positions.astype(F32)
    pos_k = posf.reshape(b, s, 1)
    pos_q = posf.reshape(b, 1, s)
    slopes = (2.0 ** (-8.0 * jnp.arange(1, DIFF_HEADS + 1, dtype=F32) / DIFF_HEADS)) * LOG2E

    lw = _prep_layer_weights(w_in, w_q_up, w_kv_up, q_norm_g, kv_norm_g)
    wbm, wbd, wo = w_br_mla.astype(BF16), w_br_diff.astype(BF16), w_out.astype(BF16)
    fw1, fw3, fw2 = ffn_w1.astype(BF16), ffn_w3.astype(BF16), ffn_w2.astype(BF16)
    mw1, mw3, mw2 = moe_w1.astype(BF16), moe_w3.astype(BF16), moe_w2.astype(BF16)
    rw = jnp.pad(router_w, ((0, 0), (0, 0), (0, HEAD_PAD - N_EXPERTS))).astype(BF16)
    rb = jnp.pad(router_b, ((0, 0), (0, HEAD_PAD - N_EXPERTS))).reshape(-1, 1, HEAD_PAD)
    gcol = diff_norm_g.reshape(depth, DIFF_HEADS, DIFF_VD, 1)
    vec = lambda a, l: a[l].reshape(1, -1)

    for l in range(depth):
        sh1, sc1, g1, sh2, sc2, g2 = [m.reshape(b, 1, d) for m in jnp.split(mod[l], 6, axis=-1)]
        w_l = {k: v[l] if k != "esel" else v for k, v in lw.items()}
        qm, km, vt, dq1, dq2, dk, dvt, ga, gb = _inproj(x, sc1, sh1, ct, st, w_l)
        lam_init = 0.8 - 0.6 * math.exp(-0.3 * l)
        mla_o, diff_o = _attention(qm, km, vt, dq1, dq2, dk, dvt, pos_k, pos_q, slopes,
                                   vec(lambda_q1, l), vec(lambda_k1, l), vec(lambda_q2, l), vec(lambda_k2, l),
                                   gcol[l], lam_init)
        x = _mix(alpha, mla_o, diff_o, ga, gb, x, g1, vec(ln1_g, l), vec(ln1_b, l), wbm[l], wbd[l], wo[l])
        if l % 2 == 0:
            x = _ffn(alpha, x, sc2, sh2, g2, vec(ln2_g, l), vec(ln2_b, l), fw1[l // 2], fw3[l // 2], fw2[l // 2])
        else:
            x = _moe(alpha, x, sc2, sh2, g2, vec(ln2_g, l), vec(ln2_b, l),
                     rw[l // 2], rb[l // 2], mw1[l // 2], mw3[l // 2], mw2[l // 2])
    return x
```

```python
import functools
import math

import jax
import jax.numpy as jnp
from jax import lax
from jax.experimental import pallas as pl
from jax.experimental.pallas import tpu as pltpu

BF16 = jnp.bfloat16
F32 = jnp.float32

D_MODEL = 1024
MLA_HEADS = 8
MLA_NOPE = 64
MLA_ROPE = 32
MLA_V = 64
MLA_Q_LORA = 384
MLA_KV_LORA = 256
ROPE_BASE = 10000.0
DIFF_HEADS = 8
DIFF_HD = 64
DIFF_VD = 2 * DIFF_HD
N_EXPERTS = 8
TOP_K = 2
LN_EPS = 1e-5
RMS_EPS = 1e-6

HEAD_PAD = 128
ONES_ROWS = 16
LOG2E = 1.4426950408889634
MLA_QSCALE = (MLA_NOPE + MLA_ROPE) ** -0.5 * LOG2E
DIFF_QSCALE = DIFF_HD ** -0.5 * LOG2E

VMEM_LIMIT = 56 * 1024 * 1024
VMEM_LIMIT_ATTN = 60 * 1024 * 1024

TS_PROJ = 512
TQ_ATTN = 256
TS_MIX = 1024
TS_FFN = 512
TM_MOE = 256
TS_ROUTE = 512
TS_COMB = 512
ROW_CHUNKS = 4


def _cparams(n_axes):
    return pltpu.CompilerParams(dimension_semantics=("arbitrary",) * n_axes,
                                vmem_limit_bytes=VMEM_LIMIT)


def _const_spec(shape):
    nd = len(shape)
    return pl.BlockSpec(shape, lambda *_: (0,) * nd, pipeline_mode=pl.Buffered(1))


def _dot(a, b):
    return jnp.dot(a, b, preferred_element_type=F32)


def _dot_nt(a, b):
    return lax.dot_general(a, b, (((1,), (1,)), ((), ())), preferred_element_type=F32)


def _sigmoid(v):
    return 1.0 / (1.0 + jnp.exp(-v))


def _layernorm(r, g, b):
    mu = jnp.mean(r, axis=-1, keepdims=True)
    d = r - mu
    var = jnp.mean(d * d, axis=-1, keepdims=True)
    return d * lax.rsqrt(var + LN_EPS) * g + b


def _rms_rows(v, g):
    ms = jnp.mean(v * v, axis=-1, keepdims=True)
    return v * lax.rsqrt(ms + RMS_EPS) * g


LANES = 128
ROW_TILE = D_MODEL // LANES


def _store_row_tiles(ref, v, lead=()):
    n = v.shape[0]
    for c in range(ROW_TILE):
        ref[lead + (pl.ds(c, n, stride=ROW_TILE), slice(None))] = v[:, c * LANES:(c + 1) * LANES]


def _load_row_tiles(ref, n, lead=()):
    return jnp.concatenate(
        [ref[lead + (pl.ds(c, n, stride=ROW_TILE), slice(None))] for c in range(ROW_TILE)], axis=1)


def _ada_kernel(c_ref, w_ref, b_ref, o_ref):
    c = c_ref[...]
    cond = c * _sigmoid(c)
    o_ref[0] = _dot(cond.astype(BF16), w_ref[0].astype(BF16)) + b_ref[0]


def _ada_mod(c, w_ada, b_ada):
    n_l, d, n6 = w_ada.shape
    b = c.shape[0]
    tn = 1536
    return pl.pallas_call(
        _ada_kernel,
        grid=(n_l, n6 // tn),
        in_specs=[
            pl.BlockSpec((b, d), lambda l, j: (0, 0)),
            pl.BlockSpec((1, d, tn), lambda l, j: (l, 0, j)),
            pl.BlockSpec((1, 1, tn), lambda l, j: (l, 0, j)),
        ],
        out_specs=pl.BlockSpec((1, b, tn), lambda l, j: (l, 0, j)),
        out_shape=jax.ShapeDtypeStruct((n_l, b, n6), F32),
        compiler_params=_cparams(2),
        name="ada_mod",
    )(c, w_ada, b_ada.reshape(n_l, 1, n6))


def _inproj_kernel(x_ref, sc_ref, sh_ref, ct_ref, st_ref,
                   wlat_ref, wdq_ref, wdk_ref, wdv_ref, wga_ref, wgb_ref,
                   qg_ref, kvg_ref, wq_ref, wkn_ref, wv_ref,
                   qm_ref, km_ref, vt_ref, dq1_ref, dq2_ref, dk_ref, dvt_ref, ga_ref, gb_ref):
    x = x_ref[0]
    h = (x * (1.0 + sc_ref[0]) + sh_ref[0]).astype(BF16)
    half = MLA_ROPE // 2
    ct = ct_ref[0]
    st = st_ref[0]
    lane = lax.broadcasted_iota(jnp.int32, st.shape, 1)
    sa = jnp.where(lane < MLA_NOPE + half, st, 0.0)
    sb = st - sa

    def rotate(v, c, s_first, s_second):
        n = v.shape[1]
        return v * c + pltpu.roll(v, n - half, axis=1) * s_first + pltpu.roll(v, half, axis=1) * s_second

    lat = _dot(h, wlat_ref[...])
    q_lat = lat[:, :MLA_Q_LORA]
    kv_lat = lat[:, MLA_Q_LORA:MLA_Q_LORA + MLA_KV_LORA]
    kr = lat[:, MLA_Q_LORA + MLA_KV_LORA:]

    qn = _rms_rows(q_lat, qg_ref[...]).astype(BF16)
    q = _dot(qn, wq_ref[...])
    tile8 = lambda t: jnp.concatenate([t] * MLA_HEADS, axis=1)
    qr = (rotate(q, tile8(ct), tile8(sa), tile8(sb)) * MLA_QSCALE).astype(BF16)
    for hd in range(MLA_HEADS):
        qm_ref[0, hd] = qr[:, hd * HEAD_PAD:(hd + 1) * HEAD_PAD]

    kvn = _rms_rows(kv_lat, kvg_ref[...]).astype(BF16)
    kn = _dot(kvn, wkn_ref[...])
    kcat = (kn + tile8(rotate(kr, ct, sa, sb))).astype(BF16)
    for hd in range(MLA_HEADS):
        km_ref[0, hd] = kcat[:, hd * HEAD_PAD:(hd + 1) * HEAD_PAD]
    v = _dot(kvn, wv_ref[...])
    vt = v.T.astype(BF16)
    ones = jnp.ones((ONES_ROWS, vt.shape[1]), BF16)
    for hd in range(MLA_HEADS):
        vt_ref[0, hd, :MLA_V, :] = vt[hd * MLA_V:(hd + 1) * MLA_V, :]
        vt_ref[0, hd, MLA_V:, :] = ones

    dq = _dot(h, wdq_ref[...]) * DIFF_QSCALE
    lane = lax.broadcasted_iota(jnp.int32, dq.shape, 1)
    first = (lane % HEAD_PAD) < DIFF_HD
    dq1 = jnp.where(first, dq, 0.0).astype(BF16)
    dq2 = jnp.where(first, 0.0, dq).astype(BF16)
    dk = _dot(h, wdk_ref[...]).astype(BF16)
    dvt = _dot(h, wdv_ref[...]).T.astype(BF16)
    for hd in range(DIFF_HEADS):
        sl = slice(hd * HEAD_PAD, (hd + 1) * HEAD_PAD)
        dq1_ref[0, hd] = dq1[:, sl]
        dq2_ref[0, hd] = dq2[:, sl]
        dk_ref[0, hd] = dk[:, sl]
        dvt_ref[0, hd, :DIFF_VD, :] = dvt[sl, :]
        dvt_ref[0, hd, DIFF_VD:, :] = ones

    ga_ref[0] = _sigmoid(_dot(h, wga_ref[...])).astype(BF16)
    gb_ref[0] = _sigmoid(_dot(h, wgb_ref[...])).astype(BF16)


def _inproj(x, sc1, sh1, ct, st, w):
    b, s, d = x.shape
    ts = min(TS_PROJ, s)
    hh = MLA_HEADS
    row = lambda i, j: (i, j, 0)
    bat = lambda i, j: (i, 0, 0)
    head_rows = pl.BlockSpec((1, hh, ts, HEAD_PAD), lambda i, j: (i, 0, j, 0))
    weights = [w["wlat"], w["wdq"], w["wdk"], w["wdv"], w["wga"], w["wgb"],
               w["qg"], w["kvg"], w["wq"], w["wkn"], w["wv"]]
    head_shape = jax.ShapeDtypeStruct((b, hh, s, HEAD_PAD), BF16)
    return pl.pallas_call(
        _inproj_kernel,
        grid=(b, s // ts),
        in_specs=[
            pl.BlockSpec((1, ts, d), row),
            pl.BlockSpec((1, 1, d), bat),
            pl.BlockSpec((1, 1, d), bat),
            pl.BlockSpec((1, ts, HEAD_PAD), row),
            pl.BlockSpec((1, ts, HEAD_PAD), row),
        ] + [_const_spec(a.shape) for a in weights],
        out_specs=[
            head_rows, head_rows,
            pl.BlockSpec((1, hh, MLA_V + ONES_ROWS, ts), lambda i, j: (i, 0, 0, j)),
            head_rows, head_rows, head_rows,
            pl.BlockSpec((1, hh, DIFF_VD + ONES_ROWS, ts), lambda i, j: (i, 0, 0, j)),
            pl.BlockSpec((1, ts, d), row),
            pl.BlockSpec((1, ts, d), row),
        ],
        out_shape=[
            head_shape, head_shape,
            jax.ShapeDtypeStruct((b, hh, MLA_V + ONES_ROWS, s), BF16),
            head_shape, head_shape, head_shape,
            jax.ShapeDtypeStruct((b, hh, DIFF_VD + ONES_ROWS, s), BF16),
            jax.ShapeDtypeStruct((b, s, d), BF16),
            jax.ShapeDtypeStruct((b, s, d), BF16),
        ],
        compiler_params=_cparams(2),
        name="inproj",
    )(x, sc1, sh1, ct, st, *weights)


def _attn_kernel(lam_init, mq_ref, mk_ref, mvt_ref, q1_ref, q2_ref, k_ref, vt_ref, pk_ref, pq_ref,
                 slope_ref, lq1_ref, lk1_ref, lq2_ref, lk2_ref, g_ref, mo_ref, o_ref,
                 macc_ref, acc_ref, dist_ref, za_ref, zb_ref, ya_ref, yb_ref):
    dist_ref[...] = jnp.abs(pk_ref[0] - pq_ref[0])
    lam = (jnp.exp(jnp.sum(lq1_ref[...] * lk1_ref[...], axis=1, keepdims=True))
           - jnp.exp(jnp.sum(lq2_ref[...] * lk2_ref[...], axis=1, keepdims=True))
           + lam_init)

    def diff_scores(hd, z_ref):
        k = k_ref[0, hd]
        bias = slope_ref[hd] * dist_ref[...]
        z1 = _dot_nt(k, q1_ref[0, hd]) - bias
        z2 = _dot_nt(k, q2_ref[0, hd]) - bias
        z_ref[0] = z1
        z_ref[1] = z2
        return jnp.max(z1, axis=0, keepdims=True), jnp.max(z2, axis=0, keepdims=True)

    def diff_values(hd, z_ref, m1, m2):
        e1 = jnp.exp2(z_ref[0] - m1).astype(BF16)
        e2 = jnp.exp2(z_ref[1] - m2).astype(BF16)
        o1 = _dot(vt_ref[0, hd], e1)
        o2 = _dot(vt_ref[0, hd], e2)
        r1 = 1.0 / o1[DIFF_VD:DIFF_VD + 1]
        r2 = lam / o2[DIFF_VD:DIFF_VD + 1]
        o_t = o1[:DIFF_VD] * r1 - o2[:DIFF_VD] * r2
        ms = jnp.mean(o_t * o_t, axis=0, keepdims=True)
        o_t = o_t * lax.rsqrt(ms + RMS_EPS) * g_ref[hd] * (1.0 - lam_init)
        acc_ref[pl.ds(pl.multiple_of(hd * DIFF_VD, DIFF_VD), DIFF_VD), :] = o_t

    def mla_scores(hd, y_ref):
        s_t = _dot_nt(mk_ref[0, hd], mq_ref[0, hd])
        y_ref[...] = s_t
        return jnp.max(s_t, axis=0, keepdims=True)

    def mla_values(hd, y_ref, m):
        p = jnp.exp2(y_ref[...] - m).astype(BF16)
        o_t = _dot(mvt_ref[0, hd], p)
        macc_ref[pl.ds(pl.multiple_of(hd * MLA_V, MLA_V), MLA_V), :] = o_t[:MLA_V] / o_t[MLA_V:MLA_V + 1]

    def scores(hd, z_ref, y_ref):
        return diff_scores(hd, z_ref), mla_scores(hd, y_ref)

    def values(hd, z_ref, y_ref, m):
        diff_values(hd, z_ref, *m[0])
        mla_values(hd, y_ref, m[1])

    def body(i, ma):
        h0 = 2 * i
        mb = scores(h0 + 1, zb_ref, yb_ref)
        values(h0, za_ref, ya_ref, ma)
        ma = scores(h0 + 2, za_ref, ya_ref)
        values(h0 + 1, zb_ref, yb_ref, mb)
        return ma

    last = DIFF_HEADS - 1
    ma = lax.fori_loop(0, DIFF_HEADS // 2 - 1, body, scores(0, za_ref, ya_ref))
    mb = scores(last, zb_ref, yb_ref)
    values(last - 1, za_ref, ya_ref, ma)
    values(last, zb_ref, yb_ref, mb)
    o_ref[0] = acc_ref[...].T.astype(BF16)
    mo_ref[0] = macc_ref[...].T.astype(BF16)


def _attention(qm, km, vt, dq1, dq2, dk, dvt, pos_k, pos_q, slopes, lq1, lk1, lq2, lk2, gcol, lam_init):
    assert MLA_HEADS == DIFF_HEADS
    b, hh, s, _ = dk.shape
    tq = min(TQ_ATTN, s)
    head_q = pl.BlockSpec((1, hh, tq, HEAD_PAD), lambda i, j: (i, 0, j, 0))
    head_k = pl.BlockSpec((1, hh, s, HEAD_PAD), lambda i, j: (i, 0, 0, 0))
    vec = pl.BlockSpec((1, DIFF_HD), lambda i, j: (0, 0))
    return pl.pallas_call(
        functools.partial(_attn_kernel, lam_init),
        grid=(b, s // tq),
        in_specs=[
            head_q, head_k,
            pl.BlockSpec((1, hh, MLA_V + ONES_ROWS, s), lambda i, j: (i, 0, 0, 0)),
            head_q, head_q, head_k,
            pl.BlockSpec((1, hh, DIFF_VD + ONES_ROWS, s), lambda i, j: (i, 0, 0, 0)),
            pl.BlockSpec((1, s, 1), lambda i, j: (i, 0, 0)),
            pl.BlockSpec((1, 1, tq), lambda i, j: (i, 0, j)),
            pl.BlockSpec(memory_space=pltpu.SMEM),
            vec, vec, vec, vec,
            pl.BlockSpec((hh, DIFF_VD, 1), lambda i, j: (0, 0, 0)),
        ],
        out_specs=[pl.BlockSpec((1, tq, hh * MLA_V), lambda i, j: (i, j, 0)),
                   pl.BlockSpec((1, tq, hh * DIFF_VD), lambda i, j: (i, j, 0))],
        out_shape=[jax.ShapeDtypeStruct((b, s, hh * MLA_V), BF16),
                   jax.ShapeDtypeStruct((b, s, hh * DIFF_VD), BF16)],
        scratch_shapes=[pltpu.VMEM((hh * MLA_V, tq), F32), pltpu.VMEM((hh * DIFF_VD, tq), F32),
                        pltpu.VMEM((s, tq), F32),
                        pltpu.VMEM((2, s, tq), F32), pltpu.VMEM((2, s, tq), F32),
                        pltpu.VMEM((s, tq), F32), pltpu.VMEM((s, tq), F32)],
        compiler_params=pltpu.CompilerParams(dimension_semantics=("arbitrary", "arbitrary"),
                                             vmem_limit_bytes=VMEM_LIMIT_ATTN),
        name="attention",
    )(qm, km, vt, dq1, dq2, dk, dvt, pos_k, pos_q, slopes, lq1, lk1, lq2, lk2, gcol)


def _mix_kernel(alpha, mo_ref, do_ref, ga_ref, gb_ref, x_ref, g1_ref, lng_ref, lnb_ref,
                wbm_ref, wbd_ref, wo_ref, o_ref):
    ch = x_ref.shape[1] // ROW_CHUNKS
    for c in range(ROW_CHUNKS):
        rows = pl.ds(c * ch, ch)
        ya = _dot(mo_ref[0, rows, :], wbm_ref[...])
        yb = _dot(do_ref[0, rows, :], wbd_ref[...])
        gated = (ga_ref[0, rows, :].astype(F32) * ya + gb_ref[0, rows, :].astype(F32) * yb).astype(BF16)
        mix = _dot(gated, wo_ref[...])
        r = alpha * x_ref[0, rows, :] + g1_ref[0] * mix
        o_ref[0, rows, :] = _layernorm(r, lng_ref[...], lnb_ref[...])


def _mix(alpha, mla_o, diff_o, ga, gb, x, g1, lng, lnb, wbm, wbd, wo):
    b, s, d = x.shape
    ts = min(TS_MIX, s)
    row = lambda i, j: (i, j, 0)
    return pl.pallas_call(
        functools.partial(_mix_kernel, alpha),
        grid=(b, s // ts),
        in_specs=[
            pl.BlockSpec((1, ts, mla_o.shape[-1]), row),
            pl.BlockSpec((1, ts, diff_o.shape[-1]), row),
            pl.BlockSpec((1, ts, d), row),
            pl.BlockSpec((1, ts, d), row),
            pl.BlockSpec((1, ts, d), row),
            pl.BlockSpec((1, 1, d), lambda i, j: (i, 0, 0)),
            _const_spec(lng.shape), _const_spec(lnb.shape),
            _const_spec(wbm.shape), _const_spec(wbd.shape), _const_spec(wo.shape),
        ],
        out_specs=pl.BlockSpec((1, ts, d), row),
        out_shape=jax.ShapeDtypeStruct((b, s, d), F32),
        compiler_params=_cparams(2),
        name="mix_ln1",
    )(mla_o, diff_o, ga, gb, x, g1, lng, lnb, wbm, wbd, wo)


def _swiglu(h, w1, w3, w2):
    a = _dot(h, w1)
    bgate = _dot(h, w3)
    u = (a * _sigmoid(a) * bgate).astype(BF16)
    return _dot(u, w2)


def _ffn_kernel(alpha, x_ref, sc_ref, sh_ref, g2_ref, lng_ref, lnb_ref, w1_ref, w3_ref, w2_ref, o_ref):
    x = x_ref[0]
    h = (x * (1.0 + sc_ref[0]) + sh_ref[0]).astype(BF16)
    f = _swiglu(h, w1_ref[...], w3_ref[...], w2_ref[...])
    r = alpha * x + g2_ref[0] * f
    o_ref[0] = _layernorm(r, lng_ref[...], lnb_ref[...])


def _ffn(alpha, x, sc2, sh2, g2, lng, lnb, w1, w3, w2):
    b, s, d = x.shape
    ts = min(TS_FFN, s)
    row = lambda i, j: (i, j, 0)
    bat = lambda i, j: (i, 0, 0)
    return pl.pallas_call(
        functools.partial(_ffn_kernel, alpha),
        grid=(b, s // ts),
        in_specs=[
            pl.BlockSpec((1, ts, d), row),
            pl.BlockSpec((1, 1, d), bat), pl.BlockSpec((1, 1, d), bat), pl.BlockSpec((1, 1, d), bat),
            _const_spec(lng.shape), _const_spec(lnb.shape),
            _const_spec(w1.shape), _const_spec(w3.shape), _const_spec(w2.shape),
        ],
        out_specs=pl.BlockSpec((1, ts, d), row),
        out_shape=jax.ShapeDtypeStruct((b, s, d), F32),
        compiler_params=_cparams(2),
        name="ffn_ln2",
    )(x, sc2, sh2, g2, lng, lnb, w1, w3, w2)


def _router_kernel(x_ref, sc_ref, sh_ref, rw_ref, rb_ref, h_ref, ei_ref, gi_ref):
    h = x_ref[0] * (1.0 + sc_ref[0]) + sh_ref[0]
    _store_row_tiles(h_ref, h)
    logits = _dot(h.astype(BF16), rw_ref[...]) + rb_ref[...]
    lane = lax.broadcasted_iota(jnp.int32, logits.shape, 1)
    lane_f = lane.astype(F32)
    neg = jnp.float32(-jnp.inf)
    lg = jnp.where(lane < N_EXPERTS, logits, neg)
    m1 = jnp.max(lg, axis=1, keepdims=True)
    i1 = jnp.min(jnp.where(lg == m1, lane_f, 128.0), axis=1, keepdims=True)
    lg2 = jnp.where(lane_f == i1, neg, lg)
    m2 = jnp.max(lg2, axis=1, keepdims=True)
    i2 = jnp.min(jnp.where(lg2 == m2, lane_f, 128.0), axis=1, keepdims=True)
    t = jnp.exp(m2 - m1)
    den = 1.0 + t
    ei_ref[0] = jnp.where(lane == 0, i1, jnp.where(lane == 1, i2, 0.0)).astype(jnp.int32)
    gi_ref[0] = jnp.where(lane == 0, 1.0 / den, jnp.where(lane == 1, t / den, 0.0))


def _router(x, sc2, sh2, rw, rb):
    b, s, d = x.shape
    ts = min(TS_ROUTE, s)
    nj = s // ts
    row = lambda i, j: (i, j, 0)
    bat = lambda i, j: (i, 0, 0)
    return pl.pallas_call(
        _router_kernel,
        grid=(b, s // ts),
        in_specs=[
            pl.BlockSpec((1, ts, d), row),
            pl.BlockSpec((1, 1, d), bat), pl.BlockSpec((1, 1, d), bat),
            _const_spec(rw.shape), _const_spec(rb.shape),
        ],
        out_specs=[pl.BlockSpec((ts * ROW_TILE, LANES), lambda i, j: (i * nj + j, 0)),
                   pl.BlockSpec((1, ts, HEAD_PAD), row),
                   pl.BlockSpec((1, ts, HEAD_PAD), row)],
        out_shape=[jax.ShapeDtypeStruct((b * s * ROW_TILE, LANES), F32),
                   jax.ShapeDtypeStruct((b, s, HEAD_PAD), jnp.int32),
                   jax.ShapeDtypeStruct((b, s, HEAD_PAD), F32)],
        compiler_params=_cparams(2),
        name="moe_router",
    )(x, sc2, sh2, rw, rb)


def _experts_kernel(te_ref, nu_ref, idx_hbm, h_hbm, w1_ref, w3_ref, w2_ref, y_hbm,
                    idx_smem, xbuf, ybuf, sem_idx, sem_rows, sem_out):
    i = pl.program_id(0)
    n_used = nu_ref[0]
    n_tiles = pl.num_programs(0)
    tm = xbuf.shape[1] // ROW_TILE
    slot = lax.rem(i, 2)
    n_idx = idx_smem.shape[0]

    def index_copy(tile, step):
        return pltpu.make_async_copy(idx_hbm.at[tile], idx_smem.at[pl.ds(lax.rem(step, n_idx), 1)], sem_idx)

    def row_tile(r):
        start = r * ROW_TILE
        return pl.ds(start if isinstance(r, int) else pl.multiple_of(start, ROW_TILE), ROW_TILE)

    def gather_copy(step, r, buf=None):
        row = lax.rem(step, n_idx)
        buf = lax.rem(step, 2) if buf is None else buf
        return pltpu.make_async_copy(h_hbm.at[row_tile(idx_smem[row, r])],
                                     xbuf.at[buf, row_tile(r)], sem_rows.at[buf])

    def scatter_copy(step, r, buf=None):
        row = lax.rem(step, n_idx)
        buf = lax.rem(step, 2) if buf is None else buf
        return pltpu.make_async_copy(ybuf.at[buf, row_tile(r)],
                                     y_hbm.at[row_tile(idx_smem[row, tm + r])], sem_out.at[buf])

    def wait_gather(buf):
        pltpu.make_async_copy(h_hbm.at[pl.ds(0, tm * ROW_TILE)], xbuf.at[buf], sem_rows.at[buf]).wait()

    def wait_scatter(buf):
        pltpu.make_async_copy(ybuf.at[buf], y_hbm.at[pl.ds(0, tm * ROW_TILE)], sem_out.at[buf]).wait()

    @pl.when(i == 0)
    def _():
        for tile, step in ((0, 0), (jnp.minimum(1, n_tiles - 1), 1), (n_tiles, n_idx - 1)):
            index_copy(tile, step).start()
            index_copy(tile, step).wait()
        lax.fori_loop(0, tm, lambda r, c: (gather_copy(0, r).start(), c)[1], 0)
        ybuf[1] = jnp.zeros(ybuf.shape[1:], F32)

    def step(slot):
        wait_gather(slot)
        for r in range(tm):
            gather_copy(i + 1, r, 1 - slot).start()
        nxt2 = jnp.minimum(i + 2, n_tiles - 1)
        index_copy(nxt2, i + 2).start()
        for r in range(tm):
            scatter_copy(i + n_idx - 1, r, 1 - slot).start()
        x = _load_row_tiles(xbuf, tm, (slot,))
        _store_row_tiles(ybuf, _swiglu(x.astype(BF16), w1_ref[0], w3_ref[0], w2_ref[0]), (slot,))
        wait_scatter(1 - slot)
        index_copy(nxt2, i + 2).wait()

    for s in (0, 1):
        pl.when((i < n_used) & (slot == s))(functools.partial(step, s))

    @pl.when(i + 1 == n_used)
    def _():
        lax.fori_loop(0, tm, lambda r, c: (scatter_copy(i, r).start(), c)[1], 0)
        wait_scatter(slot)
        wait_gather(1 - slot)


def _experts(tile_expert, n_used, idx, h2, w1, w3, w2):
    n_tiles, tm2 = idx.shape[0] - 1, idx.shape[-1]
    tm = tm2 // 2
    t = h2.shape[0] // ROW_TILE
    _, d, f = w1.shape
    grid_spec = pltpu.PrefetchScalarGridSpec(
        num_scalar_prefetch=2,
        grid=(n_tiles,),
        in_specs=[
            pl.BlockSpec(memory_space=pl.ANY),
            pl.BlockSpec(memory_space=pl.ANY),
            pl.BlockSpec((1, d, f), lambda i, te, nu: (te[i], 0, 0)),
            pl.BlockSpec((1, d, f), lambda i, te, nu: (te[i], 0, 0)),
            pl.BlockSpec((1, f, d), lambda i, te, nu: (te[i], 0, 0)),
        ],
        out_specs=pl.BlockSpec(memory_space=pl.ANY),
        scratch_shapes=[
            pltpu.SMEM((4, 2 * tm), jnp.int32),
            pltpu.VMEM((2, tm * ROW_TILE, LANES), F32),
            pltpu.VMEM((2, tm * ROW_TILE, LANES), F32),
            pltpu.SemaphoreType.DMA(()),
            pltpu.SemaphoreType.DMA((2,)),
            pltpu.SemaphoreType.DMA((2,)),
        ],
    )
    return pl.pallas_call(
        _experts_kernel,
        grid_spec=grid_spec,
        out_shape=jax.ShapeDtypeStruct(((TOP_K * t + tm) * ROW_TILE, LANES), F32),
        compiler_params=_cparams(1),
        name="moe_experts",
    )(tile_expert, n_used, idx, h2, w1, w3, w2)


def _combine_kernel(alpha, y0_ref, y1_ref, gi_ref, x_ref, g2_ref, lng_ref, lnb_ref, o_ref):
    gi = gi_ref[0]
    n = gi.shape[0]
    f = gi[:, 0:1] * _load_row_tiles(y0_ref, n) + gi[:, 1:2] * _load_row_tiles(y1_ref, n)
    r = alpha * x_ref[0] + g2_ref[0] * f
    o_ref[0] = _layernorm(r, lng_ref[...], lnb_ref[...])


def _combine(alpha, y, gi, x, g2, lng, lnb):
    b, s, d = x.shape
    ts = min(TS_COMB, s)
    nj = s // ts
    row = lambda i, j: (i, j, 0)
    return pl.pallas_call(
        functools.partial(_combine_kernel, alpha),
        grid=(b, nj),
        in_specs=[
            pl.BlockSpec((ts * ROW_TILE, LANES), lambda i, j: (i * nj + j, 0)),
            pl.BlockSpec((ts * ROW_TILE, LANES), lambda i, j: (b * nj + i * nj + j, 0)),
            pl.BlockSpec((1, ts, HEAD_PAD), row),
            pl.BlockSpec((1, ts, d), row),
            pl.BlockSpec((1, 1, d), lambda i, j: (i, 0, 0)),
            _const_spec(lng.shape), _const_spec(lnb.shape),
        ],
        out_specs=pl.BlockSpec((1, ts, d), row),
        out_shape=jax.ShapeDtypeStruct((b, s, d), F32),
        compiler_params=_cparams(2),
        name="moe_combine_ln2",
    )(y, y, gi, x, g2, lng, lnb)


def _moe_plan(expert_idx, tm):
    t = expert_idx.shape[0]
    n_slots = t * TOP_K
    flat = expert_idx.reshape(n_slots)
    counts = jnp.sum((flat[:, None] == jnp.arange(N_EXPERTS, dtype=jnp.int32)[None, :]).astype(jnp.int32), axis=0)
    padded = (counts + tm - 1) // tm * tm
    pends = jnp.cumsum(padded)
    pstarts = pends - padded
    starts = jnp.cumsum(counts) - counts
    n_tiles = (n_slots + N_EXPERTS * tm) // tm
    tile_expert = jnp.minimum(
        jnp.searchsorted(pends, jnp.arange(n_tiles, dtype=jnp.int32) * tm, side="right"),
        N_EXPERTS - 1).astype(jnp.int32)
    n_used = (pends[-1] // tm).astype(jnp.int32).reshape(1)
    order = jnp.argsort(flat, stable=True).astype(jnp.int32)
    n_rows = n_tiles * tm
    rows = jnp.arange(n_rows, dtype=jnp.int32)
    order_pad = jnp.concatenate([order, jnp.zeros((n_rows - n_slots,), jnp.int32)])
    slot_of_row = jnp.full((n_rows,), -1, jnp.int32)
    for e in range(N_EXPERTS):
        shifted = jnp.roll(order_pad, pstarts[e] - starts[e])
        slot_of_row = jnp.where((rows >= pstarts[e]) & (rows < pstarts[e] + counts[e]), shifted, slot_of_row)
    valid = slot_of_row >= 0
    tok = slot_of_row // TOP_K
    src = jnp.where(valid, tok, 0)
    dst = jnp.where(valid, (slot_of_row % TOP_K) * t + tok, n_slots + rows % tm)
    idx = jnp.concatenate([src.reshape(n_tiles, 1, tm), dst.reshape(n_tiles, 1, tm)], axis=-1)
    dummy = jnp.concatenate([jnp.zeros((1, 1, tm), jnp.int32),
                             (n_slots + jnp.arange(tm, dtype=jnp.int32)).reshape(1, 1, tm)], axis=-1)
    return tile_expert, n_used, jnp.concatenate([idx.astype(jnp.int32), dummy], axis=0)


def _moe(alpha, x, sc2, sh2, g2, lng, lnb, rw, rb, w1, w3, w2):
    b, s, d = x.shape
    t = b * s
    h2, ei, gi = _router(x, sc2, sh2, rw, rb)
    tm = min(TM_MOE, t)
    tile_expert, n_used, idx = _moe_plan(ei.reshape(t, HEAD_PAD)[:, :TOP_K], tm)
    y = _experts(tile_expert, n_used, idx, h2, w1, w3, w2)
    return _combine(alpha, y, gi, x, g2, lng, lnb)


def _prep_layer_weights(w_in, w_q_up, w_kv_up, q_norm_g, kv_norm_g):
    n_l, d, _ = w_in.shape
    o_kr = MLA_Q_LORA + MLA_KV_LORA
    o = o_kr + MLA_ROPE
    zeros = lambda n: jnp.zeros((n_l, d, n), w_in.dtype)
    wlat = jnp.concatenate([w_in[:, :, :o_kr], zeros(MLA_NOPE), w_in[:, :, o_kr:o],
                            zeros(HEAD_PAD - MLA_NOPE - MLA_ROPE)], axis=-1)
    nd = DIFF_HEADS * 2 * DIFF_HD
    wdq = w_in[:, :, o:o + nd]
    wdk = w_in[:, :, o + nd:o + 2 * nd]
    wdv = w_in[:, :, o + 2 * nd:o + 3 * nd]
    wga = w_in[:, :, o + 3 * nd:o + 3 * nd + d]
    wgb = w_in[:, :, o + 3 * nd + d:o + 3 * nd + 2 * d]

    hq = MLA_NOPE + MLA_ROPE
    wq4 = w_q_up.reshape(n_l, MLA_Q_LORA, MLA_HEADS, hq)
    zq = jnp.zeros((n_l, MLA_Q_LORA, MLA_HEADS, HEAD_PAD - hq), w_q_up.dtype)
    wq = jnp.concatenate([wq4, zq], axis=-1).reshape(n_l, MLA_Q_LORA, MLA_HEADS * HEAD_PAD)

    wkv4 = w_kv_up.reshape(n_l, MLA_KV_LORA, MLA_HEADS, MLA_NOPE + MLA_V)
    wkn = jnp.concatenate([wkv4[..., :MLA_NOPE], jnp.zeros_like(wkv4[..., :HEAD_PAD - MLA_NOPE])], axis=-1)
    wkn = wkn.reshape(n_l, MLA_KV_LORA, MLA_HEADS * HEAD_PAD)
    wv = wkv4[..., MLA_NOPE:].reshape(n_l, MLA_KV_LORA, MLA_HEADS * MLA_V)

    cast = lambda a: a.astype(BF16)
    return dict(wlat=cast(wlat), wdq=cast(wdq), wdk=cast(wdk), wdv=cast(wdv), wga=cast(wga), wgb=cast(wgb),
                wq=cast(wq), wkn=cast(wkn), wv=cast(wv),
                qg=q_norm_g.reshape(n_l, 1, MLA_Q_LORA), kvg=kv_norm_g.reshape(n_l, 1, MLA_KV_LORA))


def _rope_tables(positions):
    inv_freq = ROPE_BASE ** (-jnp.arange(0, MLA_ROPE, 2, dtype=F32) / MLA_ROPE)
    ang = positions.astype(F32)[..., None] * inv_freq
    cos, sin = jnp.cos(ang), jnp.sin(ang)
    ones = jnp.ones(positions.shape + (MLA_NOPE,), F32)
    tail = HEAD_PAD - MLA_NOPE - MLA_ROPE
    ct = jnp.concatenate([ones, cos, cos, ones[..., :tail]], axis=-1)
    st = jnp.concatenate([0.0 * ones, -sin, sin, 0.0 * ones[..., :tail]], axis=-1)
    return ct, st


def kernel(x, c, positions, w_ada, b_ada, w_in, q_norm_g, w_q_up, kv_norm_g, w_kv_up, lambda_q1, lambda_k1, lambda_q2, lambda_k2, diff_norm_g, w_br_mla, w_br_diff, w_out, ln1_g, ln1_b, ln2_g, ln2_b, ffn_w1, ffn_w3, ffn_w2, router_w, router_b, moe_w1, moe_w3, moe_w2):
    b, s, d = x.shape
    depth = w_in.shape[0]
    alpha = (2.0 * depth) ** 0.25

    mod = _ada_mod(c, w_ada, b_ada)
    ct, st = _rope_tables(positions)
    posf = positions.astype(F32)
    pos_k = posf.reshape(b, s, 1)
    pos_q = posf.reshape(b, 1, s)
    slopes = (2.0 ** (-8.0 * jnp.arange(1, DIFF_HEADS + 1, dtype=F32) / DIFF_HEADS)) * LOG2E

    lw = _prep_layer_weights(w_in, w_q_up, w_kv_up, q_norm_g, kv_norm_g)
    wbm, wbd, wo = w_br_mla.astype(BF16), w_br_diff.astype(BF16), w_out.astype(BF16)
    fw1, fw3, fw2 = ffn_w1.astype(BF16), ffn_w3.astype(BF16), ffn_w2.astype(BF16)
    mw1, mw3, mw2 = moe_w1.astype(BF16), moe_w3.astype(BF16), moe_w2.astype(BF16)
    rw = jnp.pad(router_w, ((0, 0), (0, 0), (0, HEAD_PAD - N_EXPERTS))).astype(BF16)
    rb = jnp.pad(router_b, ((0, 0), (0, HEAD_PAD - N_EXPERTS))).reshape(-1, 1, HEAD_PAD)
    gcol = diff_norm_g.reshape(depth, DIFF_HEADS, DIFF_VD, 1)
    vec = lambda a, l: a[l].reshape(1, -1)

    for l in range(depth):
        sh1, sc1, g1, sh2, sc2, g2 = [m.reshape(b, 1, d) for m in jnp.split(mod[l], 6, axis=-1)]
        w_l = {k: v[l] for k, v in lw.items()}
        qm, km, vt, dq1, dq2, dk, dvt, ga, gb = _inproj(x, sc1, sh1, ct, st, w_l)
        lam_init = 0.8 - 0.6 * math.exp(-0.3 * l)
        mla_o, diff_o = _attention(qm, km, vt, dq1, dq2, dk, dvt, pos_k, pos_q, slopes,
                                   vec(lambda_q1, l), vec(lambda_k1, l), vec(lambda_q2, l), vec(lambda_k2, l),
                                   gcol[l], lam_init)
        x = _mix(alpha, mla_o, diff_o, ga, gb, x, g1, vec(ln1_g, l), vec(ln1_b, l), wbm[l], wbd[l], wo[l])
        if l % 2 == 0:
            x = _ffn(alpha, x, sc2, sh2, g2, vec(ln2_g, l), vec(ln2_b, l), fw1[l // 2], fw3[l // 2], fw2[l // 2])
        else:
            x = _moe(alpha, x, sc2, sh2, g2, vec(ln2_g, l), vec(ln2_b, l),
                     rw[l // 2], rb[l // 2], mw1[l // 2], mw3[l // 2], mw2[l // 2])
    return x
```

```python
import functools
import math

import jax
import jax.numpy as jnp
from jax import lax
from jax.experimental import pallas as pl
from jax.experimental.pallas import tpu as pltpu

BF16 = jnp.bfloat16
F32 = jnp.float32

D_MODEL = 1024
MLA_HEADS = 8
MLA_NOPE = 64
MLA_ROPE = 32
MLA_V = 64
MLA_Q_LORA = 384
MLA_KV_LORA = 256
ROPE_BASE = 10000.0
DIFF_HEADS = 8
DIFF_HD = 64
DIFF_VD = 2 * DIFF_HD
N_EXPERTS = 8
TOP_K = 2
LN_EPS = 1e-5
RMS_EPS = 1e-6

HEAD_PAD = 128
ONES_ROWS = 16
LOG2E = 1.4426950408889634
MLA_QSCALE = (MLA_NOPE + MLA_ROPE) ** -0.5 * LOG2E
DIFF_QSCALE = DIFF_HD ** -0.5 * LOG2E

VMEM_LIMIT = 56 * 1024 * 1024
VMEM_LIMIT_ATTN = 60 * 1024 * 1024

TS_PROJ = 512
TQ_ATTN = 256
TS_MIX = 1024
TS_FFN = 512
TM_MOE = 256
TS_ROUTE = 1024
TS_COMB = 1024
ROW_CHUNKS = 4


def _cparams(n_axes):
    return pltpu.CompilerParams(dimension_semantics=("arbitrary",) * n_axes,
                                vmem_limit_bytes=VMEM_LIMIT)


def _const_spec(shape):
    nd = len(shape)
    return pl.BlockSpec(shape, lambda *_: (0,) * nd, pipeline_mode=pl.Buffered(1))


def _dot(a, b):
    return jnp.dot(a, b, preferred_element_type=F32)


def _dot_nt(a, b):
    return lax.dot_general(a, b, (((1,), (1,)), ((), ())), preferred_element_type=F32)


def _sigmoid(v):
    return 1.0 / (1.0 + jnp.exp(-v))


def _layernorm(r, g, b):
    mu = jnp.mean(r, axis=-1, keepdims=True)
    d = r - mu
    var = jnp.mean(d * d, axis=-1, keepdims=True)
    return d * lax.rsqrt(var + LN_EPS) * g + b


def _rms_rows(v, g):
    ms = jnp.mean(v * v, axis=-1, keepdims=True)
    return v * lax.rsqrt(ms + RMS_EPS) * g


LANES = 128
ROW_TILE = D_MODEL // LANES


def _store_row_tiles(ref, v, lead=()):
    n = v.shape[0]
    for c in range(ROW_TILE):
        ref[lead + (pl.ds(c, n, stride=ROW_TILE), slice(None))] = v[:, c * LANES:(c + 1) * LANES]


def _load_row_tiles(ref, n, lead=()):
    return jnp.concatenate(
        [ref[lead + (pl.ds(c, n, stride=ROW_TILE), slice(None))] for c in range(ROW_TILE)], axis=1)


def _ada_kernel(c_ref, w_ref, b_ref, o_ref):
    c = c_ref[...]
    cond = c * _sigmoid(c)
    o_ref[0] = _dot(cond.astype(BF16), w_ref[0].astype(BF16)) + b_ref[0]


def _ada_mod(c, w_ada, b_ada):
    n_l, d, n6 = w_ada.shape
    b = c.shape[0]
    tn = 1536
    return pl.pallas_call(
        _ada_kernel,
        grid=(n_l, n6 // tn),
        in_specs=[
            pl.BlockSpec((b, d), lambda l, j: (0, 0)),
            pl.BlockSpec((1, d, tn), lambda l, j: (l, 0, j)),
            pl.BlockSpec((1, 1, tn), lambda l, j: (l, 0, j)),
        ],
        out_specs=pl.BlockSpec((1, b, tn), lambda l, j: (l, 0, j)),
        out_shape=jax.ShapeDtypeStruct((n_l, b, n6), F32),
        compiler_params=_cparams(2),
        name="ada_mod",
    )(c, w_ada, b_ada.reshape(n_l, 1, n6))


def _inproj_kernel(x_ref, sc_ref, sh_ref, ct_ref, st_ref,
                   wlat_ref, wdq_ref, wdk_ref, wdv_ref, wga_ref, wgb_ref,
                   qg_ref, kvg_ref, wq_ref, wkn_ref, wv_ref, esel_ref,
                   qm_ref, km_ref, vt_ref, dq1_ref, dq2_ref, dk_ref, dvt_ref, ga_ref, gb_ref):
    x = x_ref[0]
    h = (x * (1.0 + sc_ref[0]) + sh_ref[0]).astype(BF16)
    half = MLA_ROPE // 2
    ct = ct_ref[0]
    st = st_ref[0]
    lane = lax.broadcasted_iota(jnp.int32, st.shape, 1)
    sa = jnp.where(lane < MLA_NOPE + half, st, 0.0)
    sb = st - sa

    def rotate(v, c, s_first, s_second):
        n = v.shape[1]
        return v * c + pltpu.roll(v, n - half, axis=1) * s_first + pltpu.roll(v, half, axis=1) * s_second

    lat = _dot(h, wlat_ref[...])
    q_lat = lat[:, :MLA_Q_LORA]
    kv_lat = lat[:, MLA_Q_LORA:MLA_Q_LORA + MLA_KV_LORA]
    kr = lat[:, MLA_Q_LORA + MLA_KV_LORA:]

    qn = _rms_rows(q_lat, qg_ref[...]).astype(BF16)
    q = _dot(qn, wq_ref[...])
    tile8 = lambda t: jnp.concatenate([t] * MLA_HEADS, axis=1)
    qr = (rotate(q, tile8(ct), tile8(sa), tile8(sb)) * MLA_QSCALE).astype(BF16)
    for hd in range(MLA_HEADS):
        qm_ref[0, hd] = qr[:, hd * HEAD_PAD:(hd + 1) * HEAD_PAD]

    kvn = _rms_rows(kv_lat, kvg_ref[...]).astype(BF16)
    kn = _dot(kvn, wkn_ref[...])
    to_front = lambda t: pltpu.roll(t, HEAD_PAD - MLA_NOPE, axis=1)
    kro = rotate(kr, to_front(ct), to_front(sa), to_front(sb)).astype(BF16)
    kcat = (kn + _dot(kro, esel_ref[...])).astype(BF16)
    for hd in range(MLA_HEADS):
        km_ref[0, hd] = kcat[:, hd * HEAD_PAD:(hd + 1) * HEAD_PAD]
    v = _dot(kvn, wv_ref[...])
    vt = v.T.astype(BF16)
    ones = jnp.ones((ONES_ROWS, vt.shape[1]), BF16)
    for hd in range(MLA_HEADS):
        vt_ref[0, hd, :MLA_V, :] = vt[hd * MLA_V:(hd + 1) * MLA_V, :]
        vt_ref[0, hd, MLA_V:, :] = ones

    dq = _dot(h, wdq_ref[...]) * DIFF_QSCALE
    lane = lax.broadcasted_iota(jnp.int32, dq.shape, 1)
    first = (lane % HEAD_PAD) < DIFF_HD
    dq1 = jnp.where(first, dq, 0.0).astype(BF16)
    dq2 = jnp.where(first, 0.0, dq).astype(BF16)
    dk = _dot(h, wdk_ref[...]).astype(BF16)
    dvt = _dot(h, wdv_ref[...]).T.astype(BF16)
    for hd in range(DIFF_HEADS):
        sl = slice(hd * HEAD_PAD, (hd + 1) * HEAD_PAD)
        dq1_ref[0, hd] = dq1[:, sl]
        dq2_ref[0, hd] = dq2[:, sl]
        dk_ref[0, hd] = dk[:, sl]
        dvt_ref[0, hd, :DIFF_VD, :] = dvt[sl, :]
        dvt_ref[0, hd, DIFF_VD:, :] = ones

    ga_ref[0] = _sigmoid(_dot(h, wga_ref[...])).astype(BF16)
    gb_ref[0] = _sigmoid(_dot(h, wgb_ref[...])).astype(BF16)


def _inproj(x, sc1, sh1, ct, st, w):
    b, s, d = x.shape
    ts = min(TS_PROJ, s)
    hh = MLA_HEADS
    row = lambda i, j: (i, j, 0)
    bat = lambda i, j: (i, 0, 0)
    head_rows = pl.BlockSpec((1, hh, ts, HEAD_PAD), lambda i, j: (i, 0, j, 0))
    weights = [w["wlat"], w["wdq"], w["wdk"], w["wdv"], w["wga"], w["wgb"],
               w["qg"], w["kvg"], w["wq"], w["wkn"], w["wv"], w["esel"]]
    head_shape = jax.ShapeDtypeStruct((b, hh, s, HEAD_PAD), BF16)
    return pl.pallas_call(
        _inproj_kernel,
        grid=(b, s // ts),
        in_specs=[
            pl.BlockSpec((1, ts, d), row),
            pl.BlockSpec((1, 1, d), bat),
            pl.BlockSpec((1, 1, d), bat),
            pl.BlockSpec((1, ts, HEAD_PAD), row),
            pl.BlockSpec((1, ts, HEAD_PAD), row),
        ] + [_const_spec(a.shape) for a in weights],
        out_specs=[
            head_rows, head_rows,
            pl.BlockSpec((1, hh, MLA_V + ONES_ROWS, ts), lambda i, j: (i, 0, 0, j)),
            head_rows, head_rows, head_rows,
            pl.BlockSpec((1, hh, DIFF_VD + ONES_ROWS, ts), lambda i, j: (i, 0, 0, j)),
            pl.BlockSpec((1, ts, d), row),
            pl.BlockSpec((1, ts, d), row),
        ],
        out_shape=[
            head_shape, head_shape,
            jax.ShapeDtypeStruct((b, hh, MLA_V + ONES_ROWS, s), BF16),
            head_shape, head_shape, head_shape,
            jax.ShapeDtypeStruct((b, hh, DIFF_VD + ONES_ROWS, s), BF16),
            jax.ShapeDtypeStruct((b, s, d), BF16),
            jax.ShapeDtypeStruct((b, s, d), BF16),
        ],
        compiler_params=_cparams(2),
        name="inproj",
    )(x, sc1, sh1, ct, st, *weights)


def _attn_kernel(lam_init, mq_ref, mk_ref, mvt_ref, q1_ref, q2_ref, k_ref, vt_ref, pk_ref, pq_ref,
                 slope_ref, lq1_ref, lk1_ref, lq2_ref, lk2_ref, g_ref, mo_ref, o_ref,
                 macc_ref, acc_ref, dist_ref, za_ref, zb_ref, ya_ref, yb_ref):
    dist_ref[...] = jnp.abs(pk_ref[0] - pq_ref[0])
    lam = (jnp.exp(jnp.sum(lq1_ref[...] * lk1_ref[...], axis=1, keepdims=True))
           - jnp.exp(jnp.sum(lq2_ref[...] * lk2_ref[...], axis=1, keepdims=True))
           + lam_init)

    def diff_scores(hd, z_ref):
        k = k_ref[0, hd]
        bias = slope_ref[hd] * dist_ref[...]
        z1 = _dot_nt(k, q1_ref[0, hd]) - bias
        z2 = _dot_nt(k, q2_ref[0, hd]) - bias
        z_ref[0] = z1
        z_ref[1] = z2
        return jnp.max(z1, axis=0, keepdims=True), jnp.max(z2, axis=0, keepdims=True)

    def diff_values(hd, z_ref, m1, m2):
        e1 = jnp.exp2(z_ref[0] - m1).astype(BF16)
        e2 = jnp.exp2(z_ref[1] - m2).astype(BF16)
        o1 = _dot(vt_ref[0, hd], e1)
        o2 = _dot(vt_ref[0, hd], e2)
        r1 = 1.0 / o1[DIFF_VD:DIFF_VD + 1]
        r2 = lam / o2[DIFF_VD:DIFF_VD + 1]
        o_t = o1[:DIFF_VD] * r1 - o2[:DIFF_VD] * r2
        ms = jnp.mean(o_t * o_t, axis=0, keepdims=True)
        o_t = o_t * lax.rsqrt(ms + RMS_EPS) * g_ref[hd] * (1.0 - lam_init)
        acc_ref[pl.ds(pl.multiple_of(hd * DIFF_VD, DIFF_VD), DIFF_VD), :] = o_t

    def mla_scores(hd, y_ref):
        s_t = _dot_nt(mk_ref[0, hd], mq_ref[0, hd])
        y_ref[...] = s_t
        return jnp.max(s_t, axis=0, keepdims=True)

    def mla_values(hd, y_ref, m):
        p = jnp.exp2(y_ref[...] - m).astype(BF16)
        o_t = _dot(mvt_ref[0, hd], p)
        macc_ref[pl.ds(pl.multiple_of(hd * MLA_V, MLA_V), MLA_V), :] = o_t[:MLA_V] / o_t[MLA_V:MLA_V + 1]

    def scores(hd, z_ref, y_ref):
        return diff_scores(hd, z_ref), mla_scores(hd, y_ref)

    def values(hd, z_ref, y_ref, m):
        diff_values(hd, z_ref, *m[0])
        mla_values(hd, y_ref, m[1])

    def body(i, ma):
        h0 = 2 * i
        mb = scores(h0 + 1, zb_ref, yb_ref)
        values(h0, za_ref, ya_ref, ma)
        ma = scores(h0 + 2, za_ref, ya_ref)
        values(h0 + 1, zb_ref, yb_ref, mb)
        return ma

    last = DIFF_HEADS - 1
    ma = lax.fori_loop(0, DIFF_HEADS // 2 - 1, body, scores(0, za_ref, ya_ref))
    mb = scores(last, zb_ref, yb_ref)
    values(last - 1, za_ref, ya_ref, ma)
    values(last, zb_ref, yb_ref, mb)
    o_ref[0] = acc_ref[...].T.astype(BF16)
    mo_ref[0] = macc_ref[...].T.astype(BF16)


def _attention(qm, km, vt, dq1, dq2, dk, dvt, pos_k, pos_q, slopes, lq1, lk1, lq2, lk2, gcol, lam_init):
    assert MLA_HEADS == DIFF_HEADS
    b, hh, s, _ = dk.shape
    tq = min(TQ_ATTN, s)
    head_q = pl.BlockSpec((1, hh, tq, HEAD_PAD), lambda i, j: (i, 0, j, 0))
    head_k = pl.BlockSpec((1, hh, s, HEAD_PAD), lambda i, j: (i, 0, 0, 0))
    vec = pl.BlockSpec((1, DIFF_HD), lambda i, j: (0, 0))
    return pl.pallas_call(
        functools.partial(_attn_kernel, lam_init),
        grid=(b, s // tq),
        in_specs=[
            head_q, head_k,
            pl.BlockSpec((1, hh, MLA_V + ONES_ROWS, s), lambda i, j: (i, 0, 0, 0)),
            head_q, head_q, head_k,
            pl.BlockSpec((1, hh, DIFF_VD + ONES_ROWS, s), lambda i, j: (i, 0, 0, 0)),
            pl.BlockSpec((1, s, 1), lambda i, j: (i, 0, 0)),
            pl.BlockSpec((1, 1, tq), lambda i, j: (i, 0, j)),
            pl.BlockSpec(memory_space=pltpu.SMEM),
            vec, vec, vec, vec,
            pl.BlockSpec((hh, DIFF_VD, 1), lambda i, j: (0, 0, 0)),
        ],
        out_specs=[pl.BlockSpec((1, tq, hh * MLA_V), lambda i, j: (i, j, 0)),
                   pl.BlockSpec((1, tq, hh * DIFF_VD), lambda i, j: (i, j, 0))],
        out_shape=[jax.ShapeDtypeStruct((b, s, hh * MLA_V), BF16),
                   jax.ShapeDtypeStruct((b, s, hh * DIFF_VD), BF16)],
        scratch_shapes=[pltpu.VMEM((hh * MLA_V, tq), F32), pltpu.VMEM((hh * DIFF_VD, tq), F32),
                        pltpu.VMEM((s, tq), F32),
                        pltpu.VMEM((2, s, tq), F32), pltpu.VMEM((2, s, tq), F32),
                        pltpu.VMEM((s, tq), F32), pltpu.VMEM((s, tq), F32)],
        compiler_params=pltpu.CompilerParams(dimension_semantics=("arbitrary", "arbitrary"),
                                             vmem_limit_bytes=VMEM_LIMIT_ATTN),
        name="attention",
    )(qm, km, vt, dq1, dq2, dk, dvt, pos_k, pos_q, slopes, lq1, lk1, lq2, lk2, gcol)


def _mix_kernel(alpha, mo_ref, do_ref, ga_ref, gb_ref, x_ref, g1_ref, lng_ref, lnb_ref,
                wbm_ref, wbd_ref, wo_ref, o_ref):
    ch = x_ref.shape[1] // ROW_CHUNKS
    for c in range(ROW_CHUNKS):
        rows = pl.ds(c * ch, ch)
        ya = _dot(mo_ref[0, rows, :], wbm_ref[...])
        yb = _dot(do_ref[0, rows, :], wbd_ref[...])
        gated = (ga_ref[0, rows, :].astype(F32) * ya + gb_ref[0, rows, :].astype(F32) * yb).astype(BF16)
        mix = _dot(gated, wo_ref[...])
        r = alpha * x_ref[0, rows, :] + g1_ref[0] * mix
        o_ref[0, rows, :] = _layernorm(r, lng_ref[...], lnb_ref[...])


def _mix(alpha, mla_o, diff_o, ga, gb, x, g1, lng, lnb, wbm, wbd, wo):
    b, s, d = x.shape
    ts = min(TS_MIX, s)
    row = lambda i, j: (i, j, 0)
    return pl.pallas_call(
        functools.partial(_mix_kernel, alpha),
        grid=(b, s // ts),
        in_specs=[
            pl.BlockSpec((1, ts, mla_o.shape[-1]), row),
            pl.BlockSpec((1, ts, diff_o.shape[-1]), row),
            pl.BlockSpec((1, ts, d), row),
            pl.BlockSpec((1, ts, d), row),
            pl.BlockSpec((1, ts, d), row),
            pl.BlockSpec((1, 1, d), lambda i, j: (i, 0, 0)),
            _const_spec(lng.shape), _const_spec(lnb.shape),
            _const_spec(wbm.shape), _const_spec(wbd.shape), _const_spec(wo.shape),
        ],
        out_specs=pl.BlockSpec((1, ts, d), row),
        out_shape=jax.ShapeDtypeStruct((b, s, d), F32),
        compiler_params=_cparams(2),
        name="mix_ln1",
    )(mla_o, diff_o, ga, gb, x, g1, lng, lnb, wbm, wbd, wo)


def _swiglu(h, w1, w3, w2):
    a = _dot(h, w1)
    bgate = _dot(h, w3)
    u = (a * _sigmoid(a) * bgate).astype(BF16)
    return _dot(u, w2)


def _ffn_kernel(alpha, x_ref, sc_ref, sh_ref, g2_ref, lng_ref, lnb_ref, w1_ref, w3_ref, w2_ref, o_ref):
    x = x_ref[0]
    h = (x * (1.0 + sc_ref[0]) + sh_ref[0]).astype(BF16)
    f = _swiglu(h, w1_ref[...], w3_ref[...], w2_ref[...])
    r = alpha * x + g2_ref[0] * f
    o_ref[0] = _layernorm(r, lng_ref[...], lnb_ref[...])


def _ffn(alpha, x, sc2, sh2, g2, lng, lnb, w1, w3, w2):
    b, s, d = x.shape
    ts = min(TS_FFN, s)
    row = lambda i, j: (i, j, 0)
    bat = lambda i, j: (i, 0, 0)
    return pl.pallas_call(
        functools.partial(_ffn_kernel, alpha),
        grid=(b, s // ts),
        in_specs=[
            pl.BlockSpec((1, ts, d), row),
            pl.BlockSpec((1, 1, d), bat), pl.BlockSpec((1, 1, d), bat), pl.BlockSpec((1, 1, d), bat),
            _const_spec(lng.shape), _const_spec(lnb.shape),
            _const_spec(w1.shape), _const_spec(w3.shape), _const_spec(w2.shape),
        ],
        out_specs=pl.BlockSpec((1, ts, d), row),
        out_shape=jax.ShapeDtypeStruct((b, s, d), F32),
        compiler_params=_cparams(2),
        name="ffn_ln2",
    )(x, sc2, sh2, g2, lng, lnb, w1, w3, w2)


def _router_kernel(x_ref, sc_ref, sh_ref, rw_ref, rb_ref, h_ref, ei_ref, gi_ref):
    h = x_ref[0] * (1.0 + sc_ref[0]) + sh_ref[0]
    _store_row_tiles(h_ref, h)
    logits = _dot(h.astype(BF16), rw_ref[...]) + rb_ref[...]
    lane = lax.broadcasted_iota(jnp.int32, logits.shape, 1)
    lane_f = lane.astype(F32)
    neg = jnp.float32(-jnp.inf)
    lg = jnp.where(lane < N_EXPERTS, logits, neg)
    m1 = jnp.max(lg, axis=1, keepdims=True)
    i1 = jnp.min(jnp.where(lg == m1, lane_f, 128.0), axis=1, keepdims=True)
    lg2 = jnp.where(lane_f == i1, neg, lg)
    m2 = jnp.max(lg2, axis=1, keepdims=True)
    i2 = jnp.min(jnp.where(lg2 == m2, lane_f, 128.0), axis=1, keepdims=True)
    t = jnp.exp(m2 - m1)
    den = 1.0 + t
    ei_ref[0] = jnp.where(lane == 0, i1, jnp.where(lane == 1, i2, 0.0)).astype(jnp.int32)
    gi_ref[0] = jnp.where(lane == 0, 1.0 / den, jnp.where(lane == 1, t / den, 0.0))


def _router(x, sc2, sh2, rw, rb):
    b, s, d = x.shape
    ts = min(TS_ROUTE, s)
    nj = s // ts
    row = lambda i, j: (i, j, 0)
    bat = lambda i, j: (i, 0, 0)
    return pl.pallas_call(
        _router_kernel,
        grid=(b, s // ts),
        in_specs=[
            pl.BlockSpec((1, ts, d), row),
            pl.BlockSpec((1, 1, d), bat), pl.BlockSpec((1, 1, d), bat),
            _const_spec(rw.shape), _const_spec(rb.shape),
        ],
        out_specs=[pl.BlockSpec((ts * ROW_TILE, LANES), lambda i, j: (i * nj + j, 0)),
                   pl.BlockSpec((1, ts, HEAD_PAD), row),
                   pl.BlockSpec((1, ts, HEAD_PAD), row)],
        out_shape=[jax.ShapeDtypeStruct((b * s * ROW_TILE, LANES), F32),
                   jax.ShapeDtypeStruct((b, s, HEAD_PAD), jnp.int32),
                   jax.ShapeDtypeStruct((b, s, HEAD_PAD), F32)],
        compiler_params=_cparams(2),
        name="moe_router",
    )(x, sc2, sh2, rw, rb)


def _experts_kernel(te_ref, nu_ref, idx_hbm, h_hbm, w1_ref, w3_ref, w2_ref, y_hbm,
                    idx_smem, xbuf, ybuf, sem_idx, sem_rows, sem_out):
    i = pl.program_id(0)
    n_used = nu_ref[0]
    n_tiles = pl.num_programs(0)
    tm = xbuf.shape[1] // ROW_TILE
    slot = lax.rem(i, 2)
    n_idx = idx_smem.shape[0]

    def index_copy(tile, step):
        return pltpu.make_async_copy(idx_hbm.at[tile], idx_smem.at[pl.ds(lax.rem(step, n_idx), 1)], sem_idx)

    def row_tile(r):
        start = r * ROW_TILE
        return pl.ds(start if isinstance(r, int) else pl.multiple_of(start, ROW_TILE), ROW_TILE)

    def gather_copy(step, r, buf=None):
        row = lax.rem(step, n_idx)
        buf = lax.rem(step, 2) if buf is None else buf
        return pltpu.make_async_copy(h_hbm.at[row_tile(idx_smem[row, r])],
                                     xbuf.at[buf, row_tile(r)], sem_rows.at[buf])

    def scatter_copy(step, r, buf=None):
        row = lax.rem(step, n_idx)
        buf = lax.rem(step, 2) if buf is None else buf
        return pltpu.make_async_copy(ybuf.at[buf, row_tile(r)],
                                     y_hbm.at[row_tile(idx_smem[row, tm + r])], sem_out.at[buf])

    def wait_gather(buf):
        pltpu.make_async_copy(h_hbm.at[pl.ds(0, tm * ROW_TILE)], xbuf.at[buf], sem_rows.at[buf]).wait()

    def wait_scatter(buf):
        pltpu.make_async_copy(ybuf.at[buf], y_hbm.at[pl.ds(0, tm * ROW_TILE)], sem_out.at[buf]).wait()

    @pl.when(i == 0)
    def _():
        for tile, step in ((0, 0), (jnp.minimum(1, n_tiles - 1), 1), (n_tiles, n_idx - 1)):
            index_copy(tile, step).start()
            index_copy(tile, step).wait()
        lax.fori_loop(0, tm, lambda r, c: (gather_copy(0, r).start(), c)[1], 0)
        ybuf[1] = jnp.zeros(ybuf.shape[1:], F32)

    def step(slot):
        wait_gather(slot)
        for r in range(tm):
            gather_copy(i + 1, r, 1 - slot).start()
        nxt2 = jnp.minimum(i + 2, n_tiles - 1)
        index_copy(nxt2, i + 2).start()
        for r in range(tm):
            scatter_copy(i + n_idx - 1, r, 1 - slot).start()
        x = _load_row_tiles(xbuf, tm, (slot,))
        _store_row_tiles(ybuf, _swiglu(x.astype(BF16), w1_ref[0], w3_ref[0], w2_ref[0]), (slot,))
        wait_scatter(1 - slot)
        index_copy(nxt2, i + 2).wait()

    for s in (0, 1):
        pl.when((i < n_used) & (slot == s))(functools.partial(step, s))

    @pl.when(i + 1 == n_used)
    def _():
        lax.fori_loop(0, tm, lambda r, c: (scatter_copy(i, r).start(), c)[1], 0)
        wait_scatter(slot)
        wait_gather(1 - slot)


def _experts(tile_expert, n_used, idx, h2, w1, w3, w2):
    n_tiles, tm2 = idx.shape[0] - 1, idx.shape[-1]
    tm = tm2 // 2
    t = h2.shape[0] // ROW_TILE
    _, d, f = w1.shape
    grid_spec = pltpu.PrefetchScalarGridSpec(
        num_scalar_prefetch=2,
        grid=(n_tiles,),
        in_specs=[
            pl.BlockSpec(memory_space=pl.ANY),
            pl.BlockSpec(memory_space=pl.ANY),
            pl.BlockSpec((1, d, f), lambda i, te, nu: (te[i], 0, 0)),
            pl.BlockSpec((1, d, f), lambda i, te, nu: (te[i], 0, 0)),
            pl.BlockSpec((1, f, d), lambda i, te, nu: (te[i], 0, 0)),
        ],
        out_specs=pl.BlockSpec(memory_space=pl.ANY),
        scratch_shapes=[
            pltpu.SMEM((4, 2 * tm), jnp.int32),
            pltpu.VMEM((2, tm * ROW_TILE, LANES), F32),
            pltpu.VMEM((2, tm * ROW_TILE, LANES), F32),
            pltpu.SemaphoreType.DMA(()),
            pltpu.SemaphoreType.DMA((2,)),
            pltpu.SemaphoreType.DMA((2,)),
        ],
    )
    return pl.pallas_call(
        _experts_kernel,
        grid_spec=grid_spec,
        out_shape=jax.ShapeDtypeStruct(((TOP_K * t + tm) * ROW_TILE, LANES), F32),
        compiler_params=_cparams(1),
        name="moe_experts",
    )(tile_expert, n_used, idx, h2, w1, w3, w2)


def _combine_kernel(alpha, y0_ref, y1_ref, gi_ref, x_ref, g2_ref, lng_ref, lnb_ref, o_ref):
    gi = gi_ref[0]
    n = gi.shape[0]
    f = gi[:, 0:1] * _load_row_tiles(y0_ref, n) + gi[:, 1:2] * _load_row_tiles(y1_ref, n)
    r = alpha * x_ref[0] + g2_ref[0] * f
    o_ref[0] = _layernorm(r, lng_ref[...], lnb_ref[...])


def _combine(alpha, y, gi, x, g2, lng, lnb):
    b, s, d = x.shape
    ts = min(TS_COMB, s)
    nj = s // ts
    row = lambda i, j: (i, j, 0)
    return pl.pallas_call(
        functools.partial(_combine_kernel, alpha),
        grid=(b, nj),
        in_specs=[
            pl.BlockSpec((ts * ROW_TILE, LANES), lambda i, j: (i * nj + j, 0)),
            pl.BlockSpec((ts * ROW_TILE, LANES), lambda i, j: (b * nj + i * nj + j, 0)),
            pl.BlockSpec((1, ts, HEAD_PAD), row),
            pl.BlockSpec((1, ts, d), row),
            pl.BlockSpec((1, 1, d), lambda i, j: (i, 0, 0)),
            _const_spec(lng.shape), _const_spec(lnb.shape),
        ],
        out_specs=pl.BlockSpec((1, ts, d), row),
        out_shape=jax.ShapeDtypeStruct((b, s, d), F32),
        compiler_params=_cparams(2),
        name="moe_combine_ln2",
    )(y, y, gi, x, g2, lng, lnb)


def _moe_plan(expert_idx, tm):
    t = expert_idx.shape[0]
    n_slots = t * TOP_K
    flat = expert_idx.reshape(n_slots)
    counts = jnp.sum((flat[:, None] == jnp.arange(N_EXPERTS, dtype=jnp.int32)[None, :]).astype(jnp.int32), axis=0)
    padded = (counts + tm - 1) // tm * tm
    pends = jnp.cumsum(padded)
    pstarts = pends - padded
    starts = jnp.cumsum(counts) - counts
    n_tiles = (n_slots + N_EXPERTS * tm) // tm
    tile_start = jnp.arange(n_tiles, dtype=jnp.int32) * tm
    tile_expert = jnp.minimum(jnp.sum((tile_start[:, None] >= pends[None, :]).astype(jnp.int32), axis=1),
                              N_EXPERTS - 1)
    n_used = (pends[-1] // tm).astype(jnp.int32).reshape(1)
    order = jnp.argsort(flat, stable=True).astype(jnp.int32)
    n_rows = n_tiles * tm
    rows = jnp.arange(n_rows, dtype=jnp.int32)
    order_pad = jnp.concatenate([order, jnp.zeros((n_rows - n_slots,), jnp.int32)])
    slot_of_row = jnp.full((n_rows,), -1, jnp.int32)
    for e in range(N_EXPERTS):
        shifted = jnp.roll(order_pad, pstarts[e] - starts[e])
        slot_of_row = jnp.where((rows >= pstarts[e]) & (rows < pstarts[e] + counts[e]), shifted, slot_of_row)
    valid = slot_of_row >= 0
    tok = slot_of_row // TOP_K
    src = jnp.where(valid, tok, 0)
    dst = jnp.where(valid, (slot_of_row % TOP_K) * t + tok, n_slots + rows % tm)
    idx = jnp.concatenate([src.reshape(n_tiles, 1, tm), dst.reshape(n_tiles, 1, tm)], axis=-1)
    dummy = jnp.concatenate([jnp.zeros((1, 1, tm), jnp.int32),
                             (n_slots + jnp.arange(tm, dtype=jnp.int32)).reshape(1, 1, tm)], axis=-1)
    return tile_expert, n_used, jnp.concatenate([idx.astype(jnp.int32), dummy], axis=0)


def _moe(alpha, x, sc2, sh2, g2, lng, lnb, rw, rb, w1, w3, w2):
    b, s, d = x.shape
    t = b * s
    h2, ei, gi = _router(x, sc2, sh2, rw, rb)
    tm = min(TM_MOE, t)
    tile_expert, n_used, idx = _moe_plan(ei.reshape(t, HEAD_PAD)[:, :TOP_K], tm)
    y = _experts(tile_expert, n_used, idx, h2, w1, w3, w2)
    return _combine(alpha, y, gi, x, g2, lng, lnb)


def _prep_layer_weights(w_in, w_q_up, w_kv_up, q_norm_g, kv_norm_g):
    n_l, d, _ = w_in.shape
    z96 = jnp.zeros((n_l, d, HEAD_PAD - MLA_ROPE), w_in.dtype)
    o = MLA_Q_LORA + MLA_KV_LORA + MLA_ROPE
    wlat = jnp.concatenate([w_in[:, :, :o], z96], axis=-1)
    nd = DIFF_HEADS * 2 * DIFF_HD
    wdq = w_in[:, :, o:o + nd]
    wdk = w_in[:, :, o + nd:o + 2 * nd]
    wdv = w_in[:, :, o + 2 * nd:o + 3 * nd]
    wga = w_in[:, :, o + 3 * nd:o + 3 * nd + d]
    wgb = w_in[:, :, o + 3 * nd + d:o + 3 * nd + 2 * d]

    hq = MLA_NOPE + MLA_ROPE
    wq4 = w_q_up.reshape(n_l, MLA_Q_LORA, MLA_HEADS, hq)
    zq = jnp.zeros((n_l, MLA_Q_LORA, MLA_HEADS, HEAD_PAD - hq), w_q_up.dtype)
    wq = jnp.concatenate([wq4, zq], axis=-1).reshape(n_l, MLA_Q_LORA, MLA_HEADS * HEAD_PAD)

    wkv4 = w_kv_up.reshape(n_l, MLA_KV_LORA, MLA_HEADS, MLA_NOPE + MLA_V)
    wkn = jnp.concatenate([wkv4[..., :MLA_NOPE], jnp.zeros_like(wkv4[..., :HEAD_PAD - MLA_NOPE])], axis=-1)
    wkn = wkn.reshape(n_l, MLA_KV_LORA, MLA_HEADS * HEAD_PAD)
    wv = wkv4[..., MLA_NOPE:].reshape(n_l, MLA_KV_LORA, MLA_HEADS * MLA_V)

    rr = jnp.arange(HEAD_PAD)[:, None]
    cc = jnp.arange(MLA_HEADS * HEAD_PAD)[None, :]
    esel = ((rr < MLA_ROPE) & (cc % HEAD_PAD == MLA_NOPE + rr)).astype(BF16)

    cast = lambda a: a.astype(BF16)
    return dict(wlat=cast(wlat), wdq=cast(wdq), wdk=cast(wdk), wdv=cast(wdv), wga=cast(wga), wgb=cast(wgb),
                wq=cast(wq), wkn=cast(wkn), wv=cast(wv), esel=esel,
                qg=q_norm_g.reshape(n_l, 1, MLA_Q_LORA), kvg=kv_norm_g.reshape(n_l, 1, MLA_KV_LORA))


def _rope_tables(positions):
    inv_freq = ROPE_BASE ** (-jnp.arange(0, MLA_ROPE, 2, dtype=F32) / MLA_ROPE)
    ang = positions.astype(F32)[..., None] * inv_freq
    cos, sin = jnp.cos(ang), jnp.sin(ang)
    ones = jnp.ones(positions.shape + (MLA_NOPE,), F32)
    tail = HEAD_PAD - MLA_NOPE - MLA_ROPE
    ct = jnp.concatenate([ones, cos, cos, ones[..., :tail]], axis=-1)
    st = jnp.concatenate([0.0 * ones, -sin, sin, 0.0 * ones[..., :tail]], axis=-1)
    return ct, st


def kernel(x, c, positions, w_ada, b_ada, w_in, q_norm_g, w_q_up, kv_norm_g, w_kv_up, lambda_q1, lambda_k1, lambda_q2, lambda_k2, diff_norm_g, w_br_mla, w_br_diff, w_out, ln1_g, ln1_b, ln2_g, ln2_b, ffn_w1, ffn_w3, ffn_w2, router_w, router_b, moe_w1, moe_w3, moe_w2):
    b, s, d = x.shape
    depth = w_in.shape[0]
    alpha = (2.0 * depth) ** 0.25

    mod = _ada_mod(c, w_ada, b_ada)
    ct, st = _rope_tables(positions)
    posf = positions.astype(F32)
    pos_k = posf.reshape(b, s, 1)
    pos_q = posf.reshape(b, 1, s)
    slopes = (2.0 ** (-8.0 * jnp.arange(1, DIFF_HEADS + 1, dtype=F32) / DIFF_HEADS)) * LOG2E

    lw = _prep_layer_weights(w_in, w_q_up, w_kv_up, q_norm_g, kv_norm_g)
    wbm, wbd, wo = w_br_mla.astype(BF16), w_br_diff.astype(BF16), w_out.astype(BF16)
    fw1, fw3, fw2 = ffn_w1.astype(BF16), ffn_w3.astype(BF16), ffn_w2.astype(BF16)
    mw1, mw3, mw2 = moe_w1.astype(BF16), moe_w3.astype(BF16), moe_w2.astype(BF16)
    rw = jnp.pad(router_w, ((0, 0), (0, 0), (0, HEAD_PAD - N_EXPERTS))).astype(BF16)
    rb = jnp.pad(router_b, ((0, 0), (0, HEAD_PAD - N_EXPERTS))).reshape(-1, 1, HEAD_PAD)
    gcol = diff_norm_g.reshape(depth, DIFF_HEADS, DIFF_VD, 1)
    vec = lambda a, l: a[l].reshape(1, -1)

    for l in range(depth):
        sh1, sc1, g1, sh2, sc2, g2 = [m.reshape(b, 1, d) for m in jnp.split(mod[l], 6, axis=-1)]
        w_l = {k: v[l] if k != "esel" else v for k, v in lw.items()}
        qm, km, vt, dq1, dq2, dk, dvt, ga, gb = _inproj(x, sc1, sh1, ct, st, w_l)
        lam_init = 0.8 - 0.6 * math.exp(-0.3 * l)
        mla_o, diff_o = _attention(qm, km, vt, dq1, dq2, dk, dvt, pos_k, pos_q, slopes,
                                   vec(lambda_q1, l), vec(lambda_k1, l), vec(lambda_q2, l), vec(lambda_k2, l),
                                   gcol[l], lam_init)
        x = _mix(alpha, mla_o, diff_o, ga, gb, x, g1, vec(ln1_g, l), vec(ln1_b, l), wbm[l], wbd[l], wo[l])
        if l % 2 == 0:
            x = _ffn(alpha, x, sc2, sh2, g2, vec(ln2_g, l), vec(ln2_b, l), fw1[l // 2], fw3[l // 2], fw2[l // 2])
        else:
            x = _moe(alpha, x, sc2, sh2, g2, vec(ln2_g, l), vec(ln2_b, l),
                     rw[l // 2], rb[l // 2], mw1[l // 2], mw3[l // 2], mw2[l // 2])
    return x
```

```python
import functools
import math

import jax
import jax.numpy as jnp
from jax import lax
from jax.experimental import pallas as pl
from jax.experimental.pallas import tpu as pltpu

BF16 = jnp.bfloat16
F32 = jnp.float32

D_MODEL = 1024
MLA_HEADS = 8
MLA_NOPE = 64
MLA_ROPE = 32
MLA_V = 64
MLA_Q_LORA = 384
MLA_KV_LORA = 256
ROPE_BASE = 10000.0
DIFF_HEADS = 8
DIFF_HD = 64
DIFF_VD = 2 * DIFF_HD
N_EXPERTS = 8
TOP_K = 2
LN_EPS = 1e-5
RMS_EPS = 1e-6

HEAD_PAD = 128
ONES_ROWS = 16
LOG2E = 1.4426950408889634
MLA_QSCALE = (MLA_NOPE + MLA_ROPE) ** -0.5 * LOG2E
DIFF_QSCALE = DIFF_HD ** -0.5 * LOG2E

VMEM_LIMIT = 56 * 1024 * 1024
VMEM_LIMIT_ATTN = 60 * 1024 * 1024

TS_PROJ = 512
TQ_ATTN = 256
TS_MIX = 1024
TS_FFN = 512
TM_MOE = 256
TS_COMB = 1024
ROW_CHUNKS = 4


def _cparams(n_axes):
    return pltpu.CompilerParams(dimension_semantics=("arbitrary",) * n_axes,
                                vmem_limit_bytes=VMEM_LIMIT)


def _const_spec(shape):
    nd = len(shape)
    return pl.BlockSpec(shape, lambda *_: (0,) * nd, pipeline_mode=pl.Buffered(1))


def _dot(a, b):
    return jnp.dot(a, b, preferred_element_type=F32)


def _dot_nt(a, b):
    return lax.dot_general(a, b, (((1,), (1,)), ((), ())), preferred_element_type=F32)


def _sigmoid(v):
    return 1.0 / (1.0 + jnp.exp(-v))


def _layernorm(r, g, b):
    mu = jnp.mean(r, axis=-1, keepdims=True)
    d = r - mu
    var = jnp.mean(d * d, axis=-1, keepdims=True)
    return d * lax.rsqrt(var + LN_EPS) * g + b


def _rms_rows(v, g):
    ms = jnp.mean(v * v, axis=-1, keepdims=True)
    return v * lax.rsqrt(ms + RMS_EPS) * g


LANES = 128
ROW_TILE = D_MODEL // LANES


def _store_row_tiles(ref, v, lead=(), row0=0):
    n = v.shape[0]
    for c in range(ROW_TILE):
        ref[lead + (pl.ds(row0 * ROW_TILE + c, n, stride=ROW_TILE), slice(None))] = v[:, c * LANES:(c + 1) * LANES]


def _load_row_tiles(ref, n, lead=()):
    return jnp.concatenate(
        [ref[lead + (pl.ds(c, n, stride=ROW_TILE), slice(None))] for c in range(ROW_TILE)], axis=1)


def _ada_kernel(c_ref, w_ref, b_ref, o_ref):
    c = c_ref[...]
    cond = c * _sigmoid(c)
    o_ref[0] = _dot(cond.astype(BF16), w_ref[0].astype(BF16)) + b_ref[0]


def _ada_mod(c, w_ada, b_ada):
    n_l, d, n6 = w_ada.shape
    b = c.shape[0]
    tn = 1536
    return pl.pallas_call(
        _ada_kernel,
        grid=(n_l, n6 // tn),
        in_specs=[
            pl.BlockSpec((b, d), lambda l, j: (0, 0)),
            pl.BlockSpec((1, d, tn), lambda l, j: (l, 0, j)),
            pl.BlockSpec((1, 1, tn), lambda l, j: (l, 0, j)),
        ],
        out_specs=pl.BlockSpec((1, b, tn), lambda l, j: (l, 0, j)),
        out_shape=jax.ShapeDtypeStruct((n_l, b, n6), F32),
        compiler_params=_cparams(2),
        name="ada_mod",
    )(c, w_ada, b_ada.reshape(n_l, 1, n6))


def _inproj_kernel(x_ref, sc_ref, sh_ref, ct_ref, st_ref,
                   wlat_ref, wdq_ref, wdk_ref, wdv_ref, wga_ref, wgb_ref,
                   qg_ref, kvg_ref, wq_ref, wkn_ref, wv_ref, esel_ref,
                   qm_ref, km_ref, vt_ref, dq1_ref, dq2_ref, dk_ref, dvt_ref, ga_ref, gb_ref):
    x = x_ref[0]
    h = (x * (1.0 + sc_ref[0]) + sh_ref[0]).astype(BF16)
    half = MLA_ROPE // 2
    ct = ct_ref[0]
    st = st_ref[0]
    lane = lax.broadcasted_iota(jnp.int32, st.shape, 1)
    sa = jnp.where(lane < MLA_NOPE + half, st, 0.0)
    sb = st - sa

    def rotate(v, c, s_first, s_second):
        n = v.shape[1]
        return v * c + pltpu.roll(v, n - half, axis=1) * s_first + pltpu.roll(v, half, axis=1) * s_second

    lat = _dot(h, wlat_ref[...])
    q_lat = lat[:, :MLA_Q_LORA]
    kv_lat = lat[:, MLA_Q_LORA:MLA_Q_LORA + MLA_KV_LORA]
    kr = lat[:, MLA_Q_LORA + MLA_KV_LORA:]

    qn = _rms_rows(q_lat, qg_ref[...]).astype(BF16)
    q = _dot(qn, wq_ref[...])
    tile8 = lambda t: jnp.concatenate([t] * MLA_HEADS, axis=1)
    qr = (rotate(q, tile8(ct), tile8(sa), tile8(sb)) * MLA_QSCALE).astype(BF16)
    for hd in range(MLA_HEADS):
        qm_ref[0, hd] = qr[:, hd * HEAD_PAD:(hd + 1) * HEAD_PAD]

    kvn = _rms_rows(kv_lat, kvg_ref[...]).astype(BF16)
    kn = _dot(kvn, wkn_ref[...])
    to_front = lambda t: pltpu.roll(t, HEAD_PAD - MLA_NOPE, axis=1)
    kro = rotate(kr, to_front(ct), to_front(sa), to_front(sb)).astype(BF16)
    kcat = (kn + _dot(kro, esel_ref[...])).astype(BF16)
    for hd in range(MLA_HEADS):
        km_ref[0, hd] = kcat[:, hd * HEAD_PAD:(hd + 1) * HEAD_PAD]
    v = _dot(kvn, wv_ref[...])
    vt = v.T.astype(BF16)
    ones = jnp.ones((ONES_ROWS, vt.shape[1]), BF16)
    for hd in range(MLA_HEADS):
        vt_ref[0, hd, :MLA_V, :] = vt[hd * MLA_V:(hd + 1) * MLA_V, :]
        vt_ref[0, hd, MLA_V:, :] = ones

    dq = _dot(h, wdq_ref[...]) * DIFF_QSCALE
    lane = lax.broadcasted_iota(jnp.int32, dq.shape, 1)
    first = (lane % HEAD_PAD) < DIFF_HD
    dq1 = jnp.where(first, dq, 0.0).astype(BF16)
    dq2 = jnp.where(first, 0.0, dq).astype(BF16)
    dk = _dot(h, wdk_ref[...]).astype(BF16)
    dvt = _dot(h, wdv_ref[...]).T.astype(BF16)
    for hd in range(DIFF_HEADS):
        sl = slice(hd * HEAD_PAD, (hd + 1) * HEAD_PAD)
        dq1_ref[0, hd] = dq1[:, sl]
        dq2_ref[0, hd] = dq2[:, sl]
        dk_ref[0, hd] = dk[:, sl]
        dvt_ref[0, hd, :DIFF_VD, :] = dvt[sl, :]
        dvt_ref[0, hd, DIFF_VD:, :] = ones

    ga_ref[0] = _sigmoid(_dot(h, wga_ref[...])).astype(BF16)
    gb_ref[0] = _sigmoid(_dot(h, wgb_ref[...])).astype(BF16)


def _inproj(x, sc1, sh1, ct, st, w):
    b, s, d = x.shape
    ts = min(TS_PROJ, s)
    hh = MLA_HEADS
    row = lambda i, j: (i, j, 0)
    bat = lambda i, j: (i, 0, 0)
    head_rows = pl.BlockSpec((1, hh, ts, HEAD_PAD), lambda i, j: (i, 0, j, 0))
    weights = [w["wlat"], w["wdq"], w["wdk"], w["wdv"], w["wga"], w["wgb"],
               w["qg"], w["kvg"], w["wq"], w["wkn"], w["wv"], w["esel"]]
    head_shape = jax.ShapeDtypeStruct((b, hh, s, HEAD_PAD), BF16)
    return pl.pallas_call(
        _inproj_kernel,
        grid=(b, s // ts),
        in_specs=[
            pl.BlockSpec((1, ts, d), row),
            pl.BlockSpec((1, 1, d), bat),
            pl.BlockSpec((1, 1, d), bat),
            pl.BlockSpec((1, ts, HEAD_PAD), row),
            pl.BlockSpec((1, ts, HEAD_PAD), row),
        ] + [_const_spec(a.shape) for a in weights],
        out_specs=[
            head_rows, head_rows,
            pl.BlockSpec((1, hh, MLA_V + ONES_ROWS, ts), lambda i, j: (i, 0, 0, j)),
            head_rows, head_rows, head_rows,
            pl.BlockSpec((1, hh, DIFF_VD + ONES_ROWS, ts), lambda i, j: (i, 0, 0, j)),
            pl.BlockSpec((1, ts, d), row),
            pl.BlockSpec((1, ts, d), row),
        ],
        out_shape=[
            head_shape, head_shape,
            jax.ShapeDtypeStruct((b, hh, MLA_V + ONES_ROWS, s), BF16),
            head_shape, head_shape, head_shape,
            jax.ShapeDtypeStruct((b, hh, DIFF_VD + ONES_ROWS, s), BF16),
            jax.ShapeDtypeStruct((b, s, d), BF16),
            jax.ShapeDtypeStruct((b, s, d), BF16),
        ],
        compiler_params=_cparams(2),
        name="inproj",
    )(x, sc1, sh1, ct, st, *weights)


def _attn_kernel(lam_init, mq_ref, mk_ref, mvt_ref, q1_ref, q2_ref, k_ref, vt_ref, pk_ref, pq_ref,
                 slope_ref, lq1_ref, lk1_ref, lq2_ref, lk2_ref, g_ref, mo_ref, o_ref,
                 macc_ref, acc_ref, dist_ref, za_ref, zb_ref, ya_ref, yb_ref):
    dist_ref[...] = jnp.abs(pk_ref[0] - pq_ref[0])
    lam = (jnp.exp(jnp.sum(lq1_ref[...] * lk1_ref[...], axis=1, keepdims=True))
           - jnp.exp(jnp.sum(lq2_ref[...] * lk2_ref[...], axis=1, keepdims=True))
           + lam_init)

    def diff_scores(hd, z_ref):
        k = k_ref[0, hd]
        bias = slope_ref[hd] * dist_ref[...]
        z1 = _dot_nt(k, q1_ref[0, hd]) - bias
        z2 = _dot_nt(k, q2_ref[0, hd]) - bias
        z_ref[0] = z1
        z_ref[1] = z2
        return jnp.max(z1, axis=0, keepdims=True), jnp.max(z2, axis=0, keepdims=True)

    def diff_values(hd, z_ref, m1, m2):
        e1 = jnp.exp2(z_ref[0] - m1).astype(BF16)
        e2 = jnp.exp2(z_ref[1] - m2).astype(BF16)
        o1 = _dot(vt_ref[0, hd], e1)
        o2 = _dot(vt_ref[0, hd], e2)
        r1 = 1.0 / o1[DIFF_VD:DIFF_VD + 1]
        r2 = lam / o2[DIFF_VD:DIFF_VD + 1]
        o_t = o1[:DIFF_VD] * r1 - o2[:DIFF_VD] * r2
        ms = jnp.mean(o_t * o_t, axis=0, keepdims=True)
        o_t = o_t * lax.rsqrt(ms + RMS_EPS) * g_ref[hd] * (1.0 - lam_init)
        acc_ref[pl.ds(pl.multiple_of(hd * DIFF_VD, DIFF_VD), DIFF_VD), :] = o_t

    def mla_scores(hd, y_ref):
        s_t = _dot_nt(mk_ref[0, hd], mq_ref[0, hd])
        y_ref[...] = s_t
        return jnp.max(s_t, axis=0, keepdims=True)

    def mla_values(hd, y_ref, m):
        p = jnp.exp2(y_ref[...] - m).astype(BF16)
        o_t = _dot(mvt_ref[0, hd], p)
        macc_ref[pl.ds(pl.multiple_of(hd * MLA_V, MLA_V), MLA_V), :] = o_t[:MLA_V] / o_t[MLA_V:MLA_V + 1]

    def scores(hd, z_ref, y_ref):
        return diff_scores(hd, z_ref), mla_scores(hd, y_ref)

    def values(hd, z_ref, y_ref, m):
        diff_values(hd, z_ref, *m[0])
        mla_values(hd, y_ref, m[1])

    def body(i, ma):
        h0 = 2 * i
        mb = scores(h0 + 1, zb_ref, yb_ref)
        values(h0, za_ref, ya_ref, ma)
        ma = scores(h0 + 2, za_ref, ya_ref)
        values(h0 + 1, zb_ref, yb_ref, mb)
        return ma

    last = DIFF_HEADS - 1
    ma = lax.fori_loop(0, DIFF_HEADS // 2 - 1, body, scores(0, za_ref, ya_ref))
    mb = scores(last, zb_ref, yb_ref)
    values(last - 1, za_ref, ya_ref, ma)
    values(last, zb_ref, yb_ref, mb)
    o_ref[0] = acc_ref[...].T.astype(BF16)
    mo_ref[0] = macc_ref[...].T.astype(BF16)


def _attention(qm, km, vt, dq1, dq2, dk, dvt, pos_k, pos_q, slopes, lq1, lk1, lq2, lk2, gcol, lam_init):
    assert MLA_HEADS == DIFF_HEADS
    b, hh, s, _ = dk.shape
    tq = min(TQ_ATTN, s)
    head_q = pl.BlockSpec((1, hh, tq, HEAD_PAD), lambda i, j: (i, 0, j, 0))
    head_k = pl.BlockSpec((1, hh, s, HEAD_PAD), lambda i, j: (i, 0, 0, 0))
    vec = pl.BlockSpec((1, DIFF_HD), lambda i, j: (0, 0))
    return pl.pallas_call(
        functools.partial(_attn_kernel, lam_init),
        grid=(b, s // tq),
        in_specs=[
            head_q, head_k,
            pl.BlockSpec((1, hh, MLA_V + ONES_ROWS, s), lambda i, j: (i, 0, 0, 0)),
            head_q, head_q, head_k,
            pl.BlockSpec((1, hh, DIFF_VD + ONES_ROWS, s), lambda i, j: (i, 0, 0, 0)),
            pl.BlockSpec((1, s, 1), lambda i, j: (i, 0, 0)),
            pl.BlockSpec((1, 1, tq), lambda i, j: (i, 0, j)),
            pl.BlockSpec(memory_space=pltpu.SMEM),
            vec, vec, vec, vec,
            pl.BlockSpec((hh, DIFF_VD, 1), lambda i, j: (0, 0, 0)),
        ],
        out_specs=[pl.BlockSpec((1, tq, hh * MLA_V), lambda i, j: (i, j, 0)),
                   pl.BlockSpec((1, tq, hh * DIFF_VD), lambda i, j: (i, j, 0))],
        out_shape=[jax.ShapeDtypeStruct((b, s, hh * MLA_V), BF16),
                   jax.ShapeDtypeStruct((b, s, hh * DIFF_VD), BF16)],
        scratch_shapes=[pltpu.VMEM((hh * MLA_V, tq), F32), pltpu.VMEM((hh * DIFF_VD, tq), F32),
                        pltpu.VMEM((s, tq), F32),
                        pltpu.VMEM((2, s, tq), F32), pltpu.VMEM((2, s, tq), F32),
                        pltpu.VMEM((s, tq), F32), pltpu.VMEM((s, tq), F32)],
        compiler_params=pltpu.CompilerParams(dimension_semantics=("arbitrary", "arbitrary"),
                                             vmem_limit_bytes=VMEM_LIMIT_ATTN),
        name="attention",
    )(qm, km, vt, dq1, dq2, dk, dvt, pos_k, pos_q, slopes, lq1, lk1, lq2, lk2, gcol)


def _route_top2(h, rw, rb):
    logits = _dot(h.astype(BF16), rw) + rb
    lane = lax.broadcasted_iota(jnp.int32, logits.shape, 1)
    lane_f = lane.astype(F32)
    neg = jnp.float32(-jnp.inf)
    lg = jnp.where(lane < N_EXPERTS, logits, neg)
    m1 = jnp.max(lg, axis=1, keepdims=True)
    i1 = jnp.min(jnp.where(lg == m1, lane_f, 128.0), axis=1, keepdims=True)
    lg2 = jnp.where(lane_f == i1, neg, lg)
    m2 = jnp.max(lg2, axis=1, keepdims=True)
    i2 = jnp.min(jnp.where(lg2 == m2, lane_f, 128.0), axis=1, keepdims=True)
    t = jnp.exp(m2 - m1)
    den = 1.0 + t
    ei = jnp.where(lane == 0, i1, jnp.where(lane == 1, i2, 0.0)).astype(jnp.int32)
    gi = jnp.where(lane == 0, 1.0 / den, jnp.where(lane == 1, t / den, 0.0))
    return ei, gi


def _mix_kernel(alpha, route, mo_ref, do_ref, ga_ref, gb_ref, x_ref, g1_ref, lng_ref, lnb_ref,
                wbm_ref, wbd_ref, wo_ref, *rest):
    if route:
        sc_ref, sh_ref, rw_ref, rb_ref, o_ref, h_ref, ei_ref, gi_ref = rest
    else:
        (o_ref,) = rest
    ch = x_ref.shape[1] // ROW_CHUNKS
    for c in range(ROW_CHUNKS):
        rows = pl.ds(c * ch, ch)
        ya = _dot(mo_ref[0, rows, :], wbm_ref[...])
        yb = _dot(do_ref[0, rows, :], wbd_ref[...])
        gated = (ga_ref[0, rows, :].astype(F32) * ya + gb_ref[0, rows, :].astype(F32) * yb).astype(BF16)
        mix = _dot(gated, wo_ref[...])
        r = alpha * x_ref[0, rows, :] + g1_ref[0] * mix
        x1 = _layernorm(r, lng_ref[...], lnb_ref[...])
        o_ref[0, rows, :] = x1
        if route:
            h = x1 * (1.0 + sc_ref[0]) + sh_ref[0]
            _store_row_tiles(h_ref, h, row0=c * ch)
            ei_ref[0, rows, :], gi_ref[0, rows, :] = _route_top2(h, rw_ref[...], rb_ref[...])


def _mix(alpha, mla_o, diff_o, ga, gb, x, g1, lng, lnb, wbm, wbd, wo, router=None):
    b, s, d = x.shape
    ts = min(TS_MIX, s)
    nj = s // ts
    row = lambda i, j: (i, j, 0)
    bat = lambda i, j: (i, 0, 0)
    in_specs = [
        pl.BlockSpec((1, ts, mla_o.shape[-1]), row),
        pl.BlockSpec((1, ts, diff_o.shape[-1]), row),
        pl.BlockSpec((1, ts, d), row),
        pl.BlockSpec((1, ts, d), row),
        pl.BlockSpec((1, ts, d), row),
        pl.BlockSpec((1, 1, d), bat),
        _const_spec(lng.shape), _const_spec(lnb.shape),
        _const_spec(wbm.shape), _const_spec(wbd.shape), _const_spec(wo.shape),
    ]
    out_specs = [pl.BlockSpec((1, ts, d), row)]
    out_shape = [jax.ShapeDtypeStruct((b, s, d), F32)]
    args = [mla_o, diff_o, ga, gb, x, g1, lng, lnb, wbm, wbd, wo]
    if router is not None:
        sc2, sh2, rw, rb = router
        in_specs += [pl.BlockSpec((1, 1, d), bat), pl.BlockSpec((1, 1, d), bat),
                     _const_spec(rw.shape), _const_spec(rb.shape)]
        args += [sc2, sh2, rw, rb]
        out_specs += [pl.BlockSpec((ts * ROW_TILE, LANES), lambda i, j: (i * nj + j, 0)),
                      pl.BlockSpec((1, ts, HEAD_PAD), row), pl.BlockSpec((1, ts, HEAD_PAD), row)]
        out_shape += [jax.ShapeDtypeStruct((b * s * ROW_TILE, LANES), F32),
                      jax.ShapeDtypeStruct((b, s, HEAD_PAD), jnp.int32),
                      jax.ShapeDtypeStruct((b, s, HEAD_PAD), F32)]
    return pl.pallas_call(
        functools.partial(_mix_kernel, alpha, router is not None),
        grid=(b, nj),
        in_specs=in_specs,
        out_specs=out_specs,
        out_shape=out_shape,
        compiler_params=_cparams(2),
        name="mix_ln1",
    )(*args)


def _swiglu(h, w1, w3, w2):
    a = _dot(h, w1)
    bgate = _dot(h, w3)
    u = (a * _sigmoid(a) * bgate).astype(BF16)
    return _dot(u, w2)


def _ffn_kernel(alpha, x_ref, sc_ref, sh_ref, g2_ref, lng_ref, lnb_ref, w1_ref, w3_ref, w2_ref, o_ref):
    x = x_ref[0]
    h = (x * (1.0 + sc_ref[0]) + sh_ref[0]).astype(BF16)
    f = _swiglu(h, w1_ref[...], w3_ref[...], w2_ref[...])
    r = alpha * x + g2_ref[0] * f
    o_ref[0] = _layernorm(r, lng_ref[...], lnb_ref[...])


def _ffn(alpha, x, sc2, sh2, g2, lng, lnb, w1, w3, w2):
    b, s, d = x.shape
    ts = min(TS_FFN, s)
    row = lambda i, j: (i, j, 0)
    bat = lambda i, j: (i, 0, 0)
    return pl.pallas_call(
        functools.partial(_ffn_kernel, alpha),
        grid=(b, s // ts),
        in_specs=[
            pl.BlockSpec((1, ts, d), row),
            pl.BlockSpec((1, 1, d), bat), pl.BlockSpec((1, 1, d), bat), pl.BlockSpec((1, 1, d), bat),
            _const_spec(lng.shape), _const_spec(lnb.shape),
            _const_spec(w1.shape), _const_spec(w3.shape), _const_spec(w2.shape),
        ],
        out_specs=pl.BlockSpec((1, ts, d), row),
        out_shape=jax.ShapeDtypeStruct((b, s, d), F32),
        compiler_params=_cparams(2),
        name="ffn_ln2",
    )(x, sc2, sh2, g2, lng, lnb, w1, w3, w2)


def _experts_kernel(te_ref, nu_ref, idx_hbm, h_hbm, w1_ref, w3_ref, w2_ref, y_hbm,
                    idx_smem, xbuf, ybuf, sem_idx, sem_rows, sem_out):
    i = pl.program_id(0)
    n_used = nu_ref[0]
    n_tiles = pl.num_programs(0)
    tm = xbuf.shape[1] // ROW_TILE
    slot = lax.rem(i, 2)
    n_idx = idx_smem.shape[0]

    def index_copy(tile, step):
        return pltpu.make_async_copy(idx_hbm.at[tile], idx_smem.at[pl.ds(lax.rem(step, n_idx), 1)], sem_idx)

    def row_tile(r):
        start = r * ROW_TILE
        return pl.ds(start if isinstance(r, int) else pl.multiple_of(start, ROW_TILE), ROW_TILE)

    def gather_copy(step, r, buf=None):
        row = lax.rem(step, n_idx)
        buf = lax.rem(step, 2) if buf is None else buf
        return pltpu.make_async_copy(h_hbm.at[row_tile(idx_smem[row, r])],
                                     xbuf.at[buf, row_tile(r)], sem_rows.at[buf])

    def scatter_copy(step, r, buf=None):
        row = lax.rem(step, n_idx)
        buf = lax.rem(step, 2) if buf is None else buf
        return pltpu.make_async_copy(ybuf.at[buf, row_tile(r)],
                                     y_hbm.at[row_tile(idx_smem[row, tm + r])], sem_out.at[buf])

    def wait_gather(buf):
        pltpu.make_async_copy(h_hbm.at[pl.ds(0, tm * ROW_TILE)], xbuf.at[buf], sem_rows.at[buf]).wait()

    def wait_scatter(buf):
        pltpu.make_async_copy(ybuf.at[buf], y_hbm.at[pl.ds(0, tm * ROW_TILE)], sem_out.at[buf]).wait()

    @pl.when(i == 0)
    def _():
        for tile, step in ((0, 0), (jnp.minimum(1, n_tiles - 1), 1), (n_tiles, n_idx - 1)):
            index_copy(tile, step).start()
            index_copy(tile, step).wait()
        lax.fori_loop(0, tm, lambda r, c: (gather_copy(0, r).start(), c)[1], 0)
        ybuf[1] = jnp.zeros(ybuf.shape[1:], F32)

    def step(slot):
        wait_gather(slot)
        for r in range(tm):
            gather_copy(i + 1, r, 1 - slot).start()
        nxt2 = jnp.minimum(i + 2, n_tiles - 1)
        index_copy(nxt2, i + 2).start()
        for r in range(tm):
            scatter_copy(i + n_idx - 1, r, 1 - slot).start()
        x = _load_row_tiles(xbuf, tm, (slot,))
        _store_row_tiles(ybuf, _swiglu(x.astype(BF16), w1_ref[0], w3_ref[0], w2_ref[0]), (slot,))
        wait_scatter(1 - slot)
        index_copy(nxt2, i + 2).wait()

    for s in (0, 1):
        pl.when((i < n_used) & (slot == s))(functools.partial(step, s))

    @pl.when(i + 1 == n_used)
    def _():
        lax.fori_loop(0, tm, lambda r, c: (scatter_copy(i, r).start(), c)[1], 0)
        wait_scatter(slot)
        wait_gather(1 - slot)


def _experts(tile_expert, n_used, idx, h2, w1, w3, w2):
    n_tiles, tm2 = idx.shape[0] - 1, idx.shape[-1]
    tm = tm2 // 2
    t = h2.shape[0] // ROW_TILE
    _, d, f = w1.shape
    grid_spec = pltpu.PrefetchScalarGridSpec(
        num_scalar_prefetch=2,
        grid=(n_tiles,),
        in_specs=[
            pl.BlockSpec(memory_space=pl.ANY),
            pl.BlockSpec(memory_space=pl.ANY),
            pl.BlockSpec((1, d, f), lambda i, te, nu: (te[i], 0, 0)),
            pl.BlockSpec((1, d, f), lambda i, te, nu: (te[i], 0, 0)),
            pl.BlockSpec((1, f, d), lambda i, te, nu: (te[i], 0, 0)),
        ],
        out_specs=pl.BlockSpec(memory_space=pl.ANY),
        scratch_shapes=[
            pltpu.SMEM((4, 2 * tm), jnp.int32),
            pltpu.VMEM((2, tm * ROW_TILE, LANES), F32),
            pltpu.VMEM((2, tm * ROW_TILE, LANES), F32),
            pltpu.SemaphoreType.DMA(()),
            pltpu.SemaphoreType.DMA((2,)),
            pltpu.SemaphoreType.DMA((2,)),
        ],
    )
    return pl.pallas_call(
        _experts_kernel,
        grid_spec=grid_spec,
        out_shape=jax.ShapeDtypeStruct(((TOP_K * t + tm) * ROW_TILE, LANES), F32),
        compiler_params=_cparams(1),
        name="moe_experts",
    )(tile_expert, n_used, idx, h2, w1, w3, w2)


def _combine_kernel(alpha, y0_ref, y1_ref, gi_ref, x_ref, g2_ref, lng_ref, lnb_ref, o_ref):
    gi = gi_ref[0]
    n = gi.shape[0]
    f = gi[:, 0:1] * _load_row_tiles(y0_ref, n) + gi[:, 1:2] * _load_row_tiles(y1_ref, n)
    r = alpha * x_ref[0] + g2_ref[0] * f
    o_ref[0] = _layernorm(r, lng_ref[...], lnb_ref[...])


def _combine(alpha, y, gi, x, g2, lng, lnb):
    b, s, d = x.shape
    ts = min(TS_COMB, s)
    nj = s // ts
    row = lambda i, j: (i, j, 0)
    return pl.pallas_call(
        functools.partial(_combine_kernel, alpha),
        grid=(b, nj),
        in_specs=[
            pl.BlockSpec((ts * ROW_TILE, LANES), lambda i, j: (i * nj + j, 0)),
            pl.BlockSpec((ts * ROW_TILE, LANES), lambda i, j: (b * nj + i * nj + j, 0)),
            pl.BlockSpec((1, ts, HEAD_PAD), row),
            pl.BlockSpec((1, ts, d), row),
            pl.BlockSpec((1, 1, d), lambda i, j: (i, 0, 0)),
            _const_spec(lng.shape), _const_spec(lnb.shape),
        ],
        out_specs=pl.BlockSpec((1, ts, d), row),
        out_shape=jax.ShapeDtypeStruct((b, s, d), F32),
        compiler_params=_cparams(2),
        name="moe_combine_ln2",
    )(y, y, gi, x, g2, lng, lnb)


def _moe_plan(expert_idx, tm):
    t = expert_idx.shape[0]
    n_slots = t * TOP_K
    flat = expert_idx.reshape(n_slots)
    counts = jnp.sum((flat[:, None] == jnp.arange(N_EXPERTS, dtype=jnp.int32)[None, :]).astype(jnp.int32), axis=0)
    padded = (counts + tm - 1) // tm * tm
    pends = jnp.cumsum(padded)
    pstarts = pends - padded
    starts = jnp.cumsum(counts) - counts
    n_tiles = (n_slots + N_EXPERTS * tm) // tm
    tile_start = jnp.arange(n_tiles, dtype=jnp.int32) * tm
    tile_expert = jnp.minimum(jnp.sum((tile_start[:, None] >= pends[None, :]).astype(jnp.int32), axis=1),
                              N_EXPERTS - 1)
    n_used = (pends[-1] // tm).astype(jnp.int32).reshape(1)
    order = jnp.argsort(flat, stable=True).astype(jnp.int32)
    n_rows = n_tiles * tm
    rows = jnp.arange(n_rows, dtype=jnp.int32)
    order_pad = jnp.concatenate([order, jnp.zeros((n_rows - n_slots,), jnp.int32)])
    slot_of_row = jnp.full((n_rows,), -1, jnp.int32)
    for e in range(N_EXPERTS):
        shifted = jnp.roll(order_pad, pstarts[e] - starts[e])
        slot_of_row = jnp.where((rows >= pstarts[e]) & (rows < pstarts[e] + counts[e]), shifted, slot_of_row)
    valid = slot_of_row >= 0
    tok = slot_of_row // TOP_K
    src = jnp.where(valid, tok, 0)
    dst = jnp.where(valid, (slot_of_row % TOP_K) * t + tok, n_slots + rows % tm)
    idx = jnp.concatenate([src.reshape(n_tiles, 1, tm), dst.reshape(n_tiles, 1, tm)], axis=-1)
    dummy = jnp.concatenate([jnp.zeros((1, 1, tm), jnp.int32),
                             (n_slots + jnp.arange(tm, dtype=jnp.int32)).reshape(1, 1, tm)], axis=-1)
    return tile_expert, n_used, jnp.concatenate([idx.astype(jnp.int32), dummy], axis=0)


def _moe(alpha, x, h2, ei, gi, g2, lng, lnb, w1, w3, w2):
    b, s, d = x.shape
    t = b * s
    tm = min(TM_MOE, t)
    tile_expert, n_used, idx = _moe_plan(ei.reshape(t, HEAD_PAD)[:, :TOP_K], tm)
    y = _experts(tile_expert, n_used, idx, h2, w1, w3, w2)
    return _combine(alpha, y, gi, x, g2, lng, lnb)


def _prep_layer_weights(w_in, w_q_up, w_kv_up, q_norm_g, kv_norm_g):
    n_l, d, _ = w_in.shape
    z96 = jnp.zeros((n_l, d, HEAD_PAD - MLA_ROPE), w_in.dtype)
    o = MLA_Q_LORA + MLA_KV_LORA + MLA_ROPE
    wlat = jnp.concatenate([w_in[:, :, :o], z96], axis=-1)
    nd = DIFF_HEADS * 2 * DIFF_HD
    wdq = w_in[:, :, o:o + nd]
    wdk = w_in[:, :, o + nd:o + 2 * nd]
    wdv = w_in[:, :, o + 2 * nd:o + 3 * nd]
    wga = w_in[:, :, o + 3 * nd:o + 3 * nd + d]
    wgb = w_in[:, :, o + 3 * nd + d:o + 3 * nd + 2 * d]

    hq = MLA_NOPE + MLA_ROPE
    wq4 = w_q_up.reshape(n_l, MLA_Q_LORA, MLA_HEADS, hq)
    zq = jnp.zeros((n_l, MLA_Q_LORA, MLA_HEADS, HEAD_PAD - hq), w_q_up.dtype)
    wq = jnp.concatenate([wq4, zq], axis=-1).reshape(n_l, MLA_Q_LORA, MLA_HEADS * HEAD_PAD)

    wkv4 = w_kv_up.reshape(n_l, MLA_KV_LORA, MLA_HEADS, MLA_NOPE + MLA_V)
    wkn = jnp.concatenate([wkv4[..., :MLA_NOPE], jnp.zeros_like(wkv4[..., :HEAD_PAD - MLA_NOPE])], axis=-1)
    wkn = wkn.reshape(n_l, MLA_KV_LORA, MLA_HEADS * HEAD_PAD)
    wv = wkv4[..., MLA_NOPE:].reshape(n_l, MLA_KV_LORA, MLA_HEADS * MLA_V)

    rr = jnp.arange(HEAD_PAD)[:, None]
    cc = jnp.arange(MLA_HEADS * HEAD_PAD)[None, :]
    esel = ((rr < MLA_ROPE) & (cc % HEAD_PAD == MLA_NOPE + rr)).astype(BF16)

    cast = lambda a: a.astype(BF16)
    return dict(wlat=cast(wlat), wdq=cast(wdq), wdk=cast(wdk), wdv=cast(wdv), wga=cast(wga), wgb=cast(wgb),
                wq=cast(wq), wkn=cast(wkn), wv=cast(wv), esel=esel,
                qg=q_norm_g.reshape(n_l, 1, MLA_Q_LORA), kvg=kv_norm_g.reshape(n_l, 1, MLA_KV_LORA))


def _rope_tables(positions):
    inv_freq = ROPE_BASE ** (-jnp.arange(0, MLA_ROPE, 2, dtype=F32) / MLA_ROPE)
    ang = positions.astype(F32)[..., None] * inv_freq
    cos, sin = jnp.cos(ang), jnp.sin(ang)
    ones = jnp.ones(positions.shape + (MLA_NOPE,), F32)
    tail = HEAD_PAD - MLA_NOPE - MLA_ROPE
    ct = jnp.concatenate([ones, cos, cos, ones[..., :tail]], axis=-1)
    st = jnp.concatenate([0.0 * ones, -sin, sin, 0.0 * ones[..., :tail]], axis=-1)
    return ct, st


def kernel(x, c, positions, w_ada, b_ada, w_in, q_norm_g, w_q_up, kv_norm_g, w_kv_up, lambda_q1, lambda_k1, lambda_q2, lambda_k2, diff_norm_g, w_br_mla, w_br_diff, w_out, ln1_g, ln1_b, ln2_g, ln2_b, ffn_w1, ffn_w3, ffn_w2, router_w, router_b, moe_w1, moe_w3, moe_w2):
    b, s, d = x.shape
    depth = w_in.shape[0]
    alpha = (2.0 * depth) ** 0.25

    mod = _ada_mod(c, w_ada, b_ada)
    ct, st = _rope_tables(positions)
    posf = positions.astype(F32)
    pos_k = posf.reshape(b, s, 1)
    pos_q = posf.reshape(b, 1, s)
    slopes = (2.0 ** (-8.0 * jnp.arange(1, DIFF_HEADS + 1, dtype=F32) / DIFF_HEADS)) * LOG2E

    lw = _prep_layer_weights(w_in, w_q_up, w_kv_up, q_norm_g, kv_norm_g)
    wbm, wbd, wo = w_br_mla.astype(BF16), w_br_diff.astype(BF16), w_out.astype(BF16)
    fw1, fw3, fw2 = ffn_w1.astype(BF16), ffn_w3.astype(BF16), ffn_w2.astype(BF16)
    mw1, mw3, mw2 = moe_w1.astype(BF16), moe_w3.astype(BF16), moe_w2.astype(BF16)
    rw = jnp.pad(router_w, ((0, 0), (0, 0), (0, HEAD_PAD - N_EXPERTS))).astype(BF16)
    rb = jnp.pad(router_b, ((0, 0), (0, HEAD_PAD - N_EXPERTS))).reshape(-1, 1, HEAD_PAD)
    gcol = diff_norm_g.reshape(depth, DIFF_HEADS, DIFF_VD, 1)
    vec = lambda a, l: a[l].reshape(1, -1)

    for l in range(depth):
        sh1, sc1, g1, sh2, sc2, g2 = [m.reshape(b, 1, d) for m in jnp.split(mod[l], 6, axis=-1)]
        w_l = {k: v[l] if k != "esel" else v for k, v in lw.items()}
        qm, km, vt, dq1, dq2, dk, dvt, ga, gb = _inproj(x, sc1, sh1, ct, st, w_l)
        lam_init = 0.8 - 0.6 * math.exp(-0.3 * l)
        mla_o, diff_o = _attention(qm, km, vt, dq1, dq2, dk, dvt, pos_k, pos_q, slopes,
                                   vec(lambda_q1, l), vec(lambda_k1, l), vec(lambda_q2, l), vec(lambda_k2, l),
                                   gcol[l], lam_init)
        mix_args = (alpha, mla_o, diff_o, ga, gb, x, g1, vec(ln1_g, l), vec(ln1_b, l), wbm[l], wbd[l], wo[l])
        if l % 2 == 0:
            (x,) = _mix(*mix_args)
            x = _ffn(alpha, x, sc2, sh2, g2, vec(ln2_g, l), vec(ln2_b, l), fw1[l // 2], fw3[l // 2], fw2[l // 2])
        else:
            x, h2, ei, gi = _mix(*mix_args, router=(sc2, sh2, rw[l // 2], rb[l // 2]))
            x = _moe(alpha, x, h2, ei, gi, g2, vec(ln2_g, l), vec(ln2_b, l),
                     mw1[l // 2], mw3[l // 2], mw2[l // 2])
    return x
```

```python
import functools
import math

import jax
import jax.numpy as jnp
from jax import lax
from jax.experimental import pallas as pl
from jax.experimental.pallas import tpu as pltpu

BF16 = jnp.bfloat16
F32 = jnp.float32

D_MODEL = 1024
MLA_HEADS = 8
MLA_NOPE = 64
MLA_ROPE = 32
MLA_V = 64
MLA_Q_LORA = 384
MLA_KV_LORA = 256
ROPE_BASE = 10000.0
DIFF_HEADS = 8
DIFF_HD = 64
DIFF_VD = 2 * DIFF_HD
N_EXPERTS = 8
TOP_K = 2
LN_EPS = 1e-5
RMS_EPS = 1e-6

HEAD_PAD = 128
ONES_ROWS = 16
LOG2E = 1.4426950408889634
MLA_QSCALE = (MLA_NOPE + MLA_ROPE) ** -0.5 * LOG2E
DIFF_QSCALE = DIFF_HD ** -0.5 * LOG2E

VMEM_LIMIT = 56 * 1024 * 1024
VMEM_LIMIT_ATTN = 60 * 1024 * 1024

TS_PROJ = 512
TQ_ATTN = 256
TS_MIX = 1024
TS_FFN = 512
TM_MOE = 256
TS_COMB = 1024
ROW_CHUNKS = 4


def _cparams(n_axes):
    return pltpu.CompilerParams(dimension_semantics=("arbitrary",) * n_axes,
                                vmem_limit_bytes=VMEM_LIMIT)


def _const_spec(shape):
    nd = len(shape)
    return pl.BlockSpec(shape, lambda *_: (0,) * nd, pipeline_mode=pl.Buffered(1))


def _dot(a, b):
    return jnp.dot(a, b, preferred_element_type=F32)


def _dot_nt(a, b):
    return lax.dot_general(a, b, (((1,), (1,)), ((), ())), preferred_element_type=F32)


def _sigmoid(v):
    return 1.0 / (1.0 + jnp.exp(-v))


def _layernorm(r, g, b):
    mu = jnp.mean(r, axis=-1, keepdims=True)
    d = r - mu
    var = jnp.mean(d * d, axis=-1, keepdims=True)
    return d * lax.rsqrt(var + LN_EPS) * g + b


def _rms_rows(v, g):
    ms = jnp.mean(v * v, axis=-1, keepdims=True)
    return v * lax.rsqrt(ms + RMS_EPS) * g


LANES = 128
ROW_TILE = D_MODEL // LANES


def _store_row_tiles(ref, v, lead=(), row0=0):
    n = v.shape[0]
    for c in range(ROW_TILE):
        ref[lead + (pl.ds(row0 * ROW_TILE + c, n, stride=ROW_TILE), slice(None))] = v[:, c * LANES:(c + 1) * LANES]


def _load_row_tiles(ref, n, lead=()):
    return jnp.concatenate(
        [ref[lead + (pl.ds(c, n, stride=ROW_TILE), slice(None))] for c in range(ROW_TILE)], axis=1)


def _ada_kernel(c_ref, w_ref, b_ref, o_ref):
    c = c_ref[...]
    cond = c * _sigmoid(c)
    o_ref[0] = _dot(cond.astype(BF16), w_ref[0].astype(BF16)) + b_ref[0]


def _ada_mod(c, w_ada, b_ada):
    n_l, d, n6 = w_ada.shape
    b = c.shape[0]
    tn = 1536
    return pl.pallas_call(
        _ada_kernel,
        grid=(n_l, n6 // tn),
        in_specs=[
            pl.BlockSpec((b, d), lambda l, j: (0, 0)),
            pl.BlockSpec((1, d, tn), lambda l, j: (l, 0, j)),
            pl.BlockSpec((1, 1, tn), lambda l, j: (l, 0, j)),
        ],
        out_specs=pl.BlockSpec((1, b, tn), lambda l, j: (l, 0, j)),
        out_shape=jax.ShapeDtypeStruct((n_l, b, n6), F32),
        compiler_params=_cparams(2),
        name="ada_mod",
    )(c, w_ada, b_ada.reshape(n_l, 1, n6))


def _inproj_kernel(x_ref, sc_ref, sh_ref, ct_ref, st_ref,
                   wlat_ref, wdq_ref, wdk_ref, wdv_ref, wga_ref, wgb_ref,
                   qg_ref, kvg_ref, wq_ref, wkn_ref, wv_ref, esel_ref,
                   qm_ref, km_ref, vt_ref, dq1_ref, dq2_ref, dk_ref, dvt_ref, ga_ref, gb_ref):
    x = x_ref[0]
    h = (x * (1.0 + sc_ref[0]) + sh_ref[0]).astype(BF16)
    half = MLA_ROPE // 2
    ct = ct_ref[0]
    st = st_ref[0]
    lane = lax.broadcasted_iota(jnp.int32, st.shape, 1)
    sa = jnp.where(lane < MLA_NOPE + half, st, 0.0)
    sb = st - sa

    def rotate(v, c, s_first, s_second):
        n = v.shape[1]
        return v * c + pltpu.roll(v, n - half, axis=1) * s_first + pltpu.roll(v, half, axis=1) * s_second

    lat = _dot(h, wlat_ref[...])
    q_lat = lat[:, :MLA_Q_LORA]
    kv_lat = lat[:, MLA_Q_LORA:MLA_Q_LORA + MLA_KV_LORA]
    kr = lat[:, MLA_Q_LORA + MLA_KV_LORA:]

    qn = _rms_rows(q_lat, qg_ref[...]).astype(BF16)
    q = _dot(qn, wq_ref[...])
    tile8 = lambda t: jnp.concatenate([t] * MLA_HEADS, axis=1)
    qr = (rotate(q, tile8(ct), tile8(sa), tile8(sb)) * MLA_QSCALE).astype(BF16)
    for hd in range(MLA_HEADS):
        qm_ref[0, hd] = qr[:, hd * HEAD_PAD:(hd + 1) * HEAD_PAD]

    kvn = _rms_rows(kv_lat, kvg_ref[...]).astype(BF16)
    kn = _dot(kvn, wkn_ref[...])
    to_front = lambda t: pltpu.roll(t, HEAD_PAD - MLA_NOPE, axis=1)
    kro = rotate(kr, to_front(ct), to_front(sa), to_front(sb)).astype(BF16)
    kcat = (kn + _dot(kro, esel_ref[...])).astype(BF16)
    for hd in range(MLA_HEADS):
        km_ref[0, hd] = kcat[:, hd * HEAD_PAD:(hd + 1) * HEAD_PAD]
    v = _dot(kvn, wv_ref[...])
    vt = v.T.astype(BF16)
    ones = jnp.ones((ONES_ROWS, vt.shape[1]), BF16)
    for hd in range(MLA_HEADS):
        vt_ref[0, hd, :MLA_V, :] = vt[hd * MLA_V:(hd + 1) * MLA_V, :]
        vt_ref[0, hd, MLA_V:, :] = ones

    dq = _dot(h, wdq_ref[...]) * DIFF_QSCALE
    lane = lax.broadcasted_iota(jnp.int32, dq.shape, 1)
    first = (lane % HEAD_PAD) < DIFF_HD
    dq1 = jnp.where(first, dq, 0.0).astype(BF16)
    dq2 = jnp.where(first, 0.0, dq).astype(BF16)
    dk = _dot(h, wdk_ref[...]).astype(BF16)
    dvt = _dot(h, wdv_ref[...]).T.astype(BF16)
    for hd in range(DIFF_HEADS):
        sl = slice(hd * HEAD_PAD, (hd + 1) * HEAD_PAD)
        dq1_ref[0, hd] = dq1[:, sl]
        dq2_ref[0, hd] = dq2[:, sl]
        dk_ref[0, hd] = dk[:, sl]
        dvt_ref[0, hd, :DIFF_VD, :] = dvt[sl, :]
        dvt_ref[0, hd, DIFF_VD:, :] = ones

    ga_ref[0] = _sigmoid(_dot(h, wga_ref[...])).astype(BF16)
    gb_ref[0] = _sigmoid(_dot(h, wgb_ref[...])).astype(BF16)


def _inproj(x, sc1, sh1, ct, st, w):
    b, s, d = x.shape
    ts = min(TS_PROJ, s)
    hh = MLA_HEADS
    row = lambda i, j: (i, j, 0)
    bat = lambda i, j: (i, 0, 0)
    head_rows = pl.BlockSpec((1, hh, ts, HEAD_PAD), lambda i, j: (i, 0, j, 0))
    weights = [w["wlat"], w["wdq"], w["wdk"], w["wdv"], w["wga"], w["wgb"],
               w["qg"], w["kvg"], w["wq"], w["wkn"], w["wv"], w["esel"]]
    head_shape = jax.ShapeDtypeStruct((b, hh, s, HEAD_PAD), BF16)
    return pl.pallas_call(
        _inproj_kernel,
        grid=(b, s // ts),
        in_specs=[
            pl.BlockSpec((1, ts, d), row),
            pl.BlockSpec((1, 1, d), bat),
            pl.BlockSpec((1, 1, d), bat),
            pl.BlockSpec((1, ts, HEAD_PAD), row),
            pl.BlockSpec((1, ts, HEAD_PAD), row),
        ] + [_const_spec(a.shape) for a in weights],
        out_specs=[
            head_rows, head_rows,
            pl.BlockSpec((1, hh, MLA_V + ONES_ROWS, ts), lambda i, j: (i, 0, 0, j)),
            head_rows, head_rows, head_rows,
            pl.BlockSpec((1, hh, DIFF_VD + ONES_ROWS, ts), lambda i, j: (i, 0, 0, j)),
            pl.BlockSpec((1, ts, d), row),
            pl.BlockSpec((1, ts, d), row),
        ],
        out_shape=[
            head_shape, head_shape,
            jax.ShapeDtypeStruct((b, hh, MLA_V + ONES_ROWS, s), BF16),
            head_shape, head_shape, head_shape,
            jax.ShapeDtypeStruct((b, hh, DIFF_VD + ONES_ROWS, s), BF16),
            jax.ShapeDtypeStruct((b, s, d), BF16),
            jax.ShapeDtypeStruct((b, s, d), BF16),
        ],
        compiler_params=_cparams(2),
        name="inproj",
    )(x, sc1, sh1, ct, st, *weights)


def _attn_kernel(lam_init, mq_ref, mk_ref, mvt_ref, q1_ref, q2_ref, k_ref, vt_ref, pk_ref, pq_ref,
                 slope_ref, lq1_ref, lk1_ref, lq2_ref, lk2_ref, g_ref, mo_ref, o_ref,
                 macc_ref, acc_ref, dist_ref, za_ref, zb_ref, ya_ref, yb_ref):
    dist_ref[...] = jnp.abs(pk_ref[0] - pq_ref[0])
    lam = (jnp.exp(jnp.sum(lq1_ref[...] * lk1_ref[...], axis=1, keepdims=True))
           - jnp.exp(jnp.sum(lq2_ref[...] * lk2_ref[...], axis=1, keepdims=True))
           + lam_init)

    def diff_scores(hd, z_ref):
        k = k_ref[0, hd]
        bias = slope_ref[hd] * dist_ref[...]
        z1 = _dot_nt(k, q1_ref[0, hd]) - bias
        z2 = _dot_nt(k, q2_ref[0, hd]) - bias
        z_ref[0] = z1
        z_ref[1] = z2
        return jnp.max(z1, axis=0, keepdims=True), jnp.max(z2, axis=0, keepdims=True)

    def diff_values(hd, z_ref, m1, m2):
        e1 = jnp.exp2(z_ref[0] - m1).astype(BF16)
        e2 = jnp.exp2(z_ref[1] - m2).astype(BF16)
        o1 = _dot(vt_ref[0, hd], e1)
        o2 = _dot(vt_ref[0, hd], e2)
        r1 = 1.0 / o1[DIFF_VD:DIFF_VD + 1]
        r2 = lam / o2[DIFF_VD:DIFF_VD + 1]
        o_t = o1[:DIFF_VD] * r1 - o2[:DIFF_VD] * r2
        ms = jnp.mean(o_t * o_t, axis=0, keepdims=True)
        o_t = o_t * lax.rsqrt(ms + RMS_EPS) * g_ref[hd] * (1.0 - lam_init)
        acc_ref[pl.ds(pl.multiple_of(hd * DIFF_VD, DIFF_VD), DIFF_VD), :] = o_t

    def mla_scores(hd, y_ref):
        s_t = _dot_nt(mk_ref[0, hd], mq_ref[0, hd])
        y_ref[...] = s_t
        return jnp.max(s_t, axis=0, keepdims=True)

    def mla_values(hd, y_ref, m):
        p = jnp.exp2(y_ref[...] - m).astype(BF16)
        o_t = _dot(mvt_ref[0, hd], p)
        macc_ref[pl.ds(pl.multiple_of(hd * MLA_V, MLA_V), MLA_V), :] = o_t[:MLA_V] / o_t[MLA_V:MLA_V + 1]

    def scores(hd, z_ref, y_ref):
        return diff_scores(hd, z_ref), mla_scores(hd, y_ref)

    def values(hd, z_ref, y_ref, m, mla_first):
        if mla_first:
            mla_values(hd, y_ref, m[1])
        diff_values(hd, z_ref, *m[0])
        if not mla_first:
            mla_values(hd, y_ref, m[1])

    def body(i, ma):
        h0 = 2 * i
        mb = scores(h0 + 1, zb_ref, yb_ref)
        values(h0, za_ref, ya_ref, ma, True)
        ma = scores(h0 + 2, za_ref, ya_ref)
        values(h0 + 1, zb_ref, yb_ref, mb, True)
        return ma

    last = DIFF_HEADS - 1
    ma = lax.fori_loop(0, DIFF_HEADS // 2 - 1, body, scores(0, za_ref, ya_ref))
    mb = scores(last, zb_ref, yb_ref)
    values(last - 1, za_ref, ya_ref, ma, False)
    values(last, zb_ref, yb_ref, mb, False)
    o_ref[0] = acc_ref[...].T.astype(BF16)
    mo_ref[0] = macc_ref[...].T.astype(BF16)


def _attention(qm, km, vt, dq1, dq2, dk, dvt, pos_k, pos_q, slopes, lq1, lk1, lq2, lk2, gcol, lam_init):
    assert MLA_HEADS == DIFF_HEADS
    b, hh, s, _ = dk.shape
    tq = min(TQ_ATTN, s)
    head_q = pl.BlockSpec((1, hh, tq, HEAD_PAD), lambda i, j: (i, 0, j, 0))
    head_k = pl.BlockSpec((1, hh, s, HEAD_PAD), lambda i, j: (i, 0, 0, 0))
    vec = pl.BlockSpec((1, DIFF_HD), lambda i, j: (0, 0))
    return pl.pallas_call(
        functools.partial(_attn_kernel, lam_init),
        grid=(b, s // tq),
        in_specs=[
            head_q, head_k,
            pl.BlockSpec((1, hh, MLA_V + ONES_ROWS, s), lambda i, j: (i, 0, 0, 0)),
            head_q, head_q, head_k,
            pl.BlockSpec((1, hh, DIFF_VD + ONES_ROWS, s), lambda i, j: (i, 0, 0, 0)),
            pl.BlockSpec((1, s, 1), lambda i, j: (i, 0, 0)),
            pl.BlockSpec((1, 1, tq), lambda i, j: (i, 0, j)),
            pl.BlockSpec(memory_space=pltpu.SMEM),
            vec, vec, vec, vec,
            pl.BlockSpec((hh, DIFF_VD, 1), lambda i, j: (0, 0, 0)),
        ],
        out_specs=[pl.BlockSpec((1, tq, hh * MLA_V), lambda i, j: (i, j, 0)),
                   pl.BlockSpec((1, tq, hh * DIFF_VD), lambda i, j: (i, j, 0))],
        out_shape=[jax.ShapeDtypeStruct((b, s, hh * MLA_V), BF16),
                   jax.ShapeDtypeStruct((b, s, hh * DIFF_VD), BF16)],
        scratch_shapes=[pltpu.VMEM((hh * MLA_V, tq), F32), pltpu.VMEM((hh * DIFF_VD, tq), F32),
                        pltpu.VMEM((s, tq), F32),
                        pltpu.VMEM((2, s, tq), F32), pltpu.VMEM((2, s, tq), F32),
                        pltpu.VMEM((s, tq), F32), pltpu.VMEM((s, tq), F32)],
        compiler_params=pltpu.CompilerParams(dimension_semantics=("arbitrary", "arbitrary"),
                                             vmem_limit_bytes=VMEM_LIMIT_ATTN),
        name="attention",
    )(qm, km, vt, dq1, dq2, dk, dvt, pos_k, pos_q, slopes, lq1, lk1, lq2, lk2, gcol)


def _route_top2(h, rw, rb):
    logits = _dot(h.astype(BF16), rw) + rb
    lane = lax.broadcasted_iota(jnp.int32, logits.shape, 1)
    lane_f = lane.astype(F32)
    neg = jnp.float32(-jnp.inf)
    lg = jnp.where(lane < N_EXPERTS, logits, neg)
    m1 = jnp.max(lg, axis=1, keepdims=True)
    i1 = jnp.min(jnp.where(lg == m1, lane_f, 128.0), axis=1, keepdims=True)
    lg2 = jnp.where(lane_f == i1, neg, lg)
    m2 = jnp.max(lg2, axis=1, keepdims=True)
    i2 = jnp.min(jnp.where(lg2 == m2, lane_f, 128.0), axis=1, keepdims=True)
    t = jnp.exp(m2 - m1)
    den = 1.0 + t
    ei = jnp.where(lane == 0, i1, jnp.where(lane == 1, i2, 0.0)).astype(jnp.int32)
    gi = jnp.where(lane == 0, 1.0 / den, jnp.where(lane == 1, t / den, 0.0))
    return ei, gi


def _mix_kernel(alpha, route, mo_ref, do_ref, ga_ref, gb_ref, x_ref, g1_ref, lng_ref, lnb_ref,
                wbm_ref, wbd_ref, wo_ref, *rest):
    if route:
        sc_ref, sh_ref, rw_ref, rb_ref, o_ref, h_ref, ei_ref, gi_ref = rest
    else:
        (o_ref,) = rest
    ch = x_ref.shape[1] // ROW_CHUNKS
    for c in range(ROW_CHUNKS):
        rows = pl.ds(c * ch, ch)
        ya = _dot(mo_ref[0, rows, :], wbm_ref[...])
        yb = _dot(do_ref[0, rows, :], wbd_ref[...])
        gated = (ga_ref[0, rows, :].astype(F32) * ya + gb_ref[0, rows, :].astype(F32) * yb).astype(BF16)
        mix = _dot(gated, wo_ref[...])
        r = alpha * x_ref[0, rows, :] + g1_ref[0] * mix
        x1 = _layernorm(r, lng_ref[...], lnb_ref[...])
        o_ref[0, rows, :] = x1
        if route:
            h = x1 * (1.0 + sc_ref[0]) + sh_ref[0]
            _store_row_tiles(h_ref, h, row0=c * ch)
            ei_ref[0, rows, :], gi_ref[0, rows, :] = _route_top2(h, rw_ref[...], rb_ref[...])


def _mix(alpha, mla_o, diff_o, ga, gb, x, g1, lng, lnb, wbm, wbd, wo, router=None):
    b, s, d = x.shape
    ts = min(TS_MIX, s)
    nj = s // ts
    row = lambda i, j: (i, j, 0)
    bat = lambda i, j: (i, 0, 0)
    in_specs = [
        pl.BlockSpec((1, ts, mla_o.shape[-1]), row),
        pl.BlockSpec((1, ts, diff_o.shape[-1]), row),
        pl.BlockSpec((1, ts, d), row),
        pl.BlockSpec((1, ts, d), row),
        pl.BlockSpec((1, ts, d), row),
        pl.BlockSpec((1, 1, d), bat),
        _const_spec(lng.shape), _const_spec(lnb.shape),
        _const_spec(wbm.shape), _const_spec(wbd.shape), _const_spec(wo.shape),
    ]
    out_specs = [pl.BlockSpec((1, ts, d), row)]
    out_shape = [jax.ShapeDtypeStruct((b, s, d), F32)]
    args = [mla_o, diff_o, ga, gb, x, g1, lng, lnb, wbm, wbd, wo]
    if router is not None:
        sc2, sh2, rw, rb = router
        in_specs += [pl.BlockSpec((1, 1, d), bat), pl.BlockSpec((1, 1, d), bat),
                     _const_spec(rw.shape), _const_spec(rb.shape)]
        args += [sc2, sh2, rw, rb]
        out_specs += [pl.BlockSpec((ts * ROW_TILE, LANES), lambda i, j: (i * nj + j, 0)),
                      pl.BlockSpec((1, ts, HEAD_PAD), row), pl.BlockSpec((1, ts, HEAD_PAD), row)]
        out_shape += [jax.ShapeDtypeStruct((b * s * ROW_TILE, LANES), F32),
                      jax.ShapeDtypeStruct((b, s, HEAD_PAD), jnp.int32),
                      jax.ShapeDtypeStruct((b, s, HEAD_PAD), F32)]
    return pl.pallas_call(
        functools.partial(_mix_kernel, alpha, router is not None),
        grid=(b, nj),
        in_specs=in_specs,
        out_specs=out_specs,
        out_shape=out_shape,
        compiler_params=_cparams(2),
        name="mix_ln1",
    )(*args)


def _swiglu(h, w1, w3, w2):
    a = _dot(h, w1)
    bgate = _dot(h, w3)
    u = (a * _sigmoid(a) * bgate).astype(BF16)
    return _dot(u, w2)


def _ffn_kernel(alpha, x_ref, sc_ref, sh_ref, g2_ref, lng_ref, lnb_ref, w1_ref, w3_ref, w2_ref, o_ref):
    x = x_ref[0]
    h = (x * (1.0 + sc_ref[0]) + sh_ref[0]).astype(BF16)
    f = _swiglu(h, w1_ref[...], w3_ref[...], w2_ref[...])
    r = alpha * x + g2_ref[0] * f
    o_ref[0] = _layernorm(r, lng_ref[...], lnb_ref[...])


def _ffn(alpha, x, sc2, sh2, g2, lng, lnb, w1, w3, w2):
    b, s, d = x.shape
    ts = min(TS_FFN, s)
    row = lambda i, j: (i, j, 0)
    bat = lambda i, j: (i, 0, 0)
    return pl.pallas_call(
        functools.partial(_ffn_kernel, alpha),
        grid=(b, s // ts),
        in_specs=[
            pl.BlockSpec((1, ts, d), row),
            pl.BlockSpec((1, 1, d), bat), pl.BlockSpec((1, 1, d), bat), pl.BlockSpec((1, 1, d), bat),
            _const_spec(lng.shape), _const_spec(lnb.shape),
            _const_spec(w1.shape), _const_spec(w3.shape), _const_spec(w2.shape),
        ],
        out_specs=pl.BlockSpec((1, ts, d), row),
        out_shape=jax.ShapeDtypeStruct((b, s, d), F32),
        compiler_params=_cparams(2),
        name="ffn_ln2",
    )(x, sc2, sh2, g2, lng, lnb, w1, w3, w2)


def _experts_kernel(te_ref, nu_ref, idx_hbm, h_hbm, w1_ref, w3_ref, w2_ref, y_hbm,
                    idx_smem, xbuf, ybuf, sem_idx, sem_rows, sem_out):
    i = pl.program_id(0)
    n_used = nu_ref[0]
    n_tiles = pl.num_programs(0)
    tm = xbuf.shape[1] // ROW_TILE
    slot = lax.rem(i, 2)
    n_idx = idx_smem.shape[0]

    def index_copy(tile, step):
        return pltpu.make_async_copy(idx_hbm.at[tile], idx_smem.at[pl.ds(lax.rem(step, n_idx), 1)], sem_idx)

    def row_tile(r):
        start = r * ROW_TILE
        return pl.ds(start if isinstance(r, int) else pl.multiple_of(start, ROW_TILE), ROW_TILE)

    def gather_copy(step, r, buf=None):
        row = lax.rem(step, n_idx)
        buf = lax.rem(step, 2) if buf is None else buf
        return pltpu.make_async_copy(h_hbm.at[row_tile(idx_smem[row, r])],
                                     xbuf.at[buf, row_tile(r)], sem_rows.at[buf])

    def scatter_copy(step, r, buf=None):
        row = lax.rem(step, n_idx)
        buf = lax.rem(step, 2) if buf is None else buf
        return pltpu.make_async_copy(ybuf.at[buf, row_tile(r)],
                                     y_hbm.at[row_tile(idx_smem[row, tm + r])], sem_out.at[buf])

    def wait_gather(buf):
        pltpu.make_async_copy(h_hbm.at[pl.ds(0, tm * ROW_TILE)], xbuf.at[buf], sem_rows.at[buf]).wait()

    def wait_scatter(buf):
        pltpu.make_async_copy(ybuf.at[buf], y_hbm.at[pl.ds(0, tm * ROW_TILE)], sem_out.at[buf]).wait()

    @pl.when(i == 0)
    def _():
        for tile, step in ((0, 0), (jnp.minimum(1, n_tiles - 1), 1), (n_tiles, n_idx - 1)):
            index_copy(tile, step).start()
            index_copy(tile, step).wait()
        lax.fori_loop(0, tm, lambda r, c: (gather_copy(0, r).start(), c)[1], 0)
        ybuf[1] = jnp.zeros(ybuf.shape[1:], F32)

    def step(slot):
        wait_gather(slot)
        for r in range(tm):
            gather_copy(i + 1, r, 1 - slot).start()
        nxt2 = jnp.minimum(i + 2, n_tiles - 1)
        index_copy(nxt2, i + 2).start()
        for r in range(tm):
            scatter_copy(i + n_idx - 1, r, 1 - slot).start()
        x = _load_row_tiles(xbuf, tm, (slot,))
        _store_row_tiles(ybuf, _swiglu(x.astype(BF16), w1_ref[0], w3_ref[0], w2_ref[0]), (slot,))
        wait_scatter(1 - slot)
        index_copy(nxt2, i + 2).wait()

    for s in (0, 1):
        pl.when((i < n_used) & (slot == s))(functools.partial(step, s))

    @pl.when(i + 1 == n_used)
    def _():
        lax.fori_loop(0, tm, lambda r, c: (scatter_copy(i, r).start(), c)[1], 0)
        wait_scatter(slot)
        wait_gather(1 - slot)


def _experts(tile_expert, n_used, idx, h2, w1, w3, w2):
    n_tiles, tm2 = idx.shape[0] - 1, idx.shape[-1]
    tm = tm2 // 2
    t = h2.shape[0] // ROW_TILE
    _, d, f = w1.shape
    grid_spec = pltpu.PrefetchScalarGridSpec(
        num_scalar_prefetch=2,
        grid=(n_tiles,),
        in_specs=[
            pl.BlockSpec(memory_space=pl.ANY),
            pl.BlockSpec(memory_space=pl.ANY),
            pl.BlockSpec((1, d, f), lambda i, te, nu: (te[i], 0, 0)),
            pl.BlockSpec((1, d, f), lambda i, te, nu: (te[i], 0, 0)),
            pl.BlockSpec((1, f, d), lambda i, te, nu: (te[i], 0, 0)),
        ],
        out_specs=pl.BlockSpec(memory_space=pl.ANY),
        scratch_shapes=[
            pltpu.SMEM((4, 2 * tm), jnp.int32),
            pltpu.VMEM((2, tm * ROW_TILE, LANES), F32),
            pltpu.VMEM((2, tm * ROW_TILE, LANES), F32),
            pltpu.SemaphoreType.DMA(()),
            pltpu.SemaphoreType.DMA((2,)),
            pltpu.SemaphoreType.DMA((2,)),
        ],
    )
    return pl.pallas_call(
        _experts_kernel,
        grid_spec=grid_spec,
        out_shape=jax.ShapeDtypeStruct(((TOP_K * t + tm) * ROW_TILE, LANES), F32),
        compiler_params=_cparams(1),
        name="moe_experts",
    )(tile_expert, n_used, idx, h2, w1, w3, w2)


def _combine_kernel(alpha, y0_ref, y1_ref, gi_ref, x_ref, g2_ref, lng_ref, lnb_ref, o_ref):
    gi = gi_ref[0]
    n = gi.shape[0]
    f = gi[:, 0:1] * _load_row_tiles(y0_ref, n) + gi[:, 1:2] * _load_row_tiles(y1_ref, n)
    r = alpha * x_ref[0] + g2_ref[0] * f
    o_ref[0] = _layernorm(r, lng_ref[...], lnb_ref[...])


def _combine(alpha, y, gi, x, g2, lng, lnb):
    b, s, d = x.shape
    ts = min(TS_COMB, s)
    nj = s // ts
    row = lambda i, j: (i, j, 0)
    return pl.pallas_call(
        functools.partial(_combine_kernel, alpha),
        grid=(b, nj),
        in_specs=[
            pl.BlockSpec((ts * ROW_TILE, LANES), lambda i, j: (i * nj + j, 0)),
            pl.BlockSpec((ts * ROW_TILE, LANES), lambda i, j: (b * nj + i * nj + j, 0)),
            pl.BlockSpec((1, ts, HEAD_PAD), row),
            pl.BlockSpec((1, ts, d), row),
            pl.BlockSpec((1, 1, d), lambda i, j: (i, 0, 0)),
            _const_spec(lng.shape), _const_spec(lnb.shape),
        ],
        out_specs=pl.BlockSpec((1, ts, d), row),
        out_shape=jax.ShapeDtypeStruct((b, s, d), F32),
        compiler_params=_cparams(2),
        name="moe_combine_ln2",
    )(y, y, gi, x, g2, lng, lnb)


def _moe_plan(expert_idx, tm):
    t = expert_idx.shape[0]
    n_slots = t * TOP_K
    flat = expert_idx.reshape(n_slots)
    counts = jnp.sum((flat[:, None] == jnp.arange(N_EXPERTS, dtype=jnp.int32)[None, :]).astype(jnp.int32), axis=0)
    padded = (counts + tm - 1) // tm * tm
    pends = jnp.cumsum(padded)
    pstarts = pends - padded
    starts = jnp.cumsum(counts) - counts
    n_tiles = (n_slots + N_EXPERTS * tm) // tm
    tile_start = jnp.arange(n_tiles, dtype=jnp.int32) * tm
    tile_expert = jnp.minimum(jnp.sum((tile_start[:, None] >= pends[None, :]).astype(jnp.int32), axis=1),
                              N_EXPERTS - 1)
    n_used = (pends[-1] // tm).astype(jnp.int32).reshape(1)
    order = jnp.argsort(flat, stable=True).astype(jnp.int32)
    n_rows = n_tiles * tm
    rows = jnp.arange(n_rows, dtype=jnp.int32)
    order_pad = jnp.concatenate([order, jnp.zeros((n_rows - n_slots,), jnp.int32)])
    slot_of_row = jnp.full((n_rows,), -1, jnp.int32)
    for e in range(N_EXPERTS):
        shifted = jnp.roll(order_pad, pstarts[e] - starts[e])
        slot_of_row = jnp.where((rows >= pstarts[e]) & (rows < pstarts[e] + counts[e]), shifted, slot_of_row)
    valid = slot_of_row >= 0
    tok = slot_of_row // TOP_K
    src = jnp.where(valid, tok, 0)
    dst = jnp.where(valid, (slot_of_row % TOP_K) * t + tok, n_slots + rows % tm)
    idx = jnp.concatenate([src.reshape(n_tiles, 1, tm), dst.reshape(n_tiles, 1, tm)], axis=-1)
    dummy = jnp.concatenate([jnp.zeros((1, 1, tm), jnp.int32),
                             (n_slots + jnp.arange(tm, dtype=jnp.int32)).reshape(1, 1, tm)], axis=-1)
    return tile_expert, n_used, jnp.concatenate([idx.astype(jnp.int32), dummy], axis=0)


def _moe(alpha, x, h2, ei, gi, g2, lng, lnb, w1, w3, w2):
    b, s, d = x.shape
    t = b * s
    tm = min(TM_MOE, t)
    tile_expert, n_used, idx = _moe_plan(ei.reshape(t, HEAD_PAD)[:, :TOP_K], tm)
    y = _experts(tile_expert, n_used, idx, h2, w1, w3, w2)
    return _combine(alpha, y, gi, x, g2, lng, lnb)


def _prep_layer_weights(w_in, w_q_up, w_kv_up, q_norm_g, kv_norm_g):
    n_l, d, _ = w_in.shape
    z96 = jnp.zeros((n_l, d, HEAD_PAD - MLA_ROPE), w_in.dtype)
    o = MLA_Q_LORA + MLA_KV_LORA + MLA_ROPE
    wlat = jnp.concatenate([w_in[:, :, :o], z96], axis=-1)
    nd = DIFF_HEADS * 2 * DIFF_HD
    wdq = w_in[:, :, o:o + nd]
    wdk = w_in[:, :, o + nd:o + 2 * nd]
    wdv = w_in[:, :, o + 2 * nd:o + 3 * nd]
    wga = w_in[:, :, o + 3 * nd:o + 3 * nd + d]
    wgb = w_in[:, :, o + 3 * nd + d:o + 3 * nd + 2 * d]

    hq = MLA_NOPE + MLA_ROPE
    wq4 = w_q_up.reshape(n_l, MLA_Q_LORA, MLA_HEADS, hq)
    zq = jnp.zeros((n_l, MLA_Q_LORA, MLA_HEADS, HEAD_PAD - hq), w_q_up.dtype)
    wq = jnp.concatenate([wq4, zq], axis=-1).reshape(n_l, MLA_Q_LORA, MLA_HEADS * HEAD_PAD)

    wkv4 = w_kv_up.reshape(n_l, MLA_KV_LORA, MLA_HEADS, MLA_NOPE + MLA_V)
    wkn = jnp.concatenate([wkv4[..., :MLA_NOPE], jnp.zeros_like(wkv4[..., :HEAD_PAD - MLA_NOPE])], axis=-1)
    wkn = wkn.reshape(n_l, MLA_KV_LORA, MLA_HEADS * HEAD_PAD)
    wv = wkv4[..., MLA_NOPE:].reshape(n_l, MLA_KV_LORA, MLA_HEADS * MLA_V)

    rr = jnp.arange(HEAD_PAD)[:, None]
    cc = jnp.arange(MLA_HEADS * HEAD_PAD)[None, :]
    esel = ((rr < MLA_ROPE) & (cc % HEAD_PAD == MLA_NOPE + rr)).astype(BF16)

    cast = lambda a: a.astype(BF16)
    return dict(wlat=cast(wlat), wdq=cast(wdq), wdk=cast(wdk), wdv=cast(wdv), wga=cast(wga), wgb=cast(wgb),
                wq=cast(wq), wkn=cast(wkn), wv=cast(wv), esel=esel,
                qg=q_norm_g.reshape(n_l, 1, MLA_Q_LORA), kvg=kv_norm_g.reshape(n_l, 1, MLA_KV_LORA))


def _rope_tables(positions):
    inv_freq = ROPE_BASE ** (-jnp.arange(0, MLA_ROPE, 2, dtype=F32) / MLA_ROPE)
    ang = positions.astype(F32)[..., None] * inv_freq
    cos, sin = jnp.cos(ang), jnp.sin(ang)
    ones = jnp.ones(positions.shape + (MLA_NOPE,), F32)
    tail = HEAD_PAD - MLA_NOPE - MLA_ROPE
    ct = jnp.concatenate([ones, cos, cos, ones[..., :tail]], axis=-1)
    st = jnp.concatenate([0.0 * ones, -sin, sin, 0.0 * ones[..., :tail]], axis=-1)
    return ct, st


def kernel(x, c, positions, w_ada, b_ada, w_in, q_norm_g, w_q_up, kv_norm_g, w_kv_up, lambda_q1, lambda_k1, lambda_q2, lambda_k2, diff_norm_g, w_br_mla, w_br_diff, w_out, ln1_g, ln1_b, ln2_g, ln2_b, ffn_w1, ffn_w3, ffn_w2, router_w, router_b, moe_w1, moe_w3, moe_w2):
    b, s, d = x.shape
    depth = w_in.shape[0]
    alpha = (2.0 * depth) ** 0.25

    mod = _ada_mod(c, w_ada, b_ada)
    ct, st = _rope_tables(positions)
    posf = positions.astype(F32)
    pos_k = posf.reshape(b, s, 1)
    pos_q = posf.reshape(b, 1, s)
    slopes = (2.0 ** (-8.0 * jnp.arange(1, DIFF_HEADS + 1, dtype=F32) / DIFF_HEADS)) * LOG2E

    lw = _prep_layer_weights(w_in, w_q_up, w_kv_up, q_norm_g, kv_norm_g)
    wbm, wbd, wo = w_br_mla.astype(BF16), w_br_diff.astype(BF16), w_out.astype(BF16)
    fw1, fw3, fw2 = ffn_w1.astype(BF16), ffn_w3.astype(BF16), ffn_w2.astype(BF16)
    mw1, mw3, mw2 = moe_w1.astype(BF16), moe_w3.astype(BF16), moe_w2.astype(BF16)
    rw = jnp.pad(router_w, ((0, 0), (0, 0), (0, HEAD_PAD - N_EXPERTS))).astype(BF16)
    rb = jnp.pad(router_b, ((0, 0), (0, HEAD_PAD - N_EXPERTS))).reshape(-1, 1, HEAD_PAD)
    gcol = diff_norm_g.reshape(depth, DIFF_HEADS, DIFF_VD, 1)
    vec = lambda a, l: a[l].reshape(1, -1)

    for l in range(depth):
        sh1, sc1, g1, sh2, sc2, g2 = [m.reshape(b, 1, d) for m in jnp.split(mod[l], 6, axis=-1)]
        w_l = {k: v[l] if k != "esel" else v for k, v in lw.items()}
        qm, km, vt, dq1, dq2, dk, dvt, ga, gb = _inproj(x, sc1, sh1, ct, st, w_l)
        lam_init = 0.8 - 0.6 * math.exp(-0.3 * l)
        mla_o, diff_o = _attention(qm, km, vt, dq1, dq2, dk, dvt, pos_k, pos_q, slopes,
                                   vec(lambda_q1, l), vec(lambda_k1, l), vec(lambda_q2, l), vec(lambda_k2, l),
                                   gcol[l], lam_init)
        mix_args = (alpha, mla_o, diff_o, ga, gb, x, g1, vec(ln1_g, l), vec(ln1_b, l), wbm[l], wbd[l], wo[l])
        if l % 2 == 0:
            (x,) = _mix(*mix_args)
            x = _ffn(alpha, x, sc2, sh2, g2, vec(ln2_g, l), vec(ln2_b, l), fw1[l // 2], fw3[l // 2], fw2[l // 2])
        else:
            x, h2, ei, gi = _mix(*mix_args, router=(sc2, sh2, rw[l // 2], rb[l // 2]))
            x = _moe(alpha, x, h2, ei, gi, g2, vec(ln2_g, l), vec(ln2_b, l),
                     mw1[l // 2], mw3[l // 2], mw2[l // 2])
    return x
```

```python
import functools
import math

import jax
import jax.numpy as jnp
from jax import lax
from jax.experimental import pallas as pl
from jax.experimental.pallas import tpu as pltpu

BF16 = jnp.bfloat16
F32 = jnp.float32

D_MODEL = 1024
MLA_HEADS = 8
MLA_NOPE = 64
MLA_ROPE = 32
MLA_V = 64
MLA_Q_LORA = 384
MLA_KV_LORA = 256
ROPE_BASE = 10000.0
DIFF_HEADS = 8
DIFF_HD = 64
DIFF_VD = 2 * DIFF_HD
N_EXPERTS = 8
TOP_K = 2
LN_EPS = 1e-5
RMS_EPS = 1e-6

HEAD_PAD = 128
ONES_ROWS = 16
LOG2E = 1.4426950408889634
MLA_QSCALE = (MLA_NOPE + MLA_ROPE) ** -0.5 * LOG2E
DIFF_QSCALE = DIFF_HD ** -0.5 * LOG2E

VMEM_LIMIT = 56 * 1024 * 1024
VMEM_LIMIT_ATTN = 60 * 1024 * 1024

TS_PROJ = 512
TQ_ATTN = 256
TS_MIX = 1024
TS_FFN = 512
TM_MOE = 256
TS_COMB = 1024
ROW_CHUNKS = 4
ROW_CHUNKS_ROUTE = 2
ROW_CHUNKS_FFN = 2


def _cparams(n_axes):
    return pltpu.CompilerParams(dimension_semantics=("arbitrary",) * n_axes,
                                vmem_limit_bytes=VMEM_LIMIT)


def _const_spec(shape):
    nd = len(shape)
    return pl.BlockSpec(shape, lambda *_: (0,) * nd, pipeline_mode=pl.Buffered(1))


def _dot(a, b):
    return jnp.dot(a, b, preferred_element_type=F32)


def _dot_nt(a, b):
    return lax.dot_general(a, b, (((1,), (1,)), ((), ())), preferred_element_type=F32)


def _sigmoid(v):
    return 1.0 / (1.0 + jnp.exp(-v))


def _layernorm(r, g, b):
    mu = jnp.mean(r, axis=-1, keepdims=True)
    d = r - mu
    var = jnp.mean(d * d, axis=-1, keepdims=True)
    return d * lax.rsqrt(var + LN_EPS) * g + b


def _rms_rows(v, g):
    ms = jnp.mean(v * v, axis=-1, keepdims=True)
    return v * lax.rsqrt(ms + RMS_EPS) * g


LANES = 128
ROW_TILE = D_MODEL // LANES


def _store_row_tiles(ref, v, lead=(), row0=0):
    n = v.shape[0]
    for c in range(ROW_TILE):
        ref[lead + (pl.ds(row0 * ROW_TILE + c, n, stride=ROW_TILE), slice(None))] = v[:, c * LANES:(c + 1) * LANES]


def _load_row_tiles(ref, n, lead=()):
    return jnp.concatenate(
        [ref[lead + (pl.ds(c, n, stride=ROW_TILE), slice(None))] for c in range(ROW_TILE)], axis=1)


def _ada_kernel(c_ref, w_ref, b_ref, o_ref):
    c = c_ref[...]
    cond = c * _sigmoid(c)
    o_ref[0] = _dot(cond.astype(BF16), w_ref[0].astype(BF16)) + b_ref[0]


def _ada_mod(c, w_ada, b_ada):
    n_l, d, n6 = w_ada.shape
    b = c.shape[0]
    tn = 1536
    return pl.pallas_call(
        _ada_kernel,
        grid=(n_l, n6 // tn),
        in_specs=[
            pl.BlockSpec((b, d), lambda l, j: (0, 0)),
            pl.BlockSpec((1, d, tn), lambda l, j: (l, 0, j)),
            pl.BlockSpec((1, 1, tn), lambda l, j: (l, 0, j)),
        ],
        out_specs=pl.BlockSpec((1, b, tn), lambda l, j: (l, 0, j)),
        out_shape=jax.ShapeDtypeStruct((n_l, b, n6), F32),
        compiler_params=_cparams(2),
        name="ada_mod",
    )(c, w_ada, b_ada.reshape(n_l, 1, n6))


def _inproj_kernel(x_ref, sc_ref, sh_ref, ct_ref, st_ref,
                   wlat_ref, wdq_ref, wdk_ref, wdv_ref, wga_ref, wgb_ref,
                   qg_ref, kvg_ref, wq_ref, wkn_ref, wv_ref, esel_ref,
                   qm_ref, km_ref, vt_ref, dq1_ref, dq2_ref, dk_ref, dvt_ref, ga_ref, gb_ref):
    x = x_ref[0]
    h = (x * (1.0 + sc_ref[0]) + sh_ref[0]).astype(BF16)
    half = MLA_ROPE // 2
    ct = ct_ref[0]
    st = st_ref[0]
    lane = lax.broadcasted_iota(jnp.int32, st.shape, 1)
    sa = jnp.where(lane < MLA_NOPE + half, st, 0.0)
    sb = st - sa

    def rotate(v, c, s_first, s_second):
        n = v.shape[1]
        return v * c + pltpu.roll(v, n - half, axis=1) * s_first + pltpu.roll(v, half, axis=1) * s_second

    lat = _dot(h, wlat_ref[...])
    q_lat = lat[:, :MLA_Q_LORA]
    kv_lat = lat[:, MLA_Q_LORA:MLA_Q_LORA + MLA_KV_LORA]
    kr = lat[:, MLA_Q_LORA + MLA_KV_LORA:]

    qn = _rms_rows(q_lat, qg_ref[...]).astype(BF16)
    q = _dot(qn, wq_ref[...])
    tile8 = lambda t: jnp.concatenate([t] * MLA_HEADS, axis=1)
    qr = (rotate(q, tile8(ct), tile8(sa), tile8(sb)) * MLA_QSCALE).astype(BF16)
    for hd in range(MLA_HEADS):
        qm_ref[0, hd] = qr[:, hd * HEAD_PAD:(hd + 1) * HEAD_PAD]

    kvn = _rms_rows(kv_lat, kvg_ref[...]).astype(BF16)
    kn = _dot(kvn, wkn_ref[...])
    to_front = lambda t: pltpu.roll(t, HEAD_PAD - MLA_NOPE, axis=1)
    kro = rotate(kr, to_front(ct), to_front(sa), to_front(sb)).astype(BF16)
    kcat = (kn + _dot(kro, esel_ref[...])).astype(BF16)
    for hd in range(MLA_HEADS):
        km_ref[0, hd] = kcat[:, hd * HEAD_PAD:(hd + 1) * HEAD_PAD]
    v = _dot(kvn, wv_ref[...])
    vt = v.T.astype(BF16)
    ones = jnp.ones((ONES_ROWS, vt.shape[1]), BF16)
    for hd in range(MLA_HEADS):
        vt_ref[0, hd, :MLA_V, :] = vt[hd * MLA_V:(hd + 1) * MLA_V, :]
        vt_ref[0, hd, MLA_V:, :] = ones

    dq = _dot(h, wdq_ref[...]) * DIFF_QSCALE
    lane = lax.broadcasted_iota(jnp.int32, dq.shape, 1)
    first = (lane % HEAD_PAD) < DIFF_HD
    dq1 = jnp.where(first, dq, 0.0).astype(BF16)
    dq2 = jnp.where(first, 0.0, dq).astype(BF16)
    dk = _dot(h, wdk_ref[...]).astype(BF16)
    dvt = _dot(h, wdv_ref[...]).T.astype(BF16)
    for hd in range(DIFF_HEADS):
        sl = slice(hd * HEAD_PAD, (hd + 1) * HEAD_PAD)
        dq1_ref[0, hd] = dq1[:, sl]
        dq2_ref[0, hd] = dq2[:, sl]
        dk_ref[0, hd] = dk[:, sl]
        dvt_ref[0, hd, :DIFF_VD, :] = dvt[sl, :]
        dvt_ref[0, hd, DIFF_VD:, :] = ones

    ga_ref[0] = _sigmoid(_dot(h, wga_ref[...])).astype(BF16)
    gb_ref[0] = _sigmoid(_dot(h, wgb_ref[...])).astype(BF16)


def _inproj(x, sc1, sh1, ct, st, w):
    b, s, d = x.shape
    ts = min(TS_PROJ, s)
    hh = MLA_HEADS
    row = lambda i, j: (i, j, 0)
    bat = lambda i, j: (i, 0, 0)
    head_rows = pl.BlockSpec((1, hh, ts, HEAD_PAD), lambda i, j: (i, 0, j, 0))
    weights = [w["wlat"], w["wdq"], w["wdk"], w["wdv"], w["wga"], w["wgb"],
               w["qg"], w["kvg"], w["wq"], w["wkn"], w["wv"], w["esel"]]
    head_shape = jax.ShapeDtypeStruct((b, hh, s, HEAD_PAD), BF16)
    return pl.pallas_call(
        _inproj_kernel,
        grid=(b, s // ts),
        in_specs=[
            pl.BlockSpec((1, ts, d), row),
            pl.BlockSpec((1, 1, d), bat),
            pl.BlockSpec((1, 1, d), bat),
            pl.BlockSpec((1, ts, HEAD_PAD), row),
            pl.BlockSpec((1, ts, HEAD_PAD), row),
        ] + [_const_spec(a.shape) for a in weights],
        out_specs=[
            head_rows, head_rows,
            pl.BlockSpec((1, hh, MLA_V + ONES_ROWS, ts), lambda i, j: (i, 0, 0, j)),
            head_rows, head_rows, head_rows,
            pl.BlockSpec((1, hh, DIFF_VD + ONES_ROWS, ts), lambda i, j: (i, 0, 0, j)),
            pl.BlockSpec((1, ts, d), row),
            pl.BlockSpec((1, ts, d), row),
        ],
        out_shape=[
            head_shape, head_shape,
            jax.ShapeDtypeStruct((b, hh, MLA_V + ONES_ROWS, s), BF16),
            head_shape, head_shape, head_shape,
            jax.ShapeDtypeStruct((b, hh, DIFF_VD + ONES_ROWS, s), BF16),
            jax.ShapeDtypeStruct((b, s, d), BF16),
            jax.ShapeDtypeStruct((b, s, d), BF16),
        ],
        compiler_params=_cparams(2),
        name="inproj",
    )(x, sc1, sh1, ct, st, *weights)


def _attn_kernel(lam_init, mq_ref, mk_ref, mvt_ref, q1_ref, q2_ref, k_ref, vt_ref, pk_ref, pq_ref,
                 slope_ref, lq1_ref, lk1_ref, lq2_ref, lk2_ref, g_ref, mo_ref, o_ref,
                 macc_ref, acc_ref, dist_ref, za_ref, zb_ref, ya_ref, yb_ref):
    dist_ref[...] = jnp.abs(pk_ref[0] - pq_ref[0])
    lam = (jnp.exp(jnp.sum(lq1_ref[...] * lk1_ref[...], axis=1, keepdims=True))
           - jnp.exp(jnp.sum(lq2_ref[...] * lk2_ref[...], axis=1, keepdims=True))
           + lam_init)

    def diff_scores(hd, z_ref):
        k = k_ref[0, hd]
        bias = slope_ref[hd] * dist_ref[...]
        z1 = _dot_nt(k, q1_ref[0, hd]) - bias
        z2 = _dot_nt(k, q2_ref[0, hd]) - bias
        z_ref[0] = z1
        z_ref[1] = z2
        return jnp.max(z1, axis=0, keepdims=True), jnp.max(z2, axis=0, keepdims=True)

    def diff_values(hd, z_ref, m1, m2):
        e1 = jnp.exp2(z_ref[0] - m1).astype(BF16)
        e2 = jnp.exp2(z_ref[1] - m2).astype(BF16)
        o1 = _dot(vt_ref[0, hd], e1)
        o2 = _dot(vt_ref[0, hd], e2)
        r1 = 1.0 / o1[DIFF_VD:DIFF_VD + 1]
        r2 = lam / o2[DIFF_VD:DIFF_VD + 1]
        o_t = o1[:DIFF_VD] * r1 - o2[:DIFF_VD] * r2
        ms = jnp.mean(o_t * o_t, axis=0, keepdims=True)
        o_t = o_t * lax.rsqrt(ms + RMS_EPS) * g_ref[hd] * (1.0 - lam_init)
        acc_ref[pl.ds(pl.multiple_of(hd * DIFF_VD, DIFF_VD), DIFF_VD), :] = o_t

    def mla_scores(hd, y_ref):
        s_t = _dot_nt(mk_ref[0, hd], mq_ref[0, hd])
        y_ref[...] = s_t
        return jnp.max(s_t, axis=0, keepdims=True)

    def mla_values(hd, y_ref, m):
        p = jnp.exp2(y_ref[...] - m).astype(BF16)
        o_t = _dot(mvt_ref[0, hd], p)
        macc_ref[pl.ds(pl.multiple_of(hd * MLA_V, MLA_V), MLA_V), :] = o_t[:MLA_V] / o_t[MLA_V:MLA_V + 1]

    def scores(hd, z_ref, y_ref, mla_first=False):
        if mla_first:
            m = mla_scores(hd, y_ref)
            return diff_scores(hd, z_ref), m
        return diff_scores(hd, z_ref), mla_scores(hd, y_ref)

    def values(hd, z_ref, y_ref, m, mla_first):
        if mla_first:
            mla_values(hd, y_ref, m[1])
        diff_values(hd, z_ref, *m[0])
        if not mla_first:
            mla_values(hd, y_ref, m[1])

    def body(i, ma):
        h0 = 2 * i
        mb = scores(h0 + 1, zb_ref, yb_ref)
        values(h0, za_ref, ya_ref, ma, True)
        ma = scores(h0 + 2, za_ref, ya_ref)
        values(h0 + 1, zb_ref, yb_ref, mb, True)
        return ma

    last = DIFF_HEADS - 1
    ma = lax.fori_loop(0, DIFF_HEADS // 2 - 1, body, scores(0, za_ref, ya_ref, mla_first=True))
    mb = scores(last, zb_ref, yb_ref)
    values(last - 1, za_ref, ya_ref, ma, False)
    values(last, zb_ref, yb_ref, mb, False)
    o_ref[0] = acc_ref[...].T.astype(BF16)
    mo_ref[0] = macc_ref[...].T.astype(BF16)


def _attention(qm, km, vt, dq1, dq2, dk, dvt, pos_k, pos_q, slopes, lq1, lk1, lq2, lk2, gcol, lam_init):
    assert MLA_HEADS == DIFF_HEADS
    b, hh, s, _ = dk.shape
    tq = min(TQ_ATTN, s)
    head_q = pl.BlockSpec((1, hh, tq, HEAD_PAD), lambda i, j: (i, 0, j, 0))
    head_k = pl.BlockSpec((1, hh, s, HEAD_PAD), lambda i, j: (i, 0, 0, 0))
    vec = pl.BlockSpec((1, DIFF_HD), lambda i, j: (0, 0))
    return pl.pallas_call(
        functools.partial(_attn_kernel, lam_init),
        grid=(b, s // tq),
        in_specs=[
            head_q, head_k,
            pl.BlockSpec((1, hh, MLA_V + ONES_ROWS, s), lambda i, j: (i, 0, 0, 0)),
            head_q, head_q, head_k,
            pl.BlockSpec((1, hh, DIFF_VD + ONES_ROWS, s), lambda i, j: (i, 0, 0, 0)),
            pl.BlockSpec((1, s, 1), lambda i, j: (i, 0, 0)),
            pl.BlockSpec((1, 1, tq), lambda i, j: (i, 0, j)),
            pl.BlockSpec(memory_space=pltpu.SMEM),
            vec, vec, vec, vec,
            pl.BlockSpec((hh, DIFF_VD, 1), lambda i, j: (0, 0, 0)),
        ],
        out_specs=[pl.BlockSpec((1, tq, hh * MLA_V), lambda i, j: (i, j, 0)),
                   pl.BlockSpec((1, tq, hh * DIFF_VD), lambda i, j: (i, j, 0))],
        out_shape=[jax.ShapeDtypeStruct((b, s, hh * MLA_V), BF16),
                   jax.ShapeDtypeStruct((b, s, hh * DIFF_VD), BF16)],
        scratch_shapes=[pltpu.VMEM((hh * MLA_V, tq), F32), pltpu.VMEM((hh * DIFF_VD, tq), F32),
                        pltpu.VMEM((s, tq), F32),
                        pltpu.VMEM((2, s, tq), F32), pltpu.VMEM((2, s, tq), F32),
                        pltpu.VMEM((s, tq), F32), pltpu.VMEM((s, tq), F32)],
        compiler_params=pltpu.CompilerParams(dimension_semantics=("arbitrary", "arbitrary"),
                                             vmem_limit_bytes=VMEM_LIMIT_ATTN),
        name="attention",
    )(qm, km, vt, dq1, dq2, dk, dvt, pos_k, pos_q, slopes, lq1, lk1, lq2, lk2, gcol)


def _route_top2(h, rw, rb):
    logits = _dot(h.astype(BF16), rw) + rb
    lane = lax.broadcasted_iota(jnp.int32, logits.shape, 1)
    lane_f = lane.astype(F32)
    neg = jnp.float32(-jnp.inf)
    lg = jnp.where(lane < N_EXPERTS, logits, neg)
    m1 = jnp.max(lg, axis=1, keepdims=True)
    i1 = jnp.min(jnp.where(lg == m1, lane_f, 128.0), axis=1, keepdims=True)
    lg2 = jnp.where(lane_f == i1, neg, lg)
    m2 = jnp.max(lg2, axis=1, keepdims=True)
    i2 = jnp.min(jnp.where(lg2 == m2, lane_f, 128.0), axis=1, keepdims=True)
    t = jnp.exp(m2 - m1)
    den = 1.0 + t
    ei = jnp.where(lane == 0, i1, jnp.where(lane == 1, i2, 0.0)).astype(jnp.int32)
    gi = jnp.where(lane == 0, 1.0 / den, jnp.where(lane == 1, t / den, 0.0))
    return ei, gi


def _mix_kernel(alpha, route, mo_ref, do_ref, ga_ref, gb_ref, x_ref, g1_ref, lng_ref, lnb_ref,
                wbm_ref, wbd_ref, wo_ref, *rest):
    if route:
        sc_ref, sh_ref, rw_ref, rb_ref, o_ref, h_ref, ei_ref, gi_ref = rest
    else:
        (o_ref,) = rest
    n_chunks = ROW_CHUNKS_ROUTE if route else ROW_CHUNKS
    ch = x_ref.shape[1] // n_chunks
    for c in range(n_chunks):
        rows = pl.ds(c * ch, ch)
        ya = _dot(mo_ref[0, rows, :], wbm_ref[...])
        yb = _dot(do_ref[0, rows, :], wbd_ref[...])
        gated = (ga_ref[0, rows, :].astype(F32) * ya + gb_ref[0, rows, :].astype(F32) * yb).astype(BF16)
        mix = _dot(gated, wo_ref[...])
        r = alpha * x_ref[0, rows, :] + g1_ref[0] * mix
        x1 = _layernorm(r, lng_ref[...], lnb_ref[...])
        o_ref[0, rows, :] = x1
        if route:
            h = x1 * (1.0 + sc_ref[0]) + sh_ref[0]
            _store_row_tiles(h_ref, h, row0=c * ch)
            ei_ref[0, rows, :], gi_ref[0, rows, :] = _route_top2(h, rw_ref[...], rb_ref[...])


def _mix(alpha, mla_o, diff_o, ga, gb, x, g1, lng, lnb, wbm, wbd, wo, router=None):
    b, s, d = x.shape
    ts = min(TS_MIX, s)
    nj = s // ts
    row = lambda i, j: (i, j, 0)
    bat = lambda i, j: (i, 0, 0)
    in_specs = [
        pl.BlockSpec((1, ts, mla_o.shape[-1]), row),
        pl.BlockSpec((1, ts, diff_o.shape[-1]), row),
        pl.BlockSpec((1, ts, d), row),
        pl.BlockSpec((1, ts, d), row),
        pl.BlockSpec((1, ts, d), row),
        pl.BlockSpec((1, 1, d), bat),
        _const_spec(lng.shape), _const_spec(lnb.shape),
        _const_spec(wbm.shape), _const_spec(wbd.shape), _const_spec(wo.shape),
    ]
    out_specs = [pl.BlockSpec((1, ts, d), row)]
    out_shape = [jax.ShapeDtypeStruct((b, s, d), F32)]
    args = [mla_o, diff_o, ga, gb, x, g1, lng, lnb, wbm, wbd, wo]
    if router is not None:
        sc2, sh2, rw, rb = router
        in_specs += [pl.BlockSpec((1, 1, d), bat), pl.BlockSpec((1, 1, d), bat),
                     _const_spec(rw.shape), _const_spec(rb.shape)]
        args += [sc2, sh2, rw, rb]
        out_specs += [pl.BlockSpec((ts * ROW_TILE, LANES), lambda i, j: (i * nj + j, 0)),
                      pl.BlockSpec((1, ts, HEAD_PAD), row), pl.BlockSpec((1, ts, HEAD_PAD), row)]
        out_shape += [jax.ShapeDtypeStruct((b * s * ROW_TILE, LANES), F32),
                      jax.ShapeDtypeStruct((b, s, HEAD_PAD), jnp.int32),
                      jax.ShapeDtypeStruct((b, s, HEAD_PAD), F32)]
    return pl.pallas_call(
        functools.partial(_mix_kernel, alpha, router is not None),
        grid=(b, nj),
        in_specs=in_specs,
        out_specs=out_specs,
        out_shape=out_shape,
        compiler_params=_cparams(2),
        name="mix_ln1",
    )(*args)


def _swiglu(h, w1, w3, w2):
    a = _dot(h, w1)
    bgate = _dot(h, w3)
    u = (a * _sigmoid(a) * bgate).astype(BF16)
    return _dot(u, w2)


def _ffn_kernel(alpha, x_ref, sc_ref, sh_ref, g2_ref, lng_ref, lnb_ref, w1_ref, w3_ref, w2_ref, o_ref):
    ch = x_ref.shape[1] // ROW_CHUNKS_FFN
    for c in range(ROW_CHUNKS_FFN):
        rows = pl.ds(c * ch, ch)
        x = x_ref[0, rows, :]
        h = (x * (1.0 + sc_ref[0]) + sh_ref[0]).astype(BF16)
        f = _swiglu(h, w1_ref[...], w3_ref[...], w2_ref[...])
        r = alpha * x + g2_ref[0] * f
        o_ref[0, rows, :] = _layernorm(r, lng_ref[...], lnb_ref[...])


def _ffn(alpha, x, sc2, sh2, g2, lng, lnb, w1, w3, w2):
    b, s, d = x.shape
    ts = min(TS_FFN, s)
    row = lambda i, j: (i, j, 0)
    bat = lambda i, j: (i, 0, 0)
    return pl.pallas_call(
        functools.partial(_ffn_kernel, alpha),
        grid=(b, s // ts),
        in_specs=[
            pl.BlockSpec((1, ts, d), row),
            pl.BlockSpec((1, 1, d), bat), pl.BlockSpec((1, 1, d), bat), pl.BlockSpec((1, 1, d), bat),
            _const_spec(lng.shape), _const_spec(lnb.shape),
            _const_spec(w1.shape), _const_spec(w3.shape), _const_spec(w2.shape),
        ],
        out_specs=pl.BlockSpec((1, ts, d), row),
        out_shape=jax.ShapeDtypeStruct((b, s, d), F32),
        compiler_params=_cparams(2),
        name="ffn_ln2",
    )(x, sc2, sh2, g2, lng, lnb, w1, w3, w2)


def _experts_kernel(te_ref, nu_ref, idx_hbm, h_hbm, w1_ref, w3_ref, w2_ref, y_hbm,
                    idx_smem, xbuf, ybuf, sem_idx, sem_rows, sem_out):
    i = pl.program_id(0)
    n_used = nu_ref[0]
    n_tiles = pl.num_programs(0)
    tm = xbuf.shape[1] // ROW_TILE
    slot = lax.rem(i, 2)
    n_idx = idx_smem.shape[0]

    def index_copy(tile, step):
        return pltpu.make_async_copy(idx_hbm.at[tile], idx_smem.at[pl.ds(lax.rem(step, n_idx), 1)], sem_idx)

    def row_tile(r):
        start = r * ROW_TILE
        return pl.ds(start if isinstance(r, int) else pl.multiple_of(start, ROW_TILE), ROW_TILE)

    def gather_copy(step, r, buf=None):
        row = lax.rem(step, n_idx)
        buf = lax.rem(step, 2) if buf is None else buf
        return pltpu.make_async_copy(h_hbm.at[row_tile(idx_smem[row, r])],
                                     xbuf.at[buf, row_tile(r)], sem_rows.at[buf])

    def scatter_copy(step, r, buf=None):
        row = lax.rem(step, n_idx)
        buf = lax.rem(step, 2) if buf is None else buf
        return pltpu.make_async_copy(ybuf.at[buf, row_tile(r)],
                                     y_hbm.at[row_tile(idx_smem[row, tm + r])], sem_out.at[buf])

    def wait_gather(buf):
        pltpu.make_async_copy(h_hbm.at[pl.ds(0, tm * ROW_TILE)], xbuf.at[buf], sem_rows.at[buf]).wait()

    def wait_scatter(buf):
        pltpu.make_async_copy(ybuf.at[buf], y_hbm.at[pl.ds(0, tm * ROW_TILE)], sem_out.at[buf]).wait()

    @pl.when(i == 0)
    def _():
        for tile, step in ((0, 0), (jnp.minimum(1, n_tiles - 1), 1), (n_tiles, n_idx - 1)):
            index_copy(tile, step).start()
            index_copy(tile, step).wait()
        lax.fori_loop(0, tm, lambda r, c: (gather_copy(0, r).start(), c)[1], 0)
        ybuf[1] = jnp.zeros(ybuf.shape[1:], F32)

    def step(slot):
        wait_gather(slot)
        for r in range(tm):
            gather_copy(i + 1, r, 1 - slot).start()
        nxt2 = jnp.minimum(i + 2, n_tiles - 1)
        index_copy(nxt2, i + 2).start()
        for r in range(tm):
            scatter_copy(i + n_idx - 1, r, 1 - slot).start()
        x = _load_row_tiles(xbuf, tm, (slot,))
        _store_row_tiles(ybuf, _swiglu(x.astype(BF16), w1_ref[0], w3_ref[0], w2_ref[0]), (slot,))
        wait_scatter(1 - slot)
        index_copy(nxt2, i + 2).wait()

    for s in (0, 1):
        pl.when((i < n_used) & (slot == s))(functools.partial(step, s))

    @pl.when(i + 1 == n_used)
    def _():
        lax.fori_loop(0, tm, lambda r, c: (scatter_copy(i, r).start(), c)[1], 0)
        wait_scatter(slot)
        wait_gather(1 - slot)


def _experts(tile_expert, n_used, idx, h2, w1, w3, w2):
    n_tiles, tm2 = idx.shape[0] - 1, idx.shape[-1]
    tm = tm2 // 2
    t = h2.shape[0] // ROW_TILE
    _, d, f = w1.shape
    grid_spec = pltpu.PrefetchScalarGridSpec(
        num_scalar_prefetch=2,
        grid=(n_tiles,),
        in_specs=[
            pl.BlockSpec(memory_space=pl.ANY),
            pl.BlockSpec(memory_space=pl.ANY),
            pl.BlockSpec((1, d, f), lambda i, te, nu: (te[i], 0, 0)),
            pl.BlockSpec((1, d, f), lambda i, te, nu: (te[i], 0, 0)),
            pl.BlockSpec((1, f, d), lambda i, te, nu: (te[i], 0, 0)),
        ],
        out_specs=pl.BlockSpec(memory_space=pl.ANY),
        scratch_shapes=[
            pltpu.SMEM((4, 2 * tm), jnp.int32),
            pltpu.VMEM((2, tm * ROW_TILE, LANES), F32),
            pltpu.VMEM((2, tm * ROW_TILE, LANES), F32),
            pltpu.SemaphoreType.DMA(()),
            pltpu.SemaphoreType.DMA((2,)),
            pltpu.SemaphoreType.DMA((2,)),
        ],
    )
    return pl.pallas_call(
        _experts_kernel,
        grid_spec=grid_spec,
        out_shape=jax.ShapeDtypeStruct(((TOP_K * t + tm) * ROW_TILE, LANES), F32),
        compiler_params=_cparams(1),
        name="moe_experts",
    )(tile_expert, n_used, idx, h2, w1, w3, w2)


def _combine_kernel(alpha, y0_ref, y1_ref, gi_ref, x_ref, g2_ref, lng_ref, lnb_ref, o_ref):
    gi = gi_ref[0]
    n = gi.shape[0]
    f = gi[:, 0:1] * _load_row_tiles(y0_ref, n) + gi[:, 1:2] * _load_row_tiles(y1_ref, n)
    r = alpha * x_ref[0] + g2_ref[0] * f
    o_ref[0] = _layernorm(r, lng_ref[...], lnb_ref[...])


def _combine(alpha, y, gi, x, g2, lng, lnb):
    b, s, d = x.shape
    ts = min(TS_COMB, s)
    nj = s // ts
    row = lambda i, j: (i, j, 0)
    return pl.pallas_call(
        functools.partial(_combine_kernel, alpha),
        grid=(b, nj),
        in_specs=[
            pl.BlockSpec((ts * ROW_TILE, LANES), lambda i, j: (i * nj + j, 0)),
            pl.BlockSpec((ts * ROW_TILE, LANES), lambda i, j: (b * nj + i * nj + j, 0)),
            pl.BlockSpec((1, ts, HEAD_PAD), row),
            pl.BlockSpec((1, ts, d), row),
            pl.BlockSpec((1, 1, d), lambda i, j: (i, 0, 0)),
            _const_spec(lng.shape), _const_spec(lnb.shape),
        ],
        out_specs=pl.BlockSpec((1, ts, d), row),
        out_shape=jax.ShapeDtypeStruct((b, s, d), F32),
        compiler_params=_cparams(2),
        name="moe_combine_ln2",
    )(y, y, gi, x, g2, lng, lnb)


def _moe_plan(expert_idx, tm):
    t = expert_idx.shape[0]
    n_slots = t * TOP_K
    flat = expert_idx.reshape(n_slots)
    counts = jnp.sum((flat[:, None] == jnp.arange(N_EXPERTS, dtype=jnp.int32)[None, :]).astype(jnp.int32), axis=0)
    padded = (counts + tm - 1) // tm * tm
    pends = jnp.cumsum(padded)
    pstarts = pends - padded
    starts = jnp.cumsum(counts) - counts
    n_tiles = (n_slots + N_EXPERTS * tm) // tm
    tile_start = jnp.arange(n_tiles, dtype=jnp.int32) * tm
    tile_expert = jnp.minimum(jnp.sum((tile_start[:, None] >= pends[None, :]).astype(jnp.int32), axis=1),
                              N_EXPERTS - 1)
    n_used = (pends[-1] // tm).astype(jnp.int32).reshape(1)
    order = jnp.argsort(flat, stable=True).astype(jnp.int32)
    n_rows = n_tiles * tm
    rows = jnp.arange(n_rows, dtype=jnp.int32)
    order_pad = jnp.concatenate([order, jnp.zeros((n_rows - n_slots,), jnp.int32)])
    slot_of_row = jnp.full((n_rows,), -1, jnp.int32)
    for e in range(N_EXPERTS):
        shifted = jnp.roll(order_pad, pstarts[e] - starts[e])
        slot_of_row = jnp.where((rows >= pstarts[e]) & (rows < pstarts[e] + counts[e]), shifted, slot_of_row)
    valid = slot_of_row >= 0
    tok = slot_of_row // TOP_K
    src = jnp.where(valid, tok, 0)
    dst = jnp.where(valid, (slot_of_row % TOP_K) * t + tok, n_slots + rows % tm)
    idx = jnp.concatenate([src.reshape(n_tiles, 1, tm), dst.reshape(n_tiles, 1, tm)], axis=-1)
    dummy = jnp.concatenate([jnp.zeros((1, 1, tm), jnp.int32),
                             (n_slots + jnp.arange(tm, dtype=jnp.int32)).reshape(1, 1, tm)], axis=-1)
    return tile_expert, n_used, jnp.concatenate([idx.astype(jnp.int32), dummy], axis=0)


def _moe(alpha, x, h2, ei, gi, g2, lng, lnb, w1, w3, w2):
    b, s, d = x.shape
    t = b * s
    tm = min(TM_MOE, t)
    tile_expert, n_used, idx = _moe_plan(ei.reshape(t, HEAD_PAD)[:, :TOP_K], tm)
    y = _experts(tile_expert, n_used, idx, h2, w1, w3, w2)
    return _combine(alpha, y, gi, x, g2, lng, lnb)


def _prep_layer_weights(w_in, w_q_up, w_kv_up, q_norm_g, kv_norm_g):
    n_l, d, _ = w_in.shape
    z96 = jnp.zeros((n_l, d, HEAD_PAD - MLA_ROPE), w_in.dtype)
    o = MLA_Q_LORA + MLA_KV_LORA + MLA_ROPE
    wlat = jnp.concatenate([w_in[:, :, :o], z96], axis=-1)
    nd = DIFF_HEADS * 2 * DIFF_HD
    wdq = w_in[:, :, o:o + nd]
    wdk = w_in[:, :, o + nd:o + 2 * nd]
    wdv = w_in[:, :, o + 2 * nd:o + 3 * nd]
    wga = w_in[:, :, o + 3 * nd:o + 3 * nd + d]
    wgb = w_in[:, :, o + 3 * nd + d:o + 3 * nd + 2 * d]

    hq = MLA_NOPE + MLA_ROPE
    wq4 = w_q_up.reshape(n_l, MLA_Q_LORA, MLA_HEADS, hq)
    zq = jnp.zeros((n_l, MLA_Q_LORA, MLA_HEADS, HEAD_PAD - hq), w_q_up.dtype)
    wq = jnp.concatenate([wq4, zq], axis=-1).reshape(n_l, MLA_Q_LORA, MLA_HEADS * HEAD_PAD)

    wkv4 = w_kv_up.reshape(n_l, MLA_KV_LORA, MLA_HEADS, MLA_NOPE + MLA_V)
    wkn = jnp.concatenate([wkv4[..., :MLA_NOPE], jnp.zeros_like(wkv4[..., :HEAD_PAD - MLA_NOPE])], axis=-1)
    wkn = wkn.reshape(n_l, MLA_KV_LORA, MLA_HEADS * HEAD_PAD)
    wv = wkv4[..., MLA_NOPE:].reshape(n_l, MLA_KV_LORA, MLA_HEADS * MLA_V)

    rr = jnp.arange(HEAD_PAD)[:, None]
    cc = jnp.arange(MLA_HEADS * HEAD_PAD)[None, :]
    esel = ((rr < MLA_ROPE) & (cc % HEAD_PAD == MLA_NOPE + rr)).astype(BF16)

    cast = lambda a: a.astype(BF16)
    return dict(wlat=cast(wlat), wdq=cast(wdq), wdk=cast(wdk), wdv=cast(wdv), wga=cast(wga), wgb=cast(wgb),
                wq=cast(wq), wkn=cast(wkn), wv=cast(wv), esel=esel,
                qg=q_norm_g.reshape(n_l, 1, MLA_Q_LORA), kvg=kv_norm_g.reshape(n_l, 1, MLA_KV_LORA))


def _rope_tables(positions):
    inv_freq = ROPE_BASE ** (-jnp.arange(0, MLA_ROPE, 2, dtype=F32) / MLA_ROPE)
    ang = positions.astype(F32)[..., None] * inv_freq
    cos, sin = jnp.cos(ang), jnp.sin(ang)
    ones = jnp.ones(positions.shape + (MLA_NOPE,), F32)
    tail = HEAD_PAD - MLA_NOPE - MLA_ROPE
    ct = jnp.concatenate([ones, cos, cos, ones[..., :tail]], axis=-1)
    st = jnp.concatenate([0.0 * ones, -sin, sin, 0.0 * ones[..., :tail]], axis=-1)
    return ct, st


def kernel(x, c, positions, w_ada, b_ada, w_in, q_norm_g, w_q_up, kv_norm_g, w_kv_up, lambda_q1, lambda_k1, lambda_q2, lambda_k2, diff_norm_g, w_br_mla, w_br_diff, w_out, ln1_g, ln1_b, ln2_g, ln2_b, ffn_w1, ffn_w3, ffn_w2, router_w, router_b, moe_w1, moe_w3, moe_w2):
    b, s, d = x.shape
    depth = w_in.shape[0]
    alpha = (2.0 * depth) ** 0.25

    mod = _ada_mod(c, w_ada, b_ada)
    ct, st = _rope_tables(positions)
    posf = positions.astype(F32)
    pos_k = posf.reshape(b, s, 1)
    pos_q = posf.reshape(b, 1, s)
    slopes = (2.0 ** (-8.0 * jnp.arange(1, DIFF_HEADS + 1, dtype=F32) / DIFF_HEADS)) * LOG2E

    lw = _prep_layer_weights(w_in, w_q_up, w_kv_up, q_norm_g, kv_norm_g)
    wbm, wbd, wo = w_br_mla.astype(BF16), w_br_diff.astype(BF16), w_out.astype(BF16)
    fw1, fw3, fw2 = ffn_w1.astype(BF16), ffn_w3.astype(BF16), ffn_w2.astype(BF16)
    mw1, mw3, mw2 = moe_w1.astype(BF16), moe_w3.astype(BF16), moe_w2.astype(BF16)
    rw = jnp.pad(router_w, ((0, 0), (0, 0), (0, HEAD_PAD - N_EXPERTS))).astype(BF16)
    rb = jnp.pad(router_b, ((0, 0), (0, HEAD_PAD - N_EXPERTS))).reshape(-1, 1, HEAD_PAD)
    gcol = diff_norm_g.reshape(depth, DIFF_HEADS, DIFF_VD, 1)
    vec = lambda a, l: a[l].reshape(1, -1)

    for l in range(depth):
        sh1, sc1, g1, sh2, sc2, g2 = [m.reshape(b, 1, d) for m in jnp.split(mod[l], 6, axis=-1)]
        w_l = {k: v[l] if k != "esel" else v for k, v in lw.items()}
        qm, km, vt, dq1, dq2, dk, dvt, ga, gb = _inproj(x, sc1, sh1, ct, st, w_l)
        lam_init = 0.8 - 0.6 * math.exp(-0.3 * l)
        mla_o, diff_o = _attention(qm, km, vt, dq1, dq2, dk, dvt, pos_k, pos_q, slopes,
                                   vec(lambda_q1, l), vec(lambda_k1, l), vec(lambda_q2, l), vec(lambda_k2, l),
                                   gcol[l], lam_init)
        mix_args = (alpha, mla_o, diff_o, ga, gb, x, g1, vec(ln1_g, l), vec(ln1_b, l), wbm[l], wbd[l], wo[l])
        if l % 2 == 0:
            (x,) = _mix(*mix_args)
            x = _ffn(alpha, x, sc2, sh2, g2, vec(ln2_g, l), vec(ln2_b, l), fw1[l // 2], fw3[l // 2], fw2[l // 2])
        else:
            x, h2, ei, gi = _mix(*mix_args, router=(sc2, sh2, rw[l // 2], rb[l // 2]))
            x = _moe(alpha, x, h2, ei, gi, g2, vec(ln2_g, l), vec(ln2_b, l),
                     mw1[l // 2], mw3[l // 2], mw2[l // 2])
    return x
```
